```python
import math
import jax
import jax.numpy as jnp
from jax import lax
import numpy as np

D_MODEL = 1024
BATCH = 8
SEQ = 4096
DEPTH = 2

GRID_W = 64
CTX_LEN = 256

S5_WIDTH = D_MODEL // 4
S5_GROUP_CH = 16
S5_GROUPS = S5_WIDTH // S5_GROUP_CH
S5_STATE = 64
S5_MAX_RE = -1e-4

SSD_INNER = D_MODEL // 2
SSD_HEADDIM = 64
SSD_HEADS = SSD_INNER // SSD_HEADDIM
SSD_GROUPS = 2
SSD_HPG = SSD_HEADS // SSD_GROUPS
SSD_STATE = 128
SSD_CONV = 5
SSD_CHUNK = 128
SSD_XBC = SSD_INNER + 2 * SSD_GROUPS * SSD_STATE
SSD_IN = SSD_INNER + SSD_XBC + SSD_HEADS

HY_WIDTH = D_MODEL // 4
HY_ORDER = 2
HY_SHORT = 3
HY_BANDS = 16
HY_EMB = 1 + 2 * HY_BANDS
HY_FILTER_HID = 64
HY_IN = (HY_ORDER + 1) * HY_WIDTH

MIX_IN = S5_WIDTH + SSD_IN + HY_IN
MIX_OUT = S5_WIDTH + SSD_INNER + HY_WIDTH

FFN_HIDDEN = -(-8 * D_MODEL // (3 * 256)) * 256
ALPHA = (2 * DEPTH) ** 0.25
BETA = (8 * DEPTH) ** -0.25
LN_EPS = 1e-6

kernel_name = 'hymba_s5_ssd_hyena_deepnorm_dit'

F32 = jnp.float32


def _standardise(x):
    xf = x.astype(F32)
    mu = jnp.mean(xf, axis=-1, keepdims=True)
    var = jnp.mean(jnp.square(xf - mu), axis=-1, keepdims=True)
    return (xf - mu) * lax.rsqrt(var + LN_EPS)


def layer_norm(x, g, b):
    return (_standardise(x) * g + b).astype(x.dtype)


def modulate(x, shift, scale):
    return (_standardise(x) * (1 + scale) + shift).astype(x.dtype)


def dwconv_centred(x, w, b):
    y = lax.conv_general_dilated(x, w[:, None, :].astype(x.dtype), (1,), 'SAME',
                                 dimension_numbers=('NWC', 'WIO', 'NWC'),
                                 feature_group_count=x.shape[-1])
    return y + b


def _linear_recurrence(e1, e2):
    a1, b1 = e1
    a2, b2 = e2
    return a1 * a2, a2 * b1 + b2


def s5_discretise(lam_re, lam_im, log_step, b_re, b_im):
    lam = lax.complex(jnp.minimum(lam_re.astype(F32), S5_MAX_RE), lam_im.astype(F32))
    step = jnp.exp(log_step.astype(F32))[:, None]
    lam_bar = jnp.exp(lam * step)
    b_bar = ((lam_bar - 1.0) / lam)[..., None] * lax.complex(b_re.astype(F32), b_im.astype(F32))
    return lam_bar, b_bar


def s5_scan(u, lam_bar, b_bar, h0, reverse):
    bu = jnp.einsum('gph,blgh->blgp', b_bar, u.astype(jnp.complex64))
    edge = -1 if reverse else 0
    bu = bu.at[:, edge].add(lam_bar * h0)
    a = jnp.broadcast_to(lam_bar, bu.shape)
    _, h = lax.associative_scan(_linear_recurrence, (a, bu), reverse=reverse, axis=1)
    return h


def s5_mixer(u_ctx, u_lat, lam_re, lam_im, log_step, b_re, b_im, c_re, c_im, d_skip, w_glu, b_glu, want_ctx):
    def groups(u):
        return u.astype(F32).reshape(u.shape[0], u.shape[1], S5_GROUPS, S5_GROUP_CH)

    uc, ul = groups(u_ctx), groups(u_lat)
    h_init = jnp.zeros((ul.shape[0], S5_GROUPS, S5_STATE), jnp.complex64)
    d = d_skip.astype(F32).reshape(S5_GROUPS, S5_GROUP_CH)
    y_ctx = d * uc if want_ctx else None
    y_lat = d * ul
    for dirn, reverse in enumerate((False, True)):
        lam_bar, b_bar = s5_discretise(lam_re[dirn], lam_im[dirn], log_step[dirn], b_re[dirn], b_im[dirn])
        c_mat = lax.complex(c_re[dirn].astype(F32), c_im[dirn].astype(F32))
        h_ctx = s5_scan(uc, lam_bar, b_bar, h_init, reverse)
        h_carry = h_ctx[:, 0] if reverse else h_ctx[:, -1]
        h_lat = s5_scan(ul, lam_bar, b_bar, h_carry, reverse)
        y_lat = y_lat + jnp.real(jnp.einsum('ghp,blgp->blgh', c_mat, h_lat))
        if want_ctx:
            y_ctx = y_ctx + jnp.real(jnp.einsum('ghp,blgp->blgh', c_mat, h_ctx))

    def glu(y, like):
        y = jax.nn.gelu(y.reshape(y.shape[0], y.shape[1], S5_WIDTH))
        return (y * jax.nn.sigmoid(y @ w_glu + b_glu)).astype(like.dtype)

    return (glu(y_ctx, u_ctx) if want_ctx else None), glu(y_lat, u_lat)


def segsum(a):
    T = a.shape[-1]
    a_rep = jnp.broadcast_to(a[..., None], a.shape + (T,))
    a_rep = jnp.where(jnp.tril(jnp.ones((T, T), bool), -1), a_rep, 0.0)
    ss = jnp.cumsum(a_rep, axis=-2)
    return jnp.where(jnp.tril(jnp.ones((T, T), bool)), ss, -jnp.inf)


def ssd_chunked(xdt, a, b, c, h0, want_y):
    bsz, L = xdt.shape[:2]
    nc = L // SSD_CHUNK
    xdt = xdt.reshape(bsz, nc, SSD_CHUNK, SSD_GROUPS, SSD_HPG, SSD_HEADDIM)
    b = b.reshape(bsz, nc, SSD_CHUNK, SSD_GROUPS, SSD_STATE)
    c = c.reshape(bsz, nc, SSD_CHUNK, SSD_GROUPS, SSD_STATE)
    a = a.reshape(bsz, nc, SSD_CHUNK, SSD_GROUPS, SSD_HPG).transpose(0, 3, 4, 1, 2)
    a_cs = jnp.cumsum(a, axis=-1)
    decay_to_end = jnp.exp(a_cs[..., -1:] - a_cs)
    states = jnp.einsum('bclgn,bgecl,bclgep->bcgepn', b, decay_to_end, xdt)
    states = jnp.concatenate([h0[:, None], states], axis=1)
    chunk_a = jnp.pad(a_cs[..., -1], ((0, 0),) * 3 + ((1, 0),))
    chunk_decay = jnp.exp(segsum(chunk_a))
    states = jnp.einsum('bgezc,bcgepn->bzgepn', chunk_decay, states)
    h_last = states[:, -1]
    if not want_y:
        return None, h_last
    lmat = jnp.exp(segsum(a))
    cb = jnp.einsum('bclgn,bcsgn->bcgls', c, b)
    y_diag = jnp.einsum('bcgls,bgecls,bcsgep->bclgep', cb, lmat, xdt)
    y_off = jnp.einsum('bclgn,bcgepn,bgecl->bclgep', c, states[:, :-1], jnp.exp(a_cs))
    y = (y_diag + y_off).reshape(bsz, L, SSD_GROUPS, SSD_HPG, SSD_HEADDIM)
    return y, h_last


def ssd_direction(xs, bs, cs, dt_raw, dt_bias, a_log, h0, reverse, want_y):
    bsz, L = xs.shape[:2]
    dt = jax.nn.softplus(dt_raw + dt_bias.astype(F32)).reshape(bsz, L, SSD_GROUPS, SSD_HPG)
    a = -jnp.exp(a_log.astype(F32)).reshape(SSD_GROUPS, SSD_HPG) * dt
    xdt = xs * dt[..., None]
    if reverse:
        xdt, a, bs, cs = (jnp.flip(t, axis=1) for t in (xdt, a, bs, cs))
    y, h_last = ssd_chunked(xdt, a, bs, cs, h0, want_y)
    if reverse and want_y:
        y = jnp.flip(y, axis=1)
    return y, h_last


def ssd_mixer(p_ctx, p_lat, conv_w, conv_b, dt_bias, a_log, d_skip, norm_w, want_ctx):
    def prep(p):
        bsz, L = p.shape[:2]
        z, xbc, dt = jnp.split(p, [SSD_INNER, SSD_INNER + SSD_XBC], axis=-1)
        xbc = jax.nn.silu(dwconv_centred(xbc, conv_w, conv_b)).astype(F32)
        xs, bs, cs = jnp.split(xbc, [SSD_INNER, SSD_INNER + SSD_GROUPS * SSD_STATE], axis=-1)
        return (z,
                xs.reshape(bsz, L, SSD_GROUPS, SSD_HPG, SSD_HEADDIM),
                bs.reshape(bsz, L, SSD_GROUPS, SSD_STATE),
                cs.reshape(bsz, L, SSD_GROUPS, SSD_STATE),
                dt.astype(F32))

    zc, xc, bc, cc, dtc = prep(p_ctx)
    zl, xl, bl, cl, dtl = prep(p_lat)
    d = d_skip.astype(F32).reshape(SSD_GROUPS, SSD_HPG, 1)
    h_init = jnp.zeros((xl.shape[0], SSD_GROUPS, SSD_HPG, SSD_HEADDIM, SSD_STATE), F32)
    y_ctx = d * xc if want_ctx else None
    y_lat = d * xl
    for dirn, reverse in enumerate((False, True)):
        yc, h_ctx = ssd_direction(xc, bc, cc, dtc, dt_bias[dirn], a_log[dirn], h_init, reverse, want_ctx)
        yl, _ = ssd_direction(xl, bl, cl, dtl, dt_bias[dirn], a_log[dirn], h_ctx, reverse, True)
        y_lat = y_lat + yl
        if want_ctx:
            y_ctx = y_ctx + yc

    def gated_rmsnorm(y, z):
        bsz, L = z.shape[:2]
        g = y.reshape(bsz, L, SSD_INNER) * jax.nn.silu(z.astype(F32))
        g = g * lax.rsqrt(jnp.mean(jnp.square(g), axis=-1, keepdims=True) + LN_EPS) * norm_w
        return g.astype(z.dtype)

    return (gated_rmsnorm(y_ctx, zc) if want_ctx else None), gated_rmsnorm(y_lat, zl)


def hyena_filter_spectrum(L, w1, b1, w2, b2, w3, freq, decay):
    pos = jnp.arange(L, dtype=F32)
    t = pos / max(L - 1, 1)
    bands = jnp.linspace(1e-4, HY_BANDS - 1, HY_BANDS, dtype=F32)
    ang = (2.0 * math.pi / L) * pos[:, None] * bands
    feats = jnp.concatenate([t[:, None], jnp.cos(ang), -jnp.sin(ang)], axis=-1)
    freq = freq.astype(F32)
    hid = jnp.sin(freq * (feats @ w1.astype(F32) + b1.astype(F32)))
    hid = jnp.sin(freq * (hid @ w2.astype(F32) + b2.astype(F32)))
    h = (hid @ w3.astype(F32)) * jnp.exp(-t[:, None] * jnp.abs(decay.astype(F32)))
    h = h.reshape(L, 2, HY_ORDER, HY_WIDTH)
    h_fwd, h_bwd = h[:, 0], h[:, 1]
    k = jnp.concatenate([h_fwd[:1] + h_bwd[:1], h_fwd[1:], jnp.zeros_like(h_fwd[:1]), h_bwd[:0:-1]], axis=0)
    k = k / jnp.sum(jnp.abs(k), axis=0, keepdims=True)
    return jnp.fft.rfft(k, axis=0)


def long_conv(u, k_spec, bias):
    L = u.shape[1]
    y = jnp.fft.irfft(jnp.fft.rfft(u, n=2 * L, axis=1) * k_spec, n=2 * L, axis=1)[:, :L]
    return y + u * bias.astype(F32)


def hyena_mixer(p, conv_w, conv_b, w1, b1, w2, b2, w3, freq, decay, bias):
    L = p.shape[1]
    q = dwconv_centred(p, conv_w, conv_b).astype(F32)
    v, x1, x2 = jnp.split(q, 3, axis=-1)
    k_spec = hyena_filter_spectrum(L, w1, b1, w2, b2, w3, freq, decay)
    z = x1 * long_conv(v, k_spec[:, 0], bias[0])
    z = x2 * long_conv(z, k_spec[:, 1], bias[1])
    return z.astype(p.dtype)


def mixer_block(h_ctx, h_lat, w_in, w_out, s5_p, ssd_p, hy_p, want_ctx):
    cuts = [S5_WIDTH, S5_WIDTH + SSD_IN]
    s5_c, ssd_c, hy_c = jnp.split(h_ctx @ w_in, cuts, axis=-1)
    s5_l, ssd_l, hy_l = jnp.split(h_lat @ w_in, cuts, axis=-1)
    y_s5_c, y_s5_l = s5_mixer(s5_c, s5_l, *s5_p, want_ctx)
    y_ssd_c, y_ssd_l = ssd_mixer(ssd_c, ssd_l, *ssd_p, want_ctx)
    y_lat = jnp.concatenate([y_s5_l, y_ssd_l, hyena_mixer(hy_l, *hy_p)], axis=-1) @ w_out
    if not want_ctx:
        return None, y_lat
    y_ctx = jnp.concatenate([y_s5_c, y_ssd_c, hyena_mixer(hy_c, *hy_p)], axis=-1) @ w_out
    return y_ctx, y_lat


def swiglu(h, w_in, w_out):
    gate, up = jnp.split(h @ w_in, 2, axis=-1)
    return (jax.nn.silu(gate) * up) @ w_out


def setup_inputs(seed: int = 0) -> dict:
    key = jax.random.key(seed)
    ks = iter(jax.random.split(key, 64))

    def nrm(shape, scale=1.0):
        return scale * jax.random.normal(next(ks), shape, F32)

    def unif(shape, lo, hi):
        return jax.random.uniform(next(ks), shape, F32, lo, hi)

    D = D_MODEL
    L2 = (DEPTH, 2)
    dt0 = jnp.exp(unif(L2 + (SSD_HEADS,), math.log(1e-3), math.log(1e-1)))
    return {
        'x': nrm((BATCH, SEQ, D)),
        'c': nrm((BATCH, D)),
        'ctx': nrm((BATCH, CTX_LEN, D)),
        'c_ctx': nrm((D,)),
        'w_mod': nrm((DEPTH, D, 6 * D), 0.5 * D ** -0.5),
        'b_mod': nrm((DEPTH, 6 * D), 0.01),
        'w_in': nrm((DEPTH, D, MIX_IN), D ** -0.5),
        's5_lam_re': -0.5 + nrm(L2 + (S5_GROUPS, S5_STATE), 0.01),
        's5_lam_im': math.pi * jnp.arange(S5_STATE, dtype=F32) + nrm(L2 + (S5_GROUPS, S5_STATE), 0.01),
        's5_log_step': unif(L2 + (S5_GROUPS,), math.log(1e-3), math.log(1e-1)),
        's5_b_re': nrm(L2 + (S5_GROUPS, S5_STATE, S5_GROUP_CH), (2 * S5_GROUP_CH) ** -0.5),
        's5_b_im': nrm(L2 + (S5_GROUPS, S5_STATE, S5_GROUP_CH), (2 * S5_GROUP_CH) ** -0.5),
        's5_c_re': nrm(L2 + (S5_GROUPS, S5_GROUP_CH, S5_STATE), 0.5),
        's5_c_im': nrm(L2 + (S5_GROUPS, S5_GROUP_CH, S5_STATE), 0.5),
        's5_d': nrm((DEPTH, S5_WIDTH)),
        's5_w_glu': nrm((DEPTH, S5_WIDTH, S5_WIDTH), S5_WIDTH ** -0.5),
        's5_b_glu': nrm((DEPTH, S5_WIDTH), 0.01),
        'ssd_conv_w': nrm((DEPTH, SSD_CONV, SSD_XBC), SSD_CONV ** -0.5),
        'ssd_conv_b': nrm((DEPTH, SSD_XBC), 0.01),
        'ssd_dt_bias': dt0 + jnp.log(-jnp.expm1(-dt0)),
        'ssd_a_log': jnp.log(unif(L2 + (SSD_HEADS,), 1.0, 16.0)),
        'ssd_d': 1.0 + nrm((DEPTH, SSD_HEADS), 0.01),
        'ssd_norm_w': 1.0 + nrm((DEPTH, SSD_INNER), 0.01),
        'hy_conv_w': nrm((DEPTH, HY_SHORT, HY_IN), HY_SHORT ** -0.5),
        'hy_conv_b': nrm((DEPTH, HY_IN), 0.01),
        'hy_w1': nrm((DEPTH, HY_EMB, HY_FILTER_HID), HY_EMB ** -0.5),
        'hy_b1': nrm((DEPTH, HY_FILTER_HID), 0.1),
        'hy_w2': nrm((DEPTH, HY_FILTER_HID, HY_FILTER_HID), HY_FILTER_HID ** -0.5),
        'hy_b2': nrm((DEPTH, HY_FILTER_HID), 0.1),
        'hy_w3': nrm((DEPTH, HY_FILTER_HID, 2 * HY_ORDER * HY_WIDTH), HY_FILTER_HID ** -0.5),
        'hy_freq': 1.0 + nrm((DEPTH, HY_FILTER_HID), 0.01),
        'hy_decay': unif((DEPTH, 2 * HY_ORDER * HY_WIDTH), 3.07, 15.35),
        'hy_bias': nrm((DEPTH, HY_ORDER, HY_WIDTH)),
        'w_out': nrm((DEPTH, MIX_OUT, D), BETA * MIX_OUT ** -0.5),
        'ln1_g': 1.0 + nrm((DEPTH, D), 0.01),
        'ln1_b': nrm((DEPTH, D), 0.01),
        'ffn_w_in': nrm((DEPTH, D, 2 * FFN_HIDDEN), D ** -0.5),
        'ffn_w_out': nrm((DEPTH, FFN_HIDDEN, D), BETA * FFN_HIDDEN ** -0.5),
        'ln2_g': 1.0 + nrm((DEPTH, D), 0.01),
        'ln2_b': nrm((DEPTH, D), 0.01),
    }


def reference(x, c, ctx, c_ctx, w_mod, b_mod, w_in,
              s5_lam_re, s5_lam_im, s5_log_step, s5_b_re, s5_b_im, s5_c_re, s5_c_im, s5_d, s5_w_glu, s5_b_glu,
              ssd_conv_w, ssd_conv_b, ssd_dt_bias, ssd_a_log, ssd_d, ssd_norm_w,
              hy_conv_w, hy_conv_b, hy_w1, hy_b1, hy_w2, hy_b2, hy_w3, hy_freq, hy_decay, hy_bias,
              w_out, ln1_g, ln1_b, ffn_w_in, ffn_w_out, ln2_g, ln2_b):
    c_act = jax.nn.silu(c)
    c_ctx_act = jax.nn.silu(c_ctx)
    for l in range(DEPTH):
        want_ctx = l < DEPTH - 1
        sh1, sc1, g1, sh2, sc2, g2 = jnp.split((c_act @ w_mod[l] + b_mod[l])[:, None, :], 6, axis=-1)
        csh1, csc1, cg1, csh2, csc2, cg2 = jnp.split(c_ctx_act @ w_mod[l] + b_mod[l], 6, axis=-1)
        s5_p = (s5_lam_re[l], s5_lam_im[l], s5_log_step[l], s5_b_re[l], s5_b_im[l],
                s5_c_re[l], s5_c_im[l], s5_d[l], s5_w_glu[l], s5_b_glu[l])
        ssd_p = (ssd_conv_w[l], ssd_conv_b[l], ssd_dt_bias[l], ssd_a_log[l], ssd_d[l], ssd_norm_w[l])
        hy_p = (hy_conv_w[l], hy_conv_b[l], hy_w1[l], hy_b1[l], hy_w2[l], hy_b2[l],
                hy_w3[l], hy_freq[l], hy_decay[l], hy_bias[l])
        y_ctx, y_lat = mixer_block(modulate(ctx, csh1, csc1), modulate(x, sh1, sc1),
                                   w_in[l], w_out[l], s5_p, ssd_p, hy_p, want_ctx)
        x = layer_norm(ALPHA * x + g1 * y_lat, ln1_g[l], ln1_b[l])
        x = layer_norm(ALPHA * x + g2 * swiglu(modulate(x, sh2, sc2), ffn_w_in[l], ffn_w_out[l]),
                       ln2_g[l], ln2_b[l])
        if want_ctx:
            ctx = layer_norm(ALPHA * ctx + cg1 * y_ctx, ln1_g[l], ln1_b[l])
            ctx = layer_norm(ALPHA * ctx + cg2 * swiglu(modulate(ctx, csh2, csc2), ffn_w_in[l], ffn_w_out[l]),
                             ln2_g[l], ln2_b[l])
    return x
```

```python
import functools
import math

import numpy as np
import jax
import jax.numpy as jnp
from jax import lax
from jax.experimental import pallas as pl
from jax.experimental.pallas import tpu as pltpu

F32 = jnp.float32
BF16 = jnp.bfloat16
HI = lax.Precision.HIGHEST

D_MODEL = 1024
DEPTH = 2
S5_WIDTH = 256
S5_GROUP_CH = 16
S5_GROUPS = 16
S5_STATE = 64
S5_MAX_RE = -1e-4
S5_CHUNK = 16
SSD_INNER = 512
SSD_HEADDIM = 64
SSD_HEADS = 8
SSD_GROUPS = 2
SSD_STATE = 128
SSD_CONV = 5
SSD_CHUNK = 128
SSD_XBC = 1024
HY_WIDTH = 256
HY_ORDER = 2
HY_SHORT = 3
HY_BANDS = 16
HY_IN = 768
MIX_IN_PAD = 2688
FFN_HIDDEN = 2816
ALPHA = (2 * DEPTH) ** 0.25
LN_EPS = 1e-6

ROW_TILE = 256
FFT_N2 = 128
V7X_VMEM_LIMIT_MB = 56


def _cparams(sem, vmem_mb=None):
    kw = dict(dimension_semantics=sem)
    if vmem_mb is not None:
        kw["vmem_limit_bytes"] = vmem_mb * 2 ** 20
    return pltpu.CompilerParams(**kw)


def _standardise(x):
    mu = jnp.mean(x, axis=-1, keepdims=True)
    xc = x - mu
    var = jnp.mean(xc * xc, axis=-1, keepdims=True)
    return xc * lax.rsqrt(var + LN_EPS)


def _sigmoid(x):
    return 1.0 / (1.0 + jnp.exp(-x))


def _silu(x):
    return x * _sigmoid(x)


def _gelu_tanh(x):
    return 0.5 * x * (1.0 + jnp.tanh(0.7978845608028654 * (x + 0.044715 * (x * x * x))))


def _softplus(x):
    return jnp.maximum(x, 0.0) + jnp.log(1.0 + jnp.exp(-jnp.abs(x)))


def _bdot(a, b):
    return jnp.dot(a.astype(BF16), b.astype(BF16), preferred_element_type=F32)


def _split3(a):
    a1 = a.astype(BF16)
    r1 = a - a1.astype(F32)
    a2 = r1.astype(BF16)
    a3 = (r1 - a2.astype(F32)).astype(BF16)
    return a1, a2, a3


def _dot_sel_lhs(sel, a):
    a1, a2, a3 = _split3(a)
    d = functools.partial(jnp.dot, preferred_element_type=F32)
    return d(sel, a1) + d(sel, a2) + d(sel, a3)


def _dot_sel_rhs(a, sel):
    a1, a2, a3 = _split3(a)
    d = functools.partial(jnp.dot, preferred_element_type=F32)
    return d(a1, sel) + d(a2, sel) + d(a3, sel)


def _mod_kernel(c_ref, w_ref, b_ref, o_ref):
    ca = _silu(c_ref[...])
    c1, c2, c3 = _split3(ca)
    w1, w2, w3 = _split3(w_ref[...])
    d = functools.partial(jnp.dot, preferred_element_type=F32)
    acc = d(c1, w1) + d(c1, w2) + d(c2, w1) + d(c1, w3) + d(c2, w2) + d(c3, w1)
    o_ref[...] = acc + b_ref[...]


def _modulation(cvec, w, b):
    n = w.shape[1]
    tn = 1536
    return pl.pallas_call(
        _mod_kernel,
        grid=(n // tn,),
        in_specs=[pl.BlockSpec((16, D_MODEL), lambda j: (0, 0)),
                  pl.BlockSpec((D_MODEL, tn), lambda j: (0, j)),
                  pl.BlockSpec((1, tn), lambda j: (0, j))],
        out_specs=pl.BlockSpec((16, tn), lambda j: (0, j)),
        out_shape=jax.ShapeDtypeStruct((16, n), F32),
        compiler_params=_cparams(("arbitrary",), 40),
        name="modulation",
    )(cvec, w, b)


def _inproj_kernel(x_ref, m_ref, w_ref, s5_ref, z_ref, xbc_ref, hy_ref, dt_ref):
    m = m_ref[0, 0]
    h = (_standardise(x_ref[0]) * (1.0 + m[1:2]) + m[0:1]).astype(BF16)
    d = functools.partial(jnp.dot, preferred_element_type=F32)
    s5_ref[0] = d(h, w_ref[:, 0:256])
    z_ref[0] = d(h, w_ref[:, 256:768])
    xbc_ref[0] = d(h, w_ref[:, 768:1792])
    hy_ref[0] = d(h, w_ref[:, 1792:2560])
    dt_ref[0] = d(h, w_ref[:, 2560:2688])


def _inproj(x_all, mods, w_p, nt_lat):
    bsz, t_all, _ = x_all.shape
    nt = t_all // ROW_TILE
    widths = (256, 512, 1024, 768, 128)
    return pl.pallas_call(
        _inproj_kernel,
        grid=(bsz, nt),
        in_specs=[pl.BlockSpec((1, ROW_TILE, D_MODEL), lambda b, i: (b, i, 0)),
                  pl.BlockSpec((1, 1, 6, D_MODEL), lambda b, i: (b, jnp.where(i < nt_lat, 0, 1), 0, 0)),
                  pl.BlockSpec((D_MODEL, MIX_IN_PAD), lambda b, i: (0, 0))],
        out_specs=[pl.BlockSpec((1, ROW_TILE, w), lambda b, i: (b, i, 0)) for w in widths],
        out_shape=[jax.ShapeDtypeStruct((bsz, t_all, w), F32) for w in widths],
        compiler_params=_cparams(("parallel", "parallel"), 40),
        name="inproj",
    )(x_all, mods, w_p)


def _dwconv_kernel(taps, act, nt_lat, n_out, xm_ref, xp_ref, xn_ref, w_ref, b_ref, *rest):
    o_refs, ext_ref = rest[:n_out], rest[n_out]
    i = pl.program_id(1)
    tm = xm_ref.shape[1]
    ch = xm_ref.shape[2]
    has_prev = jnp.logical_and(i > 0, i < nt_lat)
    has_next = i < nt_lat - 1
    ext_ref[0:8, :] = jnp.where(has_prev, xp_ref[0], 0.0)
    ext_ref[8:8 + tm, :] = xm_ref[0]
    ext_ref[8 + tm:16 + tm, :] = jnp.where(has_next, xn_ref[0], 0.0)
    pad = taps // 2
    rb = 64
    wo = ch // n_out
    for c0 in range(0, ch, 128):
        wk = [w_ref[k:k + 1, c0:c0 + 128] for k in range(taps)]
        bias = b_ref[0:1, c0:c0 + 128]
        for r0 in range(0, tm, rb):
            acc = bias + wk[0] * ext_ref[8 - pad + r0:8 - pad + r0 + rb, c0:c0 + 128]
            for k in range(1, taps):
                acc = acc + wk[k] * ext_ref[8 - pad + k + r0:8 - pad + k + r0 + rb, c0:c0 + 128]
            if act:
                acc = _silu(acc)
            o_refs[c0 // wo][0, r0:r0 + rb, (c0 % wo):(c0 % wo) + 128] = acc


def _dwconv(x, w, b, taps, act, nt_lat, n_out):
    bsz, t_all, ch = x.shape
    nt = t_all // ROW_TILE
    r8 = ROW_TILE // 8
    last8 = t_all // 8 - 1
    wo = ch // n_out
    return pl.pallas_call(
        functools.partial(_dwconv_kernel, taps, act, nt_lat, n_out),
        grid=(bsz, nt),
        in_specs=[pl.BlockSpec((1, ROW_TILE, ch), lambda b_, i: (b_, i, 0)),
                  pl.BlockSpec((1, 8, ch), lambda b_, i: (b_, jnp.maximum(i * r8 - 1, 0), 0)),
                  pl.BlockSpec((1, 8, ch), lambda b_, i: (b_, jnp.minimum((i + 1) * r8, last8), 0)),
                  pl.BlockSpec((taps, ch), lambda b_, i: (0, 0)),
                  pl.BlockSpec((1, ch), lambda b_, i: (0, 0))],
        out_specs=[pl.BlockSpec((1, ROW_TILE, wo), lambda b_, i: (b_, i, 0)) for _ in range(n_out)],
        out_shape=[jax.ShapeDtypeStruct((bsz, t_all, wo), F32) for _ in range(n_out)],
        scratch_shapes=[pltpu.VMEM((ROW_TILE + 16, ch), F32)],
        compiler_params=_cparams(("parallel", "parallel")),
        name="dwconv%d" % taps,
    )(x, x, x, w, b.reshape(1, ch))


def _s5_kernel(n_lat, n_ctx, ut_ref, ncat_ref, tz_ref, mcat_ref, coef_ref, y_ref, s_ref, hp_ref):
    ut = ut_ref[0]
    s_ref[...] = jnp.dot(ut, ncat_ref[0], preferred_element_type=F32)
    c1f = coef_ref[0, 0, :, 0:128]
    c1b = coef_ref[0, 0, :, 128:256]
    c2f = coef_ref[0, 1, :, 0:128]
    c2b = coef_ref[0, 1, :, 128:256]

    def step(cf, cb, carry):
        hf, hsf, hb, hsb = carry
        rf = pl.multiple_of(cf * 8, 8)
        rb = pl.multiple_of(cb * 8, 8)
        hp_ref[pl.ds(rf, 8), 0:128] = hf
        hp_ref[pl.ds(rb, 8), 128:256] = hb
        sf = s_ref[pl.ds(rf, 8), 0:128]
        ssf = s_ref[pl.ds(rf, 8), 256:384]
        sb = s_ref[pl.ds(rb, 8), 128:256]
        ssb = s_ref[pl.ds(rb, 8), 384:512]
        return (c1f * hf + c2f * hsf + sf, c1f * hsf - c2f * hf + ssf,
                c1b * hb + c2b * hsb + sb, c1b * hsb - c2b * hb + ssb)

    z = jnp.zeros((8, 128), F32)
    carry = lax.fori_loop(0, n_ctx, lambda i, c: step(n_lat + i, n_lat + n_ctx - 1 - i, c), (z, z, z, z))
    lax.fori_loop(0, n_lat, lambda i, c: step(i, n_lat - 1 - i, c), carry)
    y_ref[0] = (jnp.dot(ut, tz_ref[0], preferred_element_type=F32)
                + jnp.dot(hp_ref[...].astype(BF16), mcat_ref[0], preferred_element_type=F32))


def _s5_scan(ut, ncat, tz, mcat, coef, n_lat, n_ctx):
    g, rows, _ = ut.shape
    return pl.pallas_call(
        functools.partial(_s5_kernel, n_lat, n_ctx),
        grid=(g,),
        in_specs=[pl.BlockSpec((1, rows, 256), lambda j: (j, 0, 0)),
                  pl.BlockSpec((1, 256, 512), lambda j: (j, 0, 0)),
                  pl.BlockSpec((1, 256, 256), lambda j: (j, 0, 0)),
                  pl.BlockSpec((1, 256, 256), lambda j: (j, 0, 0)),
                  pl.BlockSpec((1, 2, 8, 256), lambda j: (j, 0, 0, 0))],
        out_specs=pl.BlockSpec((1, rows, 256), lambda j: (j, 0, 0)),
        out_shape=jax.ShapeDtypeStruct((g, rows, 256), F32),
        scratch_shapes=[pltpu.VMEM((rows, 512), F32), pltpu.VMEM((rows, 256), F32)],
        compiler_params=_cparams(("parallel",), 40),
        name="s5_scan",
    )(ut, ncat, tz, mcat, coef)


def _s5_weights(lam_re, lam_im, log_step, b_re, b_im, c_re, c_im):
    q = S5_CHUNK
    lam = lax.complex(jnp.minimum(lam_re.astype(F32), S5_MAX_RE), lam_im.astype(F32))
    step = jnp.exp(log_step.astype(F32))[..., None]
    lam_bar = jnp.exp(lam * step)
    b_bar = ((lam_bar - 1.0) / lam)[..., None] * lax.complex(b_re.astype(F32), b_im.astype(F32))
    c_mat = lax.complex(c_re.astype(F32), c_im.astype(F32))
    taus = jnp.arange(q + 1, dtype=F32)
    pw = jnp.exp((lam * step)[None] * taus[:, None, None, None])
    kern = jnp.real(jnp.einsum('dghp,tdgp,dgpk->tdghk', c_mat, pw[:q], b_bar, precision=HI))
    tt = jnp.arange(q)
    lag = tt[None, :] - tt[:, None]
    kf = jnp.where((lag >= 0)[..., None, None, None], kern[jnp.clip(lag, 0, q - 1), 0], 0.0)
    kb = jnp.where((lag <= 0)[..., None, None, None], kern[jnp.clip(-lag, 0, q - 1), 1], 0.0)
    tz = jnp.transpose(kf + kb, (2, 0, 4, 1, 3)).reshape(S5_GROUPS, q * S5_GROUP_CH, q * S5_GROUP_CH)
    nf = jnp.einsum('sgp,gpk->gskp', pw[q - 1 - tt, 0], b_bar[0]).reshape(S5_GROUPS, q * S5_GROUP_CH, S5_STATE)
    nb = jnp.einsum('sgp,gpk->gskp', pw[tt, 1], b_bar[1]).reshape(S5_GROUPS, q * S5_GROUP_CH, S5_STATE)
    ncat = jnp.concatenate([jnp.real(nf), jnp.imag(nf), jnp.real(nb), jnp.imag(nb),
                            jnp.imag(nf), jnp.real(nf), jnp.imag(nb), jnp.real(nb)], axis=-1)
    mf = jnp.einsum('ghp,tgp->gpth', c_mat[0], pw[tt + 1, 0]).reshape(S5_GROUPS, S5_STATE, q * S5_GROUP_CH)
    mb = jnp.einsum('ghp,tgp->gpth', c_mat[1], pw[q - tt, 1]).reshape(S5_GROUPS, S5_STATE, q * S5_GROUP_CH)
    mcat = jnp.concatenate([jnp.real(mf), -jnp.imag(mf), jnp.real(mb), -jnp.imag(mb)], axis=1)
    lq = pw[q]
    lr, li = jnp.real(lq), jnp.imag(lq)
    c1 = jnp.concatenate([lr[0], lr[0], lr[1], lr[1]], axis=-1)
    c2 = jnp.concatenate([-li[0], li[0], -li[1], li[1]], axis=-1)
    coef = jnp.broadcast_to(jnp.stack([c1, c2], axis=1)[:, :, None, :], (S5_GROUPS, 2, 8, 256))
    return ncat.astype(BF16), tz.astype(BF16), mcat.astype(BF16), coef.astype(F32)


def _ssd_kernel(xbc_ref, dt_ref, tri_ref, par_ref, exp_ref, y_ref, st_ref):
    @pl.when(pl.program_id(2) == 0)
    def _():
        st_ref[...] = jnp.zeros_like(st_ref)

    tri = tri_ref[0]
    mask = tri > 0
    xs = xbc_ref[0, :, 0:512]
    dt_full = _softplus(_dot_sel_rhs(dt_ref[0], exp_ref[...]) + par_ref[0, 0:1, :])
    a_full = par_ref[0, 1:2, :] * dt_full
    cs = _dot_sel_lhs(tri, a_full)
    tot = jnp.sum(a_full, axis=0, keepdims=True)
    xdt = xs * dt_full
    xdt_b = xdt.astype(BF16)
    xd_end = (xdt * jnp.exp(tot - cs)).astype(BF16)
    e_cs = jnp.exp(cs)
    e_tot = jnp.exp(tot)
    lane = lax.broadcasted_iota(jnp.int32, (SSD_CHUNK, 128), 1)
    lo = lane < SSD_HEADDIM
    zero_b = jnp.zeros((SSD_CHUNK, 128), BF16)
    for g in range(SSD_GROUPS):
        bm = xbc_ref[0, :, 512 + g * 128:512 + (g + 1) * 128]
        cm = xbc_ref[0, :, 768 + g * 128:768 + (g + 1) * 128]
        bm_b = bm.astype(BF16)
        cm_b = cm.astype(BF16)
        cb = lax.dot_general(cm_b, bm_b, (((1,), (1,)), ((), ())), preferred_element_type=F32)
        st_g = st_ref[:, g * 256:(g + 1) * 256]
        y_off = jnp.dot(cm_b, st_g.astype(BF16), preferred_element_type=F32) * e_cs[:, g * 256:(g + 1) * 256]
        for j in range(2):
            c0 = g * 256 + j * 128
            cs_t = cs[:, c0:c0 + 128].T
            x_pair = xdt_b[:, c0:c0 + 128]
            acc = y_off[:, j * 128:(j + 1) * 128]
            for hh in range(2):
                col = cs[:, c0 + hh * 64:c0 + hh * 64 + 1]
                row = cs_t[hh * 64:hh * 64 + 1, :]
                decay = jnp.exp(jnp.where(mask, col - row, -1e30))
                gm = (cb * decay).astype(BF16)
                xh = jnp.where(lo if hh == 0 else jnp.logical_not(lo), x_pair, zero_b)
                acc = acc + jnp.dot(gm, xh, preferred_element_type=F32)
            y_ref[0, 0, :, c0:c0 + 128] = acc
        st_ref[:, g * 256:(g + 1) * 256] = (
            st_g * e_tot[:, g * 256:(g + 1) * 256]
            + jnp.dot(bm.T.astype(BF16), xd_end[:, g * 256:(g + 1) * 256], preferred_element_type=F32))


def _ssd_scan(xbc_c, dt_raw, tri, par, expand, n_lat, n_ctx):
    bsz, t_all, _ = xbc_c.shape
    nc = n_lat + n_ctx

    def chunk(d, i):
        fwd = jnp.where(i < n_ctx, n_lat + i, i - n_ctx)
        bwd = jnp.where(i < n_ctx, nc - 1 - i, nc - 1 - i)
        return jnp.where(d == 0, fwd, bwd)

    return pl.pallas_call(
        _ssd_kernel,
        grid=(bsz, 2, nc),
        in_specs=[pl.BlockSpec((1, SSD_CHUNK, SSD_XBC), lambda b, d, i: (b, chunk(d, i), 0)),
                  pl.BlockSpec((1, SSD_CHUNK, 128), lambda b, d, i: (b, chunk(d, i), 0)),
                  pl.BlockSpec((1, SSD_CHUNK, SSD_CHUNK), lambda b, d, i: (d, 0, 0)),
                  pl.BlockSpec((1, 8, SSD_INNER), lambda b, d, i: (d, 0, 0)),
                  pl.BlockSpec((128, SSD_INNER), lambda b, d, i: (0, 0))],
        out_specs=pl.BlockSpec((1, 1, SSD_CHUNK, SSD_INNER), lambda b, d, i: (b, d, chunk(d, i), 0)),
        out_shape=jax.ShapeDtypeStruct((bsz, 2, t_all, SSD_INNER), F32),
        scratch_shapes=[pltpu.VMEM((SSD_STATE, SSD_INNER), F32)],
        compiler_params=_cparams(("parallel", "parallel", "arbitrary")),
        name="ssd_scan",
    )(xbc_c, dt_raw, tri, par, expand)


def _hy_fwd_a_kernel(u_ref, fa_ref, a_ref):
    a_ref[0] = jnp.dot(fa_ref[...], u_ref[0].astype(BF16), preferred_element_type=F32).astype(BF16)


def _hy_mid_kernel(k1n, a_ref, mf_ref, kr_ref, ki_ref, mi_ref, b_ref):
    n2, w = a_ref.shape[3], a_ref.shape[4]
    a = a_ref[0, 0].reshape(2 * n2, w)
    x = jnp.dot(mf_ref[0], a, preferred_element_type=F32)
    xr, xi = x[:n2], x[n2:]
    kr, ki = kr_ref[0], ki_ref[0]
    y = jnp.concatenate([xr * kr - xi * ki, xr * ki + xi * kr], axis=0).astype(BF16)
    bq = jnp.dot(mi_ref[0], y, preferred_element_type=F32)
    bq = jnp.where(pl.program_id(1) < k1n, bq, 0.0)
    b_ref[0, 0] = bq.astype(BF16).reshape(2, n2, w)


def _hy_gate_next_kernel(bq_ref, ga_ref, u_ref, x_ref, bias_ref, fa_ref, z_ref, a_ref):
    y = jnp.dot(ga_ref[...], bq_ref[0], preferred_element_type=F32)
    z = x_ref[0] * (y + u_ref[0] * bias_ref[0])
    z_ref[0] = z
    a_ref[0] = jnp.dot(fa_ref[...], z.astype(BF16), preferred_element_type=F32).astype(BF16)


def _hy_gate_last_kernel(bq_ref, ga_ref, u_ref, x_ref, bias_ref, z_ref):
    y = jnp.dot(ga_ref[...], bq_ref[0], preferred_element_type=F32)
    z_ref[0] = x_ref[0] * (y + u_ref[0] * bias_ref[0])


def _hyena_lat(v, x1, x2, fa, ga, mfwd, minv, kr, ki, bias_t, seq):
    bsz, t_all, w = v.shape
    n2 = FFT_N2
    h1 = seq // n2
    kp2 = fa.shape[0]
    kp = kp2 // 2
    k1n = mfwd.shape[0]
    cols = n2 * w
    ct = 4096
    nct = cols // ct
    rows_all = t_all // n2
    view = lambda t: t.reshape(bsz, rows_all, cols)
    vv, x1v, x2v = view(v), view(x1), view(x2)
    col_spec = pl.BlockSpec((1, h1, ct), lambda b, j: (b, 0, j))
    a_spec = pl.BlockSpec((1, kp2, ct), lambda b, j: (b, 0, j))
    a_shape = jax.ShapeDtypeStruct((bsz, kp2, cols), BF16)
    z_shape = jax.ShapeDtypeStruct((bsz, h1, cols), F32)
    fa_spec = pl.BlockSpec((kp2, h1), lambda b, j: (0, 0))
    ga_spec = pl.BlockSpec((h1, kp2), lambda b, j: (0, 0))
    bias_spec = lambda o: pl.BlockSpec((1, 1, ct), lambda b, j: (o, 0, 0))
    par2 = _cparams(("parallel", "parallel"))

    def mid(a, kr_o, ki_o):
        slab = pl.BlockSpec((1, 1, 2, n2, w), lambda b, k: (b, k, 0, 0, 0))
        kc = lambda k: jnp.minimum(k, k1n - 1)
        mat = pl.BlockSpec((1, 2 * n2, 2 * n2), lambda b, k: (kc(k), 0, 0))
        spec = pl.BlockSpec((1, n2, w), lambda b, k: (kc(k), 0, 0))
        out = pl.pallas_call(
            functools.partial(_hy_mid_kernel, k1n),
            grid=(bsz, kp),
            in_specs=[slab, mat, spec, spec, mat],
            out_specs=slab,
            out_shape=jax.ShapeDtypeStruct((bsz, kp, 2, n2, w), BF16),
            compiler_params=par2,
            name="hyena_mid",
        )(a.reshape(bsz, kp, 2, n2, w), mfwd, kr_o, ki_o, minv)
        return out.reshape(bsz, kp2, cols)

    a1 = pl.pallas_call(
        _hy_fwd_a_kernel, grid=(bsz, nct),
        in_specs=[col_spec, fa_spec], out_specs=a_spec, out_shape=a_shape,
        compiler_params=par2, name="hyena_fwd_a",
    )(vv, fa)
    bq = mid(a1, kr[0], ki[0])
    z1, a2 = pl.pallas_call(
        _hy_gate_next_kernel, grid=(bsz, nct),
        in_specs=[a_spec, ga_spec, col_spec, col_spec, bias_spec(0), fa_spec],
        out_specs=[col_spec, a_spec], out_shape=[z_shape, a_shape],
        compiler_params=par2, name="hyena_gate_next",
    )(bq, ga, vv, x1v, bias_t, fa)
    bq = mid(a2, kr[1], ki[1])
    z2 = pl.pallas_call(
        _hy_gate_last_kernel, grid=(bsz, nct),
        in_specs=[a_spec, ga_spec, col_spec, col_spec, bias_spec(1)],
        out_specs=col_spec, out_shape=z_shape,
        compiler_params=par2, name="hyena_gate_last",
    )(bq, ga, z1, x2v, bias_t)
    return z2.reshape(bsz, seq, w)


def _hyena_lat_consts(seq):
    n = 2 * seq
    n2 = FFT_N2
    n1 = n // n2
    h1 = n1 // 2
    k1n = n1 // 2 + 1
    kp = -(-k1n // 8) * 8
    k1 = np.arange(k1n)
    m1 = np.arange(h1)
    ang = 2.0 * np.pi * np.outer(k1, m1) / n1
    fa = np.zeros((2 * kp, h1))
    fa[0:2 * k1n:2] = np.cos(ang)
    fa[1:2 * k1n:2] = -np.sin(ang)
    wgt = np.where((k1 == 0) | (k1 == n1 // 2), 1.0, 2.0) / n
    ga = np.zeros((h1, 2 * kp))
    ga[:, 0:2 * k1n:2] = (np.cos(ang) * wgt[:, None]).T
    ga[:, 1:2 * k1n:2] = (-np.sin(ang) * wgt[:, None]).T
    k2 = np.arange(n2)
    m2 = np.arange(n2)
    kk = k1[:, None, None] + n1 * k2[None, :, None]
    th = 2.0 * np.pi * ((kk * m2[None, None, :]) % n) / n
    mc, ms = np.cos(th), np.sin(th)
    mfwd = np.concatenate([np.concatenate([mc, ms], axis=2), np.concatenate([-ms, mc], axis=2)], axis=1)
    mct, mst = np.transpose(mc, (0, 2, 1)), np.transpose(ms, (0, 2, 1))
    minv = np.concatenate([np.concatenate([mct, -mst], axis=2), np.concatenate([mst, mct], axis=2)], axis=1)
    kidx = (k1[:, None] + n1 * k2[None, :])
    return (jnp.asarray(fa, BF16), jnp.asarray(ga, BF16), jnp.asarray(mfwd, BF16), jnp.asarray(minv, BF16), kidx)


def _hy_ctx_kernel(v_ref, x1_ref, x2_ref, fc_ref, fs_ref, gc_ref, gs_ref, kr_ref, ki_ref, bias_ref, z_ref):
    d = functools.partial(jnp.dot, preferred_element_type=F32)

    def conv(u, o):
        ub = u.astype(BF16)
        cr, ci = d(fc_ref[...], ub), d(fs_ref[...], ub)
        kr, ki = kr_ref[o], ki_ref[o]
        pr = (cr * kr - ci * ki).astype(BF16)
        pi = (cr * ki + ci * kr).astype(BF16)
        return d(gc_ref[...], pr) + d(gs_ref[...], pi) + u * bias_ref[o:o + 1, :]

    z = x1_ref[0] * conv(v_ref[0], 0)
    z_ref[0] = x2_ref[0] * conv(z, 1)


def _hyena_ctx(v, x1, x2, fc, fs, gc, gs, kr, ki, bias, seq, ctx_len):
    bsz, _, w = v.shape
    blk = seq // ctx_len
    kpad = fc.shape[0]
    tok = pl.BlockSpec((1, ctx_len, w), lambda b: (b, blk, 0))
    full = lambda shp: pl.BlockSpec(shp, lambda b: (0,) * len(shp))
    return pl.pallas_call(
        _hy_ctx_kernel, grid=(bsz,),
        in_specs=[tok, tok, tok, full((kpad, ctx_len)), full((kpad, ctx_len)), full((ctx_len, kpad)),
                  full((ctx_len, kpad)), full((2, kpad, w)), full((2, kpad, w)), full((2, w))],
        out_specs=pl.BlockSpec((1, ctx_len, w), lambda b: (b, 0, 0)),
        out_shape=jax.ShapeDtypeStruct((bsz, ctx_len, w), F32),
        compiler_params=_cparams(("parallel",)),
        name="hyena_ctx",
    )(v, x1, x2, fc, fs, gc, gs, kr, ki, bias)


def _hyena_ctx_consts(ctx_len):
    n = 2 * ctx_len
    nk = ctx_len + 1
    kpad = -(-nk // 128) * 128
    k = np.arange(nk)
    m = np.arange(ctx_len)
    ang = 2.0 * np.pi * np.outer(k, m) / n
    fc = np.zeros((kpad, ctx_len))
    fs = np.zeros((kpad, ctx_len))
    fc[:nk] = np.cos(ang)
    fs[:nk] = -np.sin(ang)
    wgt = np.where((k == 0) | (k == ctx_len), 1.0, 2.0) / n
    gc = np.zeros((ctx_len, kpad))
    gs = np.zeros((ctx_len, kpad))
    gc[:, :nk] = (np.cos(ang) * wgt[:, None]).T
    gs[:, :nk] = (-np.sin(ang) * wgt[:, None]).T
    return tuple(jnp.asarray(t, BF16) for t in (fc, fs, gc, gs)), kpad


def _hyena_filter_spectrum(length, w1, b1, w2, b2, w3, freq, decay):
    pos = jnp.arange(length, dtype=F32)
    t = pos / max(length - 1, 1)
    bands = jnp.linspace(1e-4, HY_BANDS - 1, HY_BANDS, dtype=F32)
    ang = (2.0 * math.pi / length) * pos[:, None] * bands
    feats = jnp.concatenate([t[:, None], jnp.cos(ang), -jnp.sin(ang)], axis=-1)
    freq = freq.astype(F32)
    mm = functools.partial(jnp.matmul, precision=HI)
    hid = jnp.sin(freq * (mm(feats, w1.astype(F32)) + b1.astype(F32)))
    hid = jnp.sin(freq * (mm(hid, w2.astype(F32)) + b2.astype(F32)))
    h = mm(hid, w3.astype(F32)) * jnp.exp(-t[:, None] * jnp.abs(decay.astype(F32)))
    h = h.reshape(length, 2, HY_ORDER, HY_WIDTH)
    h_fwd, h_bwd = h[:, 0], h[:, 1]
    k = jnp.concatenate([h_fwd[:1] + h_bwd[:1], h_fwd[1:], jnp.zeros_like(h_fwd[:1]), h_bwd[:0:-1]], axis=0)
    k = k / jnp.sum(jnp.abs(k), axis=0, keepdims=True)
    return jnp.fft.rfft(k, axis=0)


def _mixout_kernel(x_ref, m_ref, s5y_ref, s5u_ref, yf_ref, yb_ref, xs_ref, z_ref, hy_ref,
                   wglu_ref, vec256_ref, vec512_ref, wout_ref, ln_ref, o_ref):
    m = m_ref[0, 0]
    d = functools.partial(jnp.dot, preferred_element_type=F32)
    y5 = _gelu_tanh(s5y_ref[0] + vec256_ref[0:1, :] * s5u_ref[0])
    y5 = y5 * _sigmoid(d(y5.astype(BF16), wglu_ref[...]) + vec256_ref[1:2, :])
    ys = yf_ref[0, 0] + yb_ref[0, 0] + vec512_ref[0:1, :] * xs_ref[0]
    gsd = ys * _silu(z_ref[0])
    gsd = gsd * lax.rsqrt(jnp.mean(gsd * gsd, axis=-1, keepdims=True) + LN_EPS) * vec512_ref[1:2, :]
    mix = (d(y5.astype(BF16), wout_ref[0:256, :]) + d(gsd.astype(BF16), wout_ref[256:768, :])
           + d(hy_ref[0].astype(BF16), wout_ref[768:1024, :]))
    r = ALPHA * x_ref[0] + m[2:3] * mix
    o_ref[0] = _standardise(r) * ln_ref[0:1, :] + ln_ref[1:2, :]


def _mixout(x_all, mods, s5y, s5u, yssd, xbc_c, z, hy, wglu, vec256, vec512, wout, ln, nt_lat):
    bsz, t_all, _ = x_all.shape
    nt = t_all // ROW_TILE
    tok = lambda w: pl.BlockSpec((1, ROW_TILE, w), lambda b, i: (b, i, 0))
    full = lambda shp: pl.BlockSpec(shp, lambda b, i: (0,) * len(shp))
    return pl.pallas_call(
        _mixout_kernel, grid=(bsz, nt),
        in_specs=[tok(D_MODEL),
                  pl.BlockSpec((1, 1, 6, D_MODEL), lambda b, i: (b, jnp.where(i < nt_lat, 0, 1), 0, 0)),
                  tok(256), tok(256),
                  pl.BlockSpec((1, 1, ROW_TILE, SSD_INNER), lambda b, i: (b, 0, i, 0)),
                  pl.BlockSpec((1, 1, ROW_TILE, SSD_INNER), lambda b, i: (b, 1, i, 0)),
                  tok(SSD_INNER), tok(SSD_INNER), tok(256),
                  full((256, 256)), full((2, 256)), full((2, 512)), full((D_MODEL, D_MODEL)), full((2, D_MODEL))],
        out_specs=tok(D_MODEL),
        out_shape=jax.ShapeDtypeStruct((bsz, t_all, D_MODEL), F32),
        compiler_params=_cparams(("parallel", "parallel"), 40),
        name="mixout",
    )(x_all, mods, s5y, s5u, yssd, yssd, xbc_c, z, hy, wglu, vec256, vec512, wout, ln)


def _ffn_kernel(x_ref, m_ref, win_ref, wout_ref, ln_ref, o_ref):
    m = m_ref[0, 0]
    x = x_ref[0]
    d = functools.partial(jnp.dot, preferred_element_type=F32)
    h = (_standardise(x) * (1.0 + m[4:5]) + m[3:4]).astype(BF16)
    gate = d(h, win_ref[:, 0:FFN_HIDDEN])
    up = d(h, win_ref[:, FFN_HIDDEN:2 * FFN_HIDDEN])
    act = (_silu(gate) * up).astype(BF16)
    r = ALPHA * x + m[5:6] * d(act, wout_ref[...])
    o_ref[0] = _standardise(r) * ln_ref[0:1, :] + ln_ref[1:2, :]


def _ffn(x_all, mods, win, wout, ln, nt_lat):
    bsz, t_all, _ = x_all.shape
    nt = t_all // ROW_TILE
    tok = pl.BlockSpec((1, ROW_TILE, D_MODEL), lambda b, i: (b, i, 0))
    const = lambda shp: pl.BlockSpec(shp, lambda b, i: (0, 0), pipeline_mode=pl.Buffered(1))
    return pl.pallas_call(
        _ffn_kernel, grid=(bsz, nt),
        in_specs=[tok,
                  pl.BlockSpec((1, 1, 6, D_MODEL), lambda b, i: (b, jnp.where(i < nt_lat, 0, 1), 0, 0)),
                  const((D_MODEL, 2 * FFN_HIDDEN)), const((FFN_HIDDEN, D_MODEL)),
                  pl.BlockSpec((2, D_MODEL), lambda b, i: (0, 0))],
        out_specs=tok,
        out_shape=jax.ShapeDtypeStruct((bsz, t_all, D_MODEL), F32),
        compiler_params=_cparams(("parallel", "parallel"), V7X_VMEM_LIMIT_MB),
        name="ffn",
    )(x_all, mods, win, wout, ln)


def _layer(x_all, mods, p, seq, ctx_len, want_ctx):
    bsz, t_all, _ = x_all.shape
    nt_lat = seq // ROW_TILE
    s5u, z, xbc, hy, dt_raw = _inproj(x_all, mods, p["w_in"], nt_lat)

    q = S5_CHUNK
    nch = t_all // q
    ut = s5u.astype(BF16).reshape(bsz, nch, q, S5_GROUPS, S5_GROUP_CH)
    ut = jnp.transpose(ut, (3, 1, 0, 2, 4)).reshape(S5_GROUPS, nch * bsz, q * S5_GROUP_CH)
    yt = _s5_scan(ut, p["s5_ncat"], p["s5_tz"], p["s5_mcat"], p["s5_coef"], seq // q, ctx_len // q)
    s5y = jnp.transpose(yt.reshape(S5_GROUPS, nch, bsz, q, S5_GROUP_CH), (2, 1, 3, 0, 4)).reshape(bsz, t_all, S5_WIDTH)

    (xbc_c,) = _dwconv(xbc, p["ssd_conv_w"], p["ssd_conv_b"], SSD_CONV, True, nt_lat, 1)
    yssd = _ssd_scan(xbc_c, dt_raw, p["ssd_tri"], p["ssd_par"], p["ssd_expand"],
                     seq // SSD_CHUNK, ctx_len // SSD_CHUNK)

    v, x1, x2 = _dwconv(hy, p["hy_conv_w"], p["hy_conv_b"], HY_SHORT, False, nt_lat, 3)
    z_lat = _hyena_lat(v, x1, x2, p["hy_fa"], p["hy_ga"], p["hy_mfwd"], p["hy_minv"],
                       p["hy_kr"], p["hy_ki"], p["hy_bias_t"], seq)
    if want_ctx:
        z_ctx = _hyena_ctx(v, x1, x2, *p["hy_ctx_mats"], p["hy_ctx_kr"], p["hy_ctx_ki"], p["hy_bias"], seq, ctx_len)
    else:
        z_ctx = jnp.zeros((bsz, ctx_len, HY_WIDTH), F32)
    hyz = jnp.concatenate([z_lat, z_ctx], axis=1)

    x1_all = _mixout(x_all, mods, s5y, s5u, yssd, xbc_c, z, hyz, p["s5_wglu"], p["s5_vec"], p["ssd_vec"],
                     p["w_out"], p["ln1"], nt_lat)
    return _ffn(x1_all, mods, p["ffn_w_in"], p["ffn_w_out"], p["ln2"], nt_lat)


def kernel(x, c, ctx, c_ctx, w_mod, b_mod, w_in, s5_lam_re, s5_lam_im, s5_log_step, s5_b_re, s5_b_im, s5_c_re, s5_c_im, s5_d, s5_w_glu, s5_b_glu, ssd_conv_w, ssd_conv_b, ssd_dt_bias, ssd_a_log, ssd_d, ssd_norm_w, hy_conv_w, hy_conv_b, hy_w1, hy_b1, hy_w2, hy_b2, hy_w3, hy_freq, hy_decay, hy_bias, w_out, ln1_g, ln1_b, ffn_w_in, ffn_w_out, ln2_g, ln2_b):
    bsz, seq, _ = x.shape
    ctx_len = ctx.shape[1]
    assert bsz == 8 and seq % ROW_TILE == 0 and ctx_len == ROW_TILE

    fa, ga, mfwd, minv, kidx = _hyena_lat_consts(seq)
    ctx_mats, kpad_ctx = _hyena_ctx_consts(ctx_len)
    tt = np.arange(SSD_CHUNK)
    tri = jnp.asarray(np.stack([tt[None, :] <= tt[:, None], tt[None, :] >= tt[:, None]]), BF16)
    expand = jnp.asarray(np.repeat(np.eye(128, SSD_HEADS), SSD_HEADDIM, axis=1)[:, :SSD_INNER], BF16)
    cvec = jnp.zeros((16, D_MODEL), F32).at[:bsz].set(c.astype(F32)).at[bsz].set(c_ctx.astype(F32))

    x_all = jnp.concatenate([x.astype(F32), ctx.astype(F32)], axis=1)
    for l in range(DEPTH):
        want_ctx = l < DEPTH - 1
        mod16 = _modulation(cvec, w_mod[l].astype(F32), b_mod[l].astype(F32).reshape(1, -1))
        mods = jnp.stack([mod16[:bsz].reshape(bsz, 6, D_MODEL),
                          jnp.broadcast_to(mod16[bsz].reshape(1, 6, D_MODEL), (bsz, 6, D_MODEL))], axis=1)
        wl = w_in[l]
        w_p = jnp.concatenate([wl[:, 0:256], wl[:, 256:768], wl[:, 768:1792], wl[:, 1800:2568], wl[:, 1792:1800],
                               jnp.zeros((D_MODEL, 120), wl.dtype)], axis=1).astype(BF16)
        ncat, tz, mcat, coef = _s5_weights(s5_lam_re[l], s5_lam_im[l], s5_log_step[l], s5_b_re[l], s5_b_im[l],
                                           s5_c_re[l], s5_c_im[l])
        rep = lambda t: jnp.repeat(t.astype(F32), SSD_HEADDIM, axis=-1)
        par = jnp.zeros((2, 8, SSD_INNER), F32)
        par = par.at[:, 0].set(rep(ssd_dt_bias[l])).at[:, 1].set(rep(-jnp.exp(ssd_a_log[l].astype(F32))))
        spec = _hyena_filter_spectrum(seq, hy_w1[l], hy_b1[l], hy_w2[l], hy_b2[l], hy_w3[l], hy_freq[l], hy_decay[l])
        lo = spec[jnp.asarray(np.minimum(kidx, 2 * seq - kidx))]
        kr = jnp.transpose(jnp.real(lo), (2, 0, 1, 3))
        ki = jnp.transpose(jnp.where(jnp.asarray(kidx > seq)[..., None, None], -jnp.imag(lo), jnp.imag(lo)), (2, 0, 1, 3))
        p = dict(
            w_in=w_p, s5_ncat=ncat, s5_tz=tz, s5_mcat=mcat, s5_coef=coef,
            ssd_conv_w=ssd_conv_w[l].astype(F32), ssd_conv_b=ssd_conv_b[l].astype(F32),
            ssd_tri=tri, ssd_par=par, ssd_expand=expand,
            hy_conv_w=hy_conv_w[l].astype(F32), hy_conv_b=hy_conv_b[l].astype(F32),
            hy_fa=fa, hy_ga=ga, hy_mfwd=mfwd, hy_minv=minv, hy_kr=kr.astype(F32), hy_ki=ki.astype(F32),
            hy_bias=hy_bias[l].astype(F32),
            hy_bias_t=jnp.tile(hy_bias[l].astype(F32), (1, 4096 // HY_WIDTH))[:, None, :],
            s5_wglu=s5_w_glu[l].astype(BF16),
            s5_vec=jnp.stack([s5_d[l], s5_b_glu[l]]).astype(F32),
            ssd_vec=jnp.stack([rep(ssd_d[l]), ssd_norm_w[l].astype(F32)]),
            w_out=w_out[l].astype(BF16), ln1=jnp.stack([ln1_g[l], ln1_b[l]]).astype(F32),
            ffn_w_in=ffn_w_in[l].astype(BF16), ffn_w_out=ffn_w_out[l].astype(BF16),
            ln2=jnp.stack([ln2_g[l], ln2_b[l]]).astype(F32),
        )
        if want_ctx:
            cspec = _hyena_filter_spectrum(ctx_len, hy_w1[l], hy_b1[l], hy_w2[l], hy_b2[l], hy_w3[l],
                                           hy_freq[l], hy_decay[l])
            cspec = jnp.transpose(cspec, (1, 0, 2))
            padk = ((0, 0), (0, kpad_ctx - cspec.shape[1]), (0, 0))
            p.update(hy_ctx_mats=ctx_mats, hy_ctx_kr=jnp.pad(jnp.real(cspec), padk).astype(F32),
                     hy_ctx_ki=jnp.pad(jnp.imag(cspec), padk).astype(F32))
        x_all = _layer(x_all, mods, p, seq, ctx_len, want_ctx)
    return x_all[:, :seq].astype(x.dtype)
```

```python
import functools
import math

import numpy as np
import jax
import jax.numpy as jnp
from jax import lax
from jax.experimental import pallas as pl
from jax.experimental.pallas import tpu as pltpu

F32 = jnp.float32
BF16 = jnp.bfloat16
HI = lax.Precision.HIGHEST

D_MODEL = 1024
DEPTH = 2
S5_WIDTH = 256
S5_GROUP_CH = 16
S5_GROUPS = 16
S5_STATE = 64
S5_MAX_RE = -1e-4
S5_CHUNK = 16
SSD_INNER = 512
SSD_HEADDIM = 64
SSD_HEADS = 8
SSD_GROUPS = 2
SSD_STATE = 128
SSD_CONV = 5
SSD_CHUNK = 128
SSD_XBC = 1024
HY_WIDTH = 256
HY_ORDER = 2
HY_SHORT = 3
HY_BANDS = 16
HY_IN = 768
MIX_IN_PAD = 2688
FFN_HIDDEN = 2816
ALPHA = (2 * DEPTH) ** 0.25
LN_EPS = 1e-6

ROW_TILE = 256
FFT_N2 = 128
HY_A_PITCH = 72
HY_B_PITCH = 136
V7X_VMEM_LIMIT_MB = 56


def _cparams(sem, vmem_mb=None):
    kw = dict(dimension_semantics=sem)
    if vmem_mb is not None:
        kw["vmem_limit_bytes"] = vmem_mb * 2 ** 20
    return pltpu.CompilerParams(**kw)


def _standardise(x):
    mu = jnp.mean(x, axis=-1, keepdims=True)
    xc = x - mu
    var = jnp.mean(xc * xc, axis=-1, keepdims=True)
    return xc * lax.rsqrt(var + LN_EPS)


def _sigmoid(x):
    return 1.0 / (1.0 + jnp.exp(-x))


def _silu(x):
    return x * _sigmoid(x)


def _gelu_tanh(x):
    return 0.5 * x * (1.0 + jnp.tanh(0.7978845608028654 * (x + 0.044715 * (x * x * x))))


def _softplus(x):
    return jnp.maximum(x, 0.0) + jnp.log(1.0 + jnp.exp(-jnp.abs(x)))


def _bdot(a, b):
    return jnp.dot(a.astype(BF16), b.astype(BF16), preferred_element_type=F32)


def _split3(a):
    a1 = a.astype(BF16)
    r1 = a - a1.astype(F32)
    a2 = r1.astype(BF16)
    a3 = (r1 - a2.astype(F32)).astype(BF16)
    return a1, a2, a3


def _dot_sel_lhs(sel, a):
    a1, a2, a3 = _split3(a)
    d = functools.partial(jnp.dot, preferred_element_type=F32)
    return d(sel, a1) + d(sel, a2) + d(sel, a3)


def _dot_sel_rhs(a, sel):
    a1, a2, a3 = _split3(a)
    d = functools.partial(jnp.dot, preferred_element_type=F32)
    return d(a1, sel) + d(a2, sel) + d(a3, sel)


def _mod_kernel(c_ref, w_ref, b_ref, o_ref):
    ca = _silu(c_ref[...])
    c1, c2, c3 = _split3(ca)
    w1, w2, w3 = _split3(w_ref[...])
    d = functools.partial(jnp.dot, preferred_element_type=F32)
    acc = d(c1, w1) + d(c1, w2) + d(c2, w1) + d(c1, w3) + d(c2, w2) + d(c3, w1)
    o_ref[...] = acc + b_ref[...]


def _modulation(cvec, w, b):
    n = w.shape[1]
    tn = 1536
    return pl.pallas_call(
        _mod_kernel,
        grid=(n // tn,),
        in_specs=[pl.BlockSpec((16, D_MODEL), lambda j: (0, 0)),
                  pl.BlockSpec((D_MODEL, tn), lambda j: (0, j)),
                  pl.BlockSpec((1, tn), lambda j: (0, j))],
        out_specs=pl.BlockSpec((16, tn), lambda j: (0, j)),
        out_shape=jax.ShapeDtypeStruct((16, n), F32),
        compiler_params=_cparams(("arbitrary",), 40),
        name="modulation",
    )(cvec, w, b)


def _inproj_kernel(x_ref, m_ref, w_ref, s5_ref, z_ref, xbc_ref, hy_ref, dt_ref):
    m = m_ref[0, 0]
    h = (_standardise(x_ref[0]) * (1.0 + m[1:2]) + m[0:1]).astype(BF16)
    d = functools.partial(jnp.dot, preferred_element_type=F32)
    s5_ref[0] = d(h, w_ref[:, 0:256])
    z_ref[0] = d(h, w_ref[:, 256:768])
    xbc_ref[0] = d(h, w_ref[:, 768:1792])
    hy_ref[0] = d(h, w_ref[:, 1792:2560])
    dt_ref[0] = d(h, w_ref[:, 2560:2688])


def _inproj(x_all, mods, w_p, nt_lat):
    bsz, t_all, _ = x_all.shape
    nt = t_all // ROW_TILE
    widths = (256, 512, 1024, 768, 128)
    return pl.pallas_call(
        _inproj_kernel,
        grid=(bsz, nt),
        in_specs=[pl.BlockSpec((1, ROW_TILE, D_MODEL), lambda b, i: (b, i, 0)),
                  pl.BlockSpec((1, 1, 6, D_MODEL), lambda b, i: (b, jnp.where(i < nt_lat, 0, 1), 0, 0)),
                  pl.BlockSpec((D_MODEL, MIX_IN_PAD), lambda b, i: (0, 0))],
        out_specs=[pl.BlockSpec((1, ROW_TILE, w), lambda b, i: (b, i, 0)) for w in widths],
        out_shape=[jax.ShapeDtypeStruct((bsz, t_all, w), F32) for w in widths],
        compiler_params=_cparams(("parallel", "parallel"), 40),
        name="inproj",
    )(x_all, mods, w_p)


def _dwconv_kernel(taps, act, nt_lat, n_out, xm_ref, xp_ref, xn_ref, w_ref, b_ref, *rest):
    o_refs, ext_ref = rest[:n_out], rest[n_out]
    i = pl.program_id(1)
    tm = xm_ref.shape[1]
    ch = xm_ref.shape[2]
    has_prev = jnp.logical_and(i > 0, i < nt_lat)
    has_next = i < nt_lat - 1
    ext_ref[0:8, :] = jnp.where(has_prev, xp_ref[0], 0.0)
    ext_ref[8:8 + tm, :] = xm_ref[0]
    ext_ref[8 + tm:16 + tm, :] = jnp.where(has_next, xn_ref[0], 0.0)
    pad = taps // 2
    rb = 64
    wo = ch // n_out
    for c0 in range(0, ch, 128):
        wk = [w_ref[k:k + 1, c0:c0 + 128] for k in range(taps)]
        bias = b_ref[0:1, c0:c0 + 128]
        for r0 in range(0, tm, rb):
            acc = bias + wk[0] * ext_ref[8 - pad + r0:8 - pad + r0 + rb, c0:c0 + 128]
            for k in range(1, taps):
                acc = acc + wk[k] * ext_ref[8 - pad + k + r0:8 - pad + k + r0 + rb, c0:c0 + 128]
            if act:
                acc = _silu(acc)
            o_ref = o_refs[c0 // wo]
            o_ref[0, r0:r0 + rb, (c0 % wo):(c0 % wo) + 128] = acc.astype(o_ref.dtype)


def _dwconv(x, w, b, taps, act, nt_lat, n_out, out_dtype):
    bsz, t_all, ch = x.shape
    nt = t_all // ROW_TILE
    r8 = ROW_TILE // 8
    last8 = t_all // 8 - 1
    wo = ch // n_out
    return pl.pallas_call(
        functools.partial(_dwconv_kernel, taps, act, nt_lat, n_out),
        grid=(bsz, nt),
        in_specs=[pl.BlockSpec((1, ROW_TILE, ch), lambda b_, i: (b_, i, 0)),
                  pl.BlockSpec((1, 8, ch), lambda b_, i: (b_, jnp.maximum(i * r8 - 1, 0), 0)),
                  pl.BlockSpec((1, 8, ch), lambda b_, i: (b_, jnp.minimum((i + 1) * r8, last8), 0)),
                  pl.BlockSpec((taps, ch), lambda b_, i: (0, 0)),
                  pl.BlockSpec((1, ch), lambda b_, i: (0, 0))],
        out_specs=[pl.BlockSpec((1, ROW_TILE, wo), lambda b_, i: (b_, i, 0)) for _ in range(n_out)],
        out_shape=[jax.ShapeDtypeStruct((bsz, t_all, wo), out_dtype) for _ in range(n_out)],
        scratch_shapes=[pltpu.VMEM((ROW_TILE + 16, ch), F32)],
        compiler_params=_cparams(("parallel", "parallel")),
        name="dwconv%d" % taps,
    )(x, x, x, w, b.reshape(1, ch))


def _s5_kernel(n_lat, n_ctx, ut_ref, ncat_ref, tz_ref, mcat_ref, coef_ref, y_ref, s_ref, hp_ref):
    ut = ut_ref[0]
    s_ref[...] = jnp.dot(ut, ncat_ref[0], preferred_element_type=F32)
    c1f = coef_ref[0, 0, :, 0:128]
    c1b = coef_ref[0, 0, :, 128:256]
    c2f = coef_ref[0, 1, :, 0:128]
    c2b = coef_ref[0, 1, :, 128:256]

    def step(cf, cb, carry):
        hf, hsf, hb, hsb = carry
        rf = pl.multiple_of(cf * 8, 8)
        rb = pl.multiple_of(cb * 8, 8)
        hp_ref[pl.ds(rf, 8), 0:128] = hf
        hp_ref[pl.ds(rb, 8), 128:256] = hb
        sf = s_ref[pl.ds(rf, 8), 0:128]
        ssf = s_ref[pl.ds(rf, 8), 256:384]
        sb = s_ref[pl.ds(rb, 8), 128:256]
        ssb = s_ref[pl.ds(rb, 8), 384:512]
        return (c1f * hf + c2f * hsf + sf, c1f * hsf - c2f * hf + ssf,
                c1b * hb + c2b * hsb + sb, c1b * hsb - c2b * hb + ssb)

    z = jnp.zeros((8, 128), F32)
    carry = lax.fori_loop(0, n_ctx, lambda i, c: step(n_lat + i, n_lat + n_ctx - 1 - i, c), (z, z, z, z))
    lax.fori_loop(0, n_lat, lambda i, c: step(i, n_lat - 1 - i, c), carry)
    y_ref[0] = (jnp.dot(ut, tz_ref[0], preferred_element_type=F32)
                + jnp.dot(hp_ref[...].astype(BF16), mcat_ref[0], preferred_element_type=F32))


def _s5_scan(ut, ncat, tz, mcat, coef, n_lat, n_ctx):
    g, rows, _ = ut.shape
    return pl.pallas_call(
        functools.partial(_s5_kernel, n_lat, n_ctx),
        grid=(g,),
        in_specs=[pl.BlockSpec((1, rows, 256), lambda j: (j, 0, 0)),
                  pl.BlockSpec((1, 256, 512), lambda j: (j, 0, 0)),
                  pl.BlockSpec((1, 256, 256), lambda j: (j, 0, 0)),
                  pl.BlockSpec((1, 256, 256), lambda j: (j, 0, 0)),
                  pl.BlockSpec((1, 2, 8, 256), lambda j: (j, 0, 0, 0))],
        out_specs=pl.BlockSpec((1, rows, 256), lambda j: (j, 0, 0)),
        out_shape=jax.ShapeDtypeStruct((g, rows, 256), F32),
        scratch_shapes=[pltpu.VMEM((rows, 512), F32), pltpu.VMEM((rows, 256), F32)],
        compiler_params=_cparams(("parallel",), 40),
        name="s5_scan",
    )(ut, ncat, tz, mcat, coef)


def _s5_weights(lam_re, lam_im, log_step, b_re, b_im, c_re, c_im):
    q = S5_CHUNK
    lam = lax.complex(jnp.minimum(lam_re.astype(F32), S5_MAX_RE), lam_im.astype(F32))
    step = jnp.exp(log_step.astype(F32))[..., None]
    lam_bar = jnp.exp(lam * step)
    b_bar = ((lam_bar - 1.0) / lam)[..., None] * lax.complex(b_re.astype(F32), b_im.astype(F32))
    c_mat = lax.complex(c_re.astype(F32), c_im.astype(F32))
    taus = jnp.arange(q + 1, dtype=F32)
    pw = jnp.exp((lam * step)[None] * taus[:, None, None, None])
    kern = jnp.real(jnp.einsum('dghp,tdgp,dgpk->tdghk', c_mat, pw[:q], b_bar, precision=HI))
    tt = jnp.arange(q)
    lag = tt[None, :] - tt[:, None]
    kf = jnp.where((lag >= 0)[..., None, None, None], kern[jnp.clip(lag, 0, q - 1), 0], 0.0)
    kb = jnp.where((lag <= 0)[..., None, None, None], kern[jnp.clip(-lag, 0, q - 1), 1], 0.0)
    tz = jnp.transpose(kf + kb, (2, 0, 4, 1, 3)).reshape(S5_GROUPS, q * S5_GROUP_CH, q * S5_GROUP_CH)
    nf = jnp.einsum('sgp,gpk->gskp', pw[q - 1 - tt, 0], b_bar[0]).reshape(S5_GROUPS, q * S5_GROUP_CH, S5_STATE)
    nb = jnp.einsum('sgp,gpk->gskp', pw[tt, 1], b_bar[1]).reshape(S5_GROUPS, q * S5_GROUP_CH, S5_STATE)
    ncat = jnp.concatenate([jnp.real(nf), jnp.imag(nf), jnp.real(nb), jnp.imag(nb),
                            jnp.imag(nf), jnp.real(nf), jnp.imag(nb), jnp.real(nb)], axis=-1)
    mf = jnp.einsum('ghp,tgp->gpth', c_mat[0], pw[tt + 1, 0]).reshape(S5_GROUPS, S5_STATE, q * S5_GROUP_CH)
    mb = jnp.einsum('ghp,tgp->gpth', c_mat[1], pw[q - tt, 1]).reshape(S5_GROUPS, S5_STATE, q * S5_GROUP_CH)
    mcat = jnp.concatenate([jnp.real(mf), -jnp.imag(mf), jnp.real(mb), -jnp.imag(mb)], axis=1)
    lq = pw[q]
    lr, li = jnp.real(lq), jnp.imag(lq)
    c1 = jnp.concatenate([lr[0], lr[0], lr[1], lr[1]], axis=-1)
    c2 = jnp.concatenate([-li[0], li[0], -li[1], li[1]], axis=-1)
    coef = jnp.broadcast_to(jnp.stack([c1, c2], axis=1)[:, :, None, :], (S5_GROUPS, 2, 8, 256))
    return ncat.astype(BF16), tz.astype(BF16), mcat.astype(BF16), coef.astype(F32)


def _ssd_kernel(xbc_ref, dt_ref, tri_ref, par_ref, exp_ref, y_ref, st_ref):
    @pl.when(pl.program_id(2) == 0)
    def _():
        st_ref[...] = jnp.zeros_like(st_ref)

    tri = tri_ref[0]
    mask = tri > 0
    xs = xbc_ref[0, :, 0:512]
    dt_full = _softplus(_dot_sel_rhs(dt_ref[0], exp_ref[...]) + par_ref[0, 0:1, :])
    a_full = par_ref[0, 1:2, :] * dt_full
    cs = _dot_sel_lhs(tri, a_full)
    tot = jnp.sum(a_full, axis=0, keepdims=True)
    xdt = xs * dt_full
    xdt_b = xdt.astype(BF16)
    xd_end = (xdt * jnp.exp(tot - cs)).astype(BF16)
    e_cs = jnp.exp(cs)
    e_tot = jnp.exp(tot)
    lane = lax.broadcasted_iota(jnp.int32, (SSD_CHUNK, 128), 1)
    lo = lane < SSD_HEADDIM
    zero_b = jnp.zeros((SSD_CHUNK, 128), BF16)
    for g in range(SSD_GROUPS):
        bm = xbc_ref[0, :, 512 + g * 128:512 + (g + 1) * 128]
        cm = xbc_ref[0, :, 768 + g * 128:768 + (g + 1) * 128]
        bm_b = bm.astype(BF16)
        cm_b = cm.astype(BF16)
        cb = lax.dot_general(cm_b, bm_b, (((1,), (1,)), ((), ())), preferred_element_type=F32)
        st_g = st_ref[:, g * 256:(g + 1) * 256]
        y_off = jnp.dot(cm_b, st_g.astype(BF16), preferred_element_type=F32) * e_cs[:, g * 256:(g + 1) * 256]
        for j in range(2):
            c0 = g * 256 + j * 128
            cs_t = cs[:, c0:c0 + 128].T
            x_pair = xdt_b[:, c0:c0 + 128]
            acc = y_off[:, j * 128:(j + 1) * 128]
            for hh in range(2):
                col = cs[:, c0 + hh * 64:c0 + hh * 64 + 1]
                row = cs_t[hh * 64:hh * 64 + 1, :]
                decay = jnp.exp(jnp.where(mask, col - row, -1e30))
                gm = (cb * decay).astype(BF16)
                xh = jnp.where(lo if hh == 0 else jnp.logical_not(lo), x_pair, zero_b)
                acc = acc + jnp.dot(gm, xh, preferred_element_type=F32)
            y_ref[0, 0, :, c0:c0 + 128] = acc
        st_ref[:, g * 256:(g + 1) * 256] = (
            st_g * e_tot[:, g * 256:(g + 1) * 256]
            + jnp.dot(bm.T.astype(BF16), xd_end[:, g * 256:(g + 1) * 256], preferred_element_type=F32))


def _ssd_scan(xbc_c, dt_raw, tri, par, expand, n_lat, n_ctx):
    bsz, t_all, _ = xbc_c.shape
    nc = n_lat + n_ctx

    def chunk(d, i):
        fwd = jnp.where(i < n_ctx, n_lat + i, i - n_ctx)
        bwd = jnp.where(i < n_ctx, nc - 1 - i, nc - 1 - i)
        return jnp.where(d == 0, fwd, bwd)

    return pl.pallas_call(
        _ssd_kernel,
        grid=(bsz, 2, nc),
        in_specs=[pl.BlockSpec((1, SSD_CHUNK, SSD_XBC), lambda b, d, i: (b, chunk(d, i), 0)),
                  pl.BlockSpec((1, SSD_CHUNK, 128), lambda b, d, i: (b, chunk(d, i), 0)),
                  pl.BlockSpec((1, SSD_CHUNK, SSD_CHUNK), lambda b, d, i: (d, 0, 0)),
                  pl.BlockSpec((1, 8, SSD_INNER), lambda b, d, i: (d, 0, 0)),
                  pl.BlockSpec((128, SSD_INNER), lambda b, d, i: (0, 0))],
        out_specs=pl.BlockSpec((1, 1, SSD_CHUNK, SSD_INNER), lambda b, d, i: (b, d, chunk(d, i), 0)),
        out_shape=jax.ShapeDtypeStruct((bsz, 2, t_all, SSD_INNER), F32),
        scratch_shapes=[pltpu.VMEM((SSD_STATE, SSD_INNER), F32)],
        compiler_params=_cparams(("parallel", "parallel", "arbitrary")),
        name="ssd_scan",
    )(xbc_c, dt_raw, tri, par, expand)


def _hy_lat_kernel(k1n, kgrp, v_ref, x1_ref, x2_ref, fa_ref, ga_ref, mf_ref, mi_ref, kr_ref, ki_ref, bias_ref,
                   o_ref, z_ref, a2_ref, b2_ref):
    order = pl.program_id(1)
    ph = pl.program_id(2)
    n2 = FFT_N2
    kp, h1 = fa_ref.shape
    ks = ga_ref.shape[1]
    seq = h1 * n2
    ngroups = k1n // kgrp
    dot = functools.partial(jnp.dot, preferred_element_type=F32)

    @pl.when(jnp.logical_and(order == 0, ph == 0))
    def _():
        for j in range(2):
            z_ref[j] = v_ref[0, :, j * 128:(j + 1) * 128].astype(F32)
            b2_ref[j, 2 * k1n * HY_B_PITCH:ks * HY_B_PITCH, :] = jnp.zeros(((ks - 2 * k1n) * HY_B_PITCH, 128), F32)

    @pl.when(ph == 0)
    def _():
        fa = fa_ref[...]

        def body(i, carry):
            u = jnp.concatenate([z_ref[j, pl.ds(i, h1, stride=n2), :] for j in range(2)], axis=1)
            r = dot(fa, u.astype(BF16))
            row = pl.multiple_of(i * HY_A_PITCH, 8)
            for j in range(2):
                a2_ref[j, pl.ds(row, kp), :] = r[:, j * 128:(j + 1) * 128]
            return carry

        lax.fori_loop(0, n2, body, 0)

    @pl.when(jnp.logical_and(ph >= 1, ph <= ngroups))
    def _():
        for t in range(kgrp):
            k1 = (ph - 1) * kgrp + t
            are = jnp.concatenate([a2_ref[j, pl.ds(2 * k1, n2, stride=HY_A_PITCH), :] for j in range(2)], axis=1)
            aim = jnp.concatenate([a2_ref[j, pl.ds(2 * k1 + 1, n2, stride=HY_A_PITCH), :] for j in range(2)], axis=1)
            x = dot(mf_ref[t], jnp.concatenate([are, aim], axis=0).astype(BF16))
            xr, xi = x[:n2], x[n2:]
            kr, ki = kr_ref[0, t], ki_ref[0, t]
            y = jnp.concatenate([xr * kr - xi * ki, xr * ki + xi * kr], axis=0).astype(BF16)
            bq = dot(mi_ref[t], y)
            row = pl.multiple_of(2 * k1 * HY_B_PITCH, 8)
            for j in range(2):
                b2_ref[j, pl.ds(row, n2), :] = bq[:n2, j * 128:(j + 1) * 128]
                b2_ref[j, pl.ds(row + HY_B_PITCH, n2), :] = bq[n2:, j * 128:(j + 1) * 128]

    @pl.when(ph == ngroups + 1)
    def _():
        ga = ga_ref[...]

        def body(i, carry):
            bs = jnp.concatenate([b2_ref[j, pl.ds(i, ks, stride=HY_B_PITCH), :] for j in range(2)], axis=1)
            y = dot(ga, bs.astype(BF16))
            for j in range(2):
                o_ref[0, j, pl.ds(i, h1, stride=n2), :] = y[:, j * 128:(j + 1) * 128]
            return carry

        lax.fori_loop(0, n2, body, 0)
        rb = min(256, seq)

        def gate(i, carry):
            r0 = pl.multiple_of(i * rb, 8)
            for j in range(2):
                conv = o_ref[0, j, pl.ds(r0, rb), :] + z_ref[j, pl.ds(r0, rb), :] * bias_ref[0, :, j * 128:(j + 1) * 128]
                x1 = x1_ref[0, pl.ds(r0, rb), j * 128:(j + 1) * 128].astype(F32)
                x2 = x2_ref[0, pl.ds(r0, rb), j * 128:(j + 1) * 128].astype(F32)
                z_ref[j, pl.ds(r0, rb), :] = x1 * conv
                o_ref[0, j, pl.ds(r0, rb), :] = x2 * conv
            return carry

        lax.fori_loop(0, seq // rb, gate, 0)


def _hyena_lat(v, x1, x2, fa, ga, mfwd, minv, kr, ki, bias, seq):
    bsz, t_all, w = v.shape
    n2 = FFT_N2
    kp, h1 = fa.shape
    ks = ga.shape[1]
    k1n = mfwd.shape[0]
    kgrp = 3 if k1n % 3 == 0 else 1
    ngroups = k1n // kgrp
    tok = pl.BlockSpec((1, seq, w), lambda b, o, p: (b, 0, 0))
    full = lambda shp: pl.BlockSpec(shp, lambda b, o, p: (0,) * len(shp))
    grp = lambda p: jnp.clip(p - 1, 0, ngroups - 1)
    mat = pl.BlockSpec((kgrp, 2 * n2, 2 * n2), lambda b, o, p: (grp(p), 0, 0))
    spec = pl.BlockSpec((1, kgrp, n2, w), lambda b, o, p: (o, grp(p), 0, 0))
    return pl.pallas_call(
        functools.partial(_hy_lat_kernel, k1n, kgrp),
        grid=(bsz, HY_ORDER, ngroups + 2),
        in_specs=[tok, tok, tok, full((kp, h1)), full((h1, ks)), mat, mat, spec, spec,
                  pl.BlockSpec((1, 1, w), lambda b, o, p: (o, 0, 0))],
        out_specs=pl.BlockSpec((1, 2, seq, 128), lambda b, o, p: (b, 0, 0, 0)),
        out_shape=jax.ShapeDtypeStruct((bsz, 2, t_all, 128), F32),
        scratch_shapes=[pltpu.VMEM((2, seq, 128), F32),
                        pltpu.VMEM((2, n2 * HY_A_PITCH, 128), F32),
                        pltpu.VMEM((2, ks * HY_B_PITCH, 128), F32)],
        compiler_params=_cparams(("parallel", "arbitrary", "arbitrary"), V7X_VMEM_LIMIT_MB),
        name="hyena_lat",
    )(v, x1, x2, fa, ga, mfwd, minv, kr, ki, bias)


def _hyena_lat_consts(seq):
    n = 2 * seq
    n2 = FFT_N2
    n1 = n // n2
    h1 = n1 // 2
    k1n = n1 // 2 + 1
    kp = -(-2 * k1n // 8) * 8
    ks = -(-2 * k1n // 16) * 16
    assert kp <= HY_A_PITCH and n2 <= HY_B_PITCH
    k1 = np.arange(k1n)
    m1 = np.arange(h1)
    ang = 2.0 * np.pi * np.outer(k1, m1) / n1
    fa = np.zeros((kp, h1))
    fa[0:2 * k1n:2] = np.cos(ang)
    fa[1:2 * k1n:2] = -np.sin(ang)
    wgt = np.where((k1 == 0) | (k1 == n1 // 2), 1.0, 2.0) / n
    ga = np.zeros((h1, ks))
    ga[:, 0:2 * k1n:2] = (np.cos(ang) * wgt[:, None]).T
    ga[:, 1:2 * k1n:2] = (-np.sin(ang) * wgt[:, None]).T
    k2 = np.arange(n2)
    m2 = np.arange(n2)
    kk = k1[:, None, None] + n1 * k2[None, :, None]
    th = 2.0 * np.pi * ((kk * m2[None, None, :]) % n) / n
    mc, ms = np.cos(th), np.sin(th)
    mfwd = np.concatenate([np.concatenate([mc, ms], axis=2), np.concatenate([-ms, mc], axis=2)], axis=1)
    mct, mst = np.transpose(mc, (0, 2, 1)), np.transpose(ms, (0, 2, 1))
    minv = np.concatenate([np.concatenate([mct, -mst], axis=2), np.concatenate([mst, mct], axis=2)], axis=1)
    kidx = (k1[:, None] + n1 * k2[None, :])
    return (jnp.asarray(fa, BF16), jnp.asarray(ga, BF16), jnp.asarray(mfwd, BF16), jnp.asarray(minv, BF16), kidx)


def _hy_ctx_kernel(v_ref, x1_ref, x2_ref, fc_ref, fs_ref, gc_ref, gs_ref, kr_ref, ki_ref, bias_ref, zlat_ref, z_ref):
    del zlat_ref
    d = functools.partial(jnp.dot, preferred_element_type=F32)

    def conv(u, o):
        ub = u.astype(BF16)
        cr, ci = d(fc_ref[...], ub), d(fs_ref[...], ub)
        kr, ki = kr_ref[o], ki_ref[o]
        pr = (cr * kr - ci * ki).astype(BF16)
        pi = (cr * ki + ci * kr).astype(BF16)
        return d(gc_ref[...], pr) + d(gs_ref[...], pi) + u * bias_ref[o]

    z = x1_ref[0].astype(F32) * conv(v_ref[0].astype(F32), 0)
    z = x2_ref[0].astype(F32) * conv(z, 1)
    for j in range(2):
        z_ref[0, j] = z[:, j * 128:(j + 1) * 128]


def _hyena_ctx(v, x1, x2, fc, fs, gc, gs, kr, ki, bias, z_all, seq, ctx_len):
    bsz, _, w = v.shape
    blk = seq // ctx_len
    kpad = fc.shape[0]
    tok = pl.BlockSpec((1, ctx_len, w), lambda b: (b, blk, 0))
    full = lambda shp: pl.BlockSpec(shp, lambda b: (0,) * len(shp))
    return pl.pallas_call(
        _hy_ctx_kernel, grid=(bsz,),
        in_specs=[tok, tok, tok, full((kpad, ctx_len)), full((kpad, ctx_len)), full((ctx_len, kpad)),
                  full((ctx_len, kpad)), full((2, kpad, w)), full((2, kpad, w)), full((2, 1, w)),
                  pl.BlockSpec(memory_space=pl.ANY)],
        out_specs=pl.BlockSpec((1, 2, ctx_len, 128), lambda b: (b, 0, blk, 0)),
        out_shape=jax.ShapeDtypeStruct(z_all.shape, F32),
        input_output_aliases={10: 0},
        compiler_params=_cparams(("parallel",)),
        name="hyena_ctx",
    )(v, x1, x2, fc, fs, gc, gs, kr, ki, bias, z_all)


def _hyena_ctx_consts(ctx_len):
    n = 2 * ctx_len
    nk = ctx_len + 1
    kpad = -(-nk // 128) * 128
    k = np.arange(nk)
    m = np.arange(ctx_len)
    ang = 2.0 * np.pi * np.outer(k, m) / n
    fc = np.zeros((kpad, ctx_len))
    fs = np.zeros((kpad, ctx_len))
    fc[:nk] = np.cos(ang)
    fs[:nk] = -np.sin(ang)
    wgt = np.where((k == 0) | (k == ctx_len), 1.0, 2.0) / n
    gc = np.zeros((ctx_len, kpad))
    gs = np.zeros((ctx_len, kpad))
    gc[:, :nk] = (np.cos(ang) * wgt[:, None]).T
    gs[:, :nk] = (-np.sin(ang) * wgt[:, None]).T
    return tuple(jnp.asarray(t, BF16) for t in (fc, fs, gc, gs)), kpad


def _hyena_filter_spectrum(length, w1, b1, w2, b2, w3, freq, decay):
    pos = jnp.arange(length, dtype=F32)
    t = pos / max(length - 1, 1)
    bands = jnp.linspace(1e-4, HY_BANDS - 1, HY_BANDS, dtype=F32)
    ang = (2.0 * math.pi / length) * pos[:, None] * bands
    feats = jnp.concatenate([t[:, None], jnp.cos(ang), -jnp.sin(ang)], axis=-1)
    freq = freq.astype(F32)
    mm = functools.partial(jnp.matmul, precision=HI)
    hid = jnp.sin(freq * (mm(feats, w1.astype(F32)) + b1.astype(F32)))
    hid = jnp.sin(freq * (mm(hid, w2.astype(F32)) + b2.astype(F32)))
    h = mm(hid, w3.astype(F32)) * jnp.exp(-t[:, None] * jnp.abs(decay.astype(F32)))
    h = h.reshape(length, 2, HY_ORDER, HY_WIDTH)
    h_fwd, h_bwd = h[:, 0], h[:, 1]
    k = jnp.concatenate([h_fwd[:1] + h_bwd[:1], h_fwd[1:], jnp.zeros_like(h_fwd[:1]), h_bwd[:0:-1]], axis=0)
    k = k / jnp.sum(jnp.abs(k), axis=0, keepdims=True)
    return jnp.fft.rfft(k, axis=0)


def _mixout_kernel(x_ref, m_ref, s5y_ref, s5u_ref, yf_ref, yb_ref, xs_ref, z_ref, hy_ref,
                   wglu_ref, vec256_ref, vec512_ref, wout_ref, ln_ref, o_ref):
    m = m_ref[0, 0]
    d = functools.partial(jnp.dot, preferred_element_type=F32)
    y5 = _gelu_tanh(s5y_ref[0] + vec256_ref[0:1, :] * s5u_ref[0])
    y5 = y5 * _sigmoid(d(y5.astype(BF16), wglu_ref[...]) + vec256_ref[1:2, :])
    ys = yf_ref[0, 0] + yb_ref[0, 0] + vec512_ref[0:1, :] * xs_ref[0]
    gsd = ys * _silu(z_ref[0])
    gsd = gsd * lax.rsqrt(jnp.mean(gsd * gsd, axis=-1, keepdims=True) + LN_EPS) * vec512_ref[1:2, :]
    mix = (d(y5.astype(BF16), wout_ref[0:256, :]) + d(gsd.astype(BF16), wout_ref[256:768, :])
           + d(hy_ref[0, 0].astype(BF16), wout_ref[768:896, :]) + d(hy_ref[0, 1].astype(BF16), wout_ref[896:1024, :]))
    r = ALPHA * x_ref[0] + m[2:3] * mix
    o_ref[0] = _standardise(r) * ln_ref[0:1, :] + ln_ref[1:2, :]


def _mixout(x_all, mods, s5y, s5u, yssd, xbc_c, z, hy, wglu, vec256, vec512, wout, ln, nt_lat, rows):
    bsz = x_all.shape[0]
    nt = rows // ROW_TILE
    tok = lambda w: pl.BlockSpec((1, ROW_TILE, w), lambda b, i: (b, i, 0))
    full = lambda shp: pl.BlockSpec(shp, lambda b, i: (0,) * len(shp))
    return pl.pallas_call(
        _mixout_kernel, grid=(bsz, nt),
        in_specs=[tok(D_MODEL),
                  pl.BlockSpec((1, 1, 6, D_MODEL), lambda b, i: (b, jnp.where(i < nt_lat, 0, 1), 0, 0)),
                  tok(256), tok(256),
                  pl.BlockSpec((1, 1, ROW_TILE, SSD_INNER), lambda b, i: (b, 0, i, 0)),
                  pl.BlockSpec((1, 1, ROW_TILE, SSD_INNER), lambda b, i: (b, 1, i, 0)),
                  tok(SSD_INNER), tok(SSD_INNER),
                  pl.BlockSpec((1, 2, ROW_TILE, 128), lambda b, i: (b, 0, i, 0)),
                  full((256, 256)), full((2, 256)), full((2, 512)), full((D_MODEL, D_MODEL)), full((2, D_MODEL))],
        out_specs=tok(D_MODEL),
        out_shape=jax.ShapeDtypeStruct((bsz, rows, D_MODEL), F32),
        compiler_params=_cparams(("parallel", "parallel"), 40),
        name="mixout",
    )(x_all, mods, s5y, s5u, yssd, yssd, xbc_c, z, hy, wglu, vec256, vec512, wout, ln)


def _ffn_kernel(x_ref, m_ref, win_ref, wout_ref, ln_ref, o_ref):
    m = m_ref[0, 0]
    x = x_ref[0]
    d = functools.partial(jnp.dot, preferred_element_type=F32)
    h = (_standardise(x) * (1.0 + m[4:5]) + m[3:4]).astype(BF16)
    gate = d(h, win_ref[:, 0:FFN_HIDDEN])
    up = d(h, win_ref[:, FFN_HIDDEN:2 * FFN_HIDDEN])
    act = (_silu(gate) * up).astype(BF16)
    r = ALPHA * x + m[5:6] * d(act, wout_ref[...])
    o_ref[0] = _standardise(r) * ln_ref[0:1, :] + ln_ref[1:2, :]


def _ffn(x_all, mods, win, wout, ln, nt_lat):
    bsz, t_all, _ = x_all.shape
    nt = t_all // ROW_TILE
    tok = pl.BlockSpec((1, ROW_TILE, D_MODEL), lambda b, i: (b, i, 0))
    const = lambda shp: pl.BlockSpec(shp, lambda b, i: (0, 0), pipeline_mode=pl.Buffered(1))
    return pl.pallas_call(
        _ffn_kernel, grid=(bsz, nt),
        in_specs=[tok,
                  pl.BlockSpec((1, 1, 6, D_MODEL), lambda b, i: (b, jnp.where(i < nt_lat, 0, 1), 0, 0)),
                  const((D_MODEL, 2 * FFN_HIDDEN)), const((FFN_HIDDEN, D_MODEL)),
                  pl.BlockSpec((2, D_MODEL), lambda b, i: (0, 0))],
        out_specs=tok,
        out_shape=jax.ShapeDtypeStruct((bsz, t_all, D_MODEL), F32),
        compiler_params=_cparams(("parallel", "parallel"), V7X_VMEM_LIMIT_MB),
        name="ffn",
    )(x_all, mods, win, wout, ln)


def _layer(x_all, mods, p, seq, ctx_len, want_ctx):
    bsz, t_all, _ = x_all.shape
    nt_lat = seq // ROW_TILE
    s5u, z, xbc, hy, dt_raw = _inproj(x_all, mods, p["w_in"], nt_lat)

    q = S5_CHUNK
    nch = t_all // q
    ut = s5u.astype(BF16).reshape(bsz, nch, q, S5_GROUPS, S5_GROUP_CH)
    ut = jnp.transpose(ut, (3, 1, 0, 2, 4)).reshape(S5_GROUPS, nch * bsz, q * S5_GROUP_CH)
    yt = _s5_scan(ut, p["s5_ncat"], p["s5_tz"], p["s5_mcat"], p["s5_coef"], seq // q, ctx_len // q)
    s5y = jnp.transpose(yt.reshape(S5_GROUPS, nch, bsz, q, S5_GROUP_CH), (2, 1, 3, 0, 4)).reshape(bsz, t_all, S5_WIDTH)

    (xbc_c,) = _dwconv(xbc, p["ssd_conv_w"], p["ssd_conv_b"], SSD_CONV, True, nt_lat, 1, F32)
    yssd = _ssd_scan(xbc_c, dt_raw, p["ssd_tri"], p["ssd_par"], p["ssd_expand"],
                     seq // SSD_CHUNK, ctx_len // SSD_CHUNK)

    v, x1, x2 = _dwconv(hy, p["hy_conv_w"], p["hy_conv_b"], HY_SHORT, False, nt_lat, 3, BF16)
    hyz = _hyena_lat(v, x1, x2, p["hy_fa"], p["hy_ga"], p["hy_mfwd"], p["hy_minv"],
                     p["hy_kr"], p["hy_ki"], p["hy_bias"], seq)
    if want_ctx:
        hyz = _hyena_ctx(v, x1, x2, *p["hy_ctx_mats"], p["hy_ctx_kr"], p["hy_ctx_ki"], p["hy_bias"], hyz, seq, ctx_len)

    rows = t_all if want_ctx else seq
    x1_all = _mixout(x_all, mods, s5y, s5u, yssd, xbc_c, z, hyz, p["s5_wglu"], p["s5_vec"], p["ssd_vec"],
                     p["w_out"], p["ln1"], nt_lat, rows)
    return _ffn(x1_all, mods, p["ffn_w_in"], p["ffn_w_out"], p["ln2"], nt_lat)


def kernel(x, c, ctx, c_ctx, w_mod, b_mod, w_in, s5_lam_re, s5_lam_im, s5_log_step, s5_b_re, s5_b_im, s5_c_re, s5_c_im, s5_d, s5_w_glu, s5_b_glu, ssd_conv_w, ssd_conv_b, ssd_dt_bias, ssd_a_log, ssd_d, ssd_norm_w, hy_conv_w, hy_conv_b, hy_w1, hy_b1, hy_w2, hy_b2, hy_w3, hy_freq, hy_decay, hy_bias, w_out, ln1_g, ln1_b, ffn_w_in, ffn_w_out, ln2_g, ln2_b):
    bsz, seq, _ = x.shape
    ctx_len = ctx.shape[1]
    assert bsz == 8 and seq % ROW_TILE == 0 and ctx_len == ROW_TILE

    fa, ga, mfwd, minv, kidx = _hyena_lat_consts(seq)
    ctx_mats, kpad_ctx = _hyena_ctx_consts(ctx_len)
    tt = np.arange(SSD_CHUNK)
    tri = jnp.asarray(np.stack([tt[None, :] <= tt[:, None], tt[None, :] >= tt[:, None]]), BF16)
    expand = jnp.asarray(np.repeat(np.eye(128, SSD_HEADS), SSD_HEADDIM, axis=1)[:, :SSD_INNER], BF16)
    cvec = jnp.zeros((16, D_MODEL), F32).at[:bsz].set(c.astype(F32)).at[bsz].set(c_ctx.astype(F32))

    x_all = jnp.concatenate([x.astype(F32), ctx.astype(F32)], axis=1)
    for l in range(DEPTH):
        want_ctx = l < DEPTH - 1
        mod16 = _modulation(cvec, w_mod[l].astype(F32), b_mod[l].astype(F32).reshape(1, -1))
        mods = jnp.stack([mod16[:bsz].reshape(bsz, 6, D_MODEL),
                          jnp.broadcast_to(mod16[bsz].reshape(1, 6, D_MODEL), (bsz, 6, D_MODEL))], axis=1)
        wl = w_in[l]
        w_p = jnp.concatenate([wl[:, 0:256], wl[:, 256:768], wl[:, 768:1792], wl[:, 1800:2568], wl[:, 1792:1800],
                               jnp.zeros((D_MODEL, 120), wl.dtype)], axis=1).astype(BF16)
        ncat, tz, mcat, coef = _s5_weights(s5_lam_re[l], s5_lam_im[l], s5_log_step[l], s5_b_re[l], s5_b_im[l],
                                           s5_c_re[l], s5_c_im[l])
        rep = lambda t: jnp.repeat(t.astype(F32), SSD_HEADDIM, axis=-1)
        par = jnp.zeros((2, 8, SSD_INNER), F32)
        par = par.at[:, 0].set(rep(ssd_dt_bias[l])).at[:, 1].set(rep(-jnp.exp(ssd_a_log[l].astype(F32))))
        spec = _hyena_filter_spectrum(seq, hy_w1[l], hy_b1[l], hy_w2[l], hy_b2[l], hy_w3[l], hy_freq[l], hy_decay[l])
        lo = spec[jnp.asarray(np.minimum(kidx, 2 * seq - kidx))]
        kr = jnp.transpose(jnp.real(lo), (2, 0, 1, 3))
        ki = jnp.transpose(jnp.where(jnp.asarray(kidx > seq)[..., None, None], -jnp.imag(lo), jnp.imag(lo)), (2, 0, 1, 3))
        p = dict(
            w_in=w_p, s5_ncat=ncat, s5_tz=tz, s5_mcat=mcat, s5_coef=coef,
            ssd_conv_w=ssd_conv_w[l].astype(F32), ssd_conv_b=ssd_conv_b[l].astype(F32),
            ssd_tri=tri, ssd_par=par, ssd_expand=expand,
            hy_conv_w=hy_conv_w[l].astype(F32), hy_conv_b=hy_conv_b[l].astype(F32),
            hy_fa=fa, hy_ga=ga, hy_mfwd=mfwd, hy_minv=minv, hy_kr=kr.astype(F32), hy_ki=ki.astype(F32),
            hy_bias=hy_bias[l].astype(F32)[:, None, :],
            s5_wglu=s5_w_glu[l].astype(BF16),
            s5_vec=jnp.stack([s5_d[l], s5_b_glu[l]]).astype(F32),
            ssd_vec=jnp.stack([rep(ssd_d[l]), ssd_norm_w[l].astype(F32)]),
            w_out=w_out[l].astype(BF16), ln1=jnp.stack([ln1_g[l], ln1_b[l]]).astype(F32),
            ffn_w_in=ffn_w_in[l].astype(BF16), ffn_w_out=ffn_w_out[l].astype(BF16),
            ln2=jnp.stack([ln2_g[l], ln2_b[l]]).astype(F32),
        )
        if want_ctx:
            cspec = _hyena_filter_spectrum(ctx_len, hy_w1[l], hy_b1[l], hy_w2[l], hy_b2[l], hy_w3[l],
                                           hy_freq[l], hy_decay[l])
            cspec = jnp.transpose(cspec, (1, 0, 2))
            padk = ((0, 0), (0, kpad_ctx - cspec.shape[1]), (0, 0))
            p.update(hy_ctx_mats=ctx_mats, hy_ctx_kr=jnp.pad(jnp.real(cspec), padk).astype(F32),
                     hy_ctx_ki=jnp.pad(jnp.imag(cspec), padk).astype(F32))
        x_all = _layer(x_all, mods, p, seq, ctx_len, want_ctx)
    return x_all.astype(x.dtype)
```

```python
import functools
import math

import numpy as np
import jax
import jax.numpy as jnp
from jax import lax
from jax.experimental import pallas as pl
from jax.experimental.pallas import tpu as pltpu

F32 = jnp.float32
BF16 = jnp.bfloat16
HI = lax.Precision.HIGHEST

D_MODEL = 1024
DEPTH = 2
S5_WIDTH = 256
S5_GROUP_CH = 16
S5_GROUPS = 16
S5_STATE = 64
S5_MAX_RE = -1e-4
S5_CHUNK = 16
SSD_INNER = 512
SSD_HEADDIM = 64
SSD_HEADS = 8
SSD_GROUPS = 2
SSD_STATE = 128
SSD_CONV = 5
SSD_CHUNK = 128
SSD_XBC = 1024
HY_WIDTH = 256
HY_ORDER = 2
HY_SHORT = 3
HY_BANDS = 16
HY_IN = 768
MIX_IN_PAD = 2688
FFN_HIDDEN = 2816
ALPHA = (2 * DEPTH) ** 0.25
LN_EPS = 1e-6

ROW_TILE = 256
FFT_N2 = 128
HY_A_PITCH = 72
HY_B_PITCH = 136
V7X_VMEM_LIMIT_MB = 56


def _cparams(sem, vmem_mb=None):
    kw = dict(dimension_semantics=sem)
    if vmem_mb is not None:
        kw["vmem_limit_bytes"] = vmem_mb * 2 ** 20
    return pltpu.CompilerParams(**kw)


def _standardise(x):
    mu = jnp.mean(x, axis=-1, keepdims=True)
    xc = x - mu
    var = jnp.mean(xc * xc, axis=-1, keepdims=True)
    return xc * lax.rsqrt(var + LN_EPS)


def _sigmoid(x):
    return 1.0 / (1.0 + jnp.exp(-x))


def _silu(x):
    return x * _sigmoid(x)


def _gelu_tanh(x):
    return 0.5 * x * (1.0 + jnp.tanh(0.7978845608028654 * (x + 0.044715 * (x * x * x))))


def _softplus(x):
    return jnp.maximum(x, 0.0) + jnp.log(1.0 + jnp.exp(-jnp.abs(x)))


def _bdot(a, b):
    return jnp.dot(a.astype(BF16), b.astype(BF16), preferred_element_type=F32)


def _split3(a):
    a1 = a.astype(BF16)
    r1 = a - a1.astype(F32)
    a2 = r1.astype(BF16)
    a3 = (r1 - a2.astype(F32)).astype(BF16)
    return a1, a2, a3


def _dot_sel_lhs(sel, a):
    a1, a2, a3 = _split3(a)
    d = functools.partial(jnp.dot, preferred_element_type=F32)
    return d(sel, a1) + d(sel, a2) + d(sel, a3)


def _dot_sel_rhs(a, sel):
    a1, a2, a3 = _split3(a)
    d = functools.partial(jnp.dot, preferred_element_type=F32)
    return d(a1, sel) + d(a2, sel) + d(a3, sel)


def _mod_kernel(c_ref, w_ref, b_ref, o_ref):
    ca = _silu(c_ref[...])
    c1, c2, c3 = _split3(ca)
    w1, w2, w3 = _split3(w_ref[...])
    d = functools.partial(jnp.dot, preferred_element_type=F32)
    acc = d(c1, w1) + d(c1, w2) + d(c2, w1) + d(c1, w3) + d(c2, w2) + d(c3, w1)
    o_ref[...] = acc + b_ref[...]


def _modulation(cvec, w, b):
    n = w.shape[1]
    tn = 1536
    return pl.pallas_call(
        _mod_kernel,
        grid=(n // tn,),
        in_specs=[pl.BlockSpec((16, D_MODEL), lambda j: (0, 0)),
                  pl.BlockSpec((D_MODEL, tn), lambda j: (0, j)),
                  pl.BlockSpec((1, tn), lambda j: (0, j))],
        out_specs=pl.BlockSpec((16, tn), lambda j: (0, j)),
        out_shape=jax.ShapeDtypeStruct((16, n), F32),
        compiler_params=_cparams(("arbitrary",), 40),
        name="modulation",
    )(cvec, w, b)


def _rows_of(nt_lat, xl_ref, xc_ref):
    return jnp.where(pl.program_id(1) < nt_lat, xl_ref[0], xc_ref[0])


def _row_specs(nt_lat, ctx_blk, width):
    return [pl.BlockSpec((1, ROW_TILE, width), lambda b, i: (b, jnp.minimum(i, nt_lat - 1), 0)),
            pl.BlockSpec((1, ROW_TILE, width), lambda b, i: (b, ctx_blk, 0))]


def _inproj_kernel(nt_lat, xl_ref, xc_ref, m_ref, w_ref, s5_ref, z_ref, xbc_ref, hy_ref, dt_ref):
    m = m_ref[0, 0]
    h = (_standardise(_rows_of(nt_lat, xl_ref, xc_ref)) * (1.0 + m[1:2]) + m[0:1]).astype(BF16)
    d = functools.partial(jnp.dot, preferred_element_type=F32)
    s5_ref[0] = d(h, w_ref[:, 0:256])
    z_ref[0] = d(h, w_ref[:, 256:768]).astype(z_ref.dtype)
    xbc_ref[0] = d(h, w_ref[:, 768:1792])
    hy_ref[0] = d(h, w_ref[:, 1792:2560])
    dt_ref[0] = d(h, w_ref[:, 2560:2688])


def _inproj(x_lat, x_ctx, ctx_blk, mods, w_p, nt_lat):
    bsz = x_lat.shape[0]
    nt = nt_lat + 1
    t_all = nt * ROW_TILE
    widths = (256, 512, 1024, 768, 128)
    return pl.pallas_call(
        functools.partial(_inproj_kernel, nt_lat),
        grid=(bsz, nt),
        in_specs=_row_specs(nt_lat, ctx_blk, D_MODEL) + [
                  pl.BlockSpec((1, 1, 6, D_MODEL), lambda b, i: (b, jnp.where(i < nt_lat, 0, 1), 0, 0)),
                  pl.BlockSpec((D_MODEL, MIX_IN_PAD), lambda b, i: (0, 0))],
        out_specs=[pl.BlockSpec((1, ROW_TILE, w), lambda b, i: (b, i, 0)) for w in widths],
        out_shape=[jax.ShapeDtypeStruct((bsz, t_all, w), dt) for w, dt in zip(widths, (F32, BF16, F32, F32, F32))],
        compiler_params=_cparams(("parallel", "parallel"), 40),
        name="inproj",
    )(x_lat, x_ctx, mods, w_p)


def _dwconv_kernel(taps, act, nt_lat, n_out, xm_ref, xp_ref, xn_ref, w_ref, b_ref, *rest):
    o_refs, ext_ref = rest[:n_out], rest[n_out]
    i = pl.program_id(1)
    tm = xm_ref.shape[1]
    ch = xm_ref.shape[2]
    has_prev = jnp.logical_and(i > 0, i < nt_lat)
    has_next = i < nt_lat - 1
    ext_ref[0:8, :] = jnp.where(has_prev, xp_ref[0], 0.0)
    ext_ref[8:8 + tm, :] = xm_ref[0]
    ext_ref[8 + tm:16 + tm, :] = jnp.where(has_next, xn_ref[0], 0.0)
    pad = taps // 2
    rb = 64
    wo = ch // n_out
    for c0 in range(0, ch, 128):
        wk = [w_ref[k:k + 1, c0:c0 + 128] for k in range(taps)]
        bias = b_ref[0:1, c0:c0 + 128]
        for r0 in range(0, tm, rb):
            acc = bias + wk[0] * ext_ref[8 - pad + r0:8 - pad + r0 + rb, c0:c0 + 128]
            for k in range(1, taps):
                acc = acc + wk[k] * ext_ref[8 - pad + k + r0:8 - pad + k + r0 + rb, c0:c0 + 128]
            if act:
                acc = _silu(acc)
            o_ref = o_refs[c0 // wo]
            o_ref[0, r0:r0 + rb, (c0 % wo):(c0 % wo) + 128] = acc.astype(o_ref.dtype)


def _dwconv(x, w, b, taps, act, nt_lat, n_out, out_dtype):
    bsz, t_all, ch = x.shape
    nt = t_all // ROW_TILE
    r8 = ROW_TILE // 8
    last8 = t_all // 8 - 1
    wo = ch // n_out
    return pl.pallas_call(
        functools.partial(_dwconv_kernel, taps, act, nt_lat, n_out),
        grid=(bsz, nt),
        in_specs=[pl.BlockSpec((1, ROW_TILE, ch), lambda b_, i: (b_, i, 0)),
                  pl.BlockSpec((1, 8, ch), lambda b_, i: (b_, jnp.maximum(i * r8 - 1, 0), 0)),
                  pl.BlockSpec((1, 8, ch), lambda b_, i: (b_, jnp.minimum((i + 1) * r8, last8), 0)),
                  pl.BlockSpec((taps, ch), lambda b_, i: (0, 0)),
                  pl.BlockSpec((1, ch), lambda b_, i: (0, 0))],
        out_specs=[pl.BlockSpec((1, ROW_TILE, wo), lambda b_, i: (b_, i, 0)) for _ in range(n_out)],
        out_shape=[jax.ShapeDtypeStruct((bsz, t_all, wo), out_dtype) for _ in range(n_out)],
        scratch_shapes=[pltpu.VMEM((ROW_TILE + 16, ch), F32)],
        compiler_params=_cparams(("parallel", "parallel")),
        name="dwconv%d" % taps,
    )(x, x, x, w, b.reshape(1, ch))


def _s5_kernel(n_lat, n_ctx, u0_ref, u1_ref, ncat_ref, tz_ref, mcat_ref, coef_ref, dvec_ref, y_ref,
               ut_ref, s_ref, hp_ref):
    q, gch, ng = S5_CHUNK, S5_GROUP_CH, S5_GROUPS
    nc = n_lat + n_ctx
    dot = functools.partial(jnp.dot, preferred_element_type=F32)
    u_refs = (u0_ref, u1_ref)
    gpl = 128 // gch
    for j in range(2):
        for s in range(q):
            rows = u_refs[j][0, pl.ds(s, nc, stride=q), :]
            for gg in range(gpl):
                ut_ref[gpl * j + gg, :, s * gch:(s + 1) * gch] = rows[:, gg * gch:(gg + 1) * gch]
    for g in range(ng):
        sg = dot(ut_ref[g].astype(BF16), ncat_ref[g])
        for k in range(4):
            s_ref[k, pl.ds(g, nc, stride=ng), :] = sg[:, k * 128:(k + 1) * 128]
    c1f, c2f, c1b, c2b = coef_ref[0], coef_ref[1], coef_ref[2], coef_ref[3]

    def step(cf, cb, carry):
        hf, hsf, hb, hsb = carry
        rf = pl.multiple_of(cf * ng, ng)
        rb = pl.multiple_of(cb * ng, ng)
        hp_ref[0, pl.ds(rf, ng), :] = hf
        hp_ref[1, pl.ds(rb, ng), :] = hb
        sf = s_ref[0, pl.ds(rf, ng), :]
        sb = s_ref[1, pl.ds(rb, ng), :]
        ssf = s_ref[2, pl.ds(rf, ng), :]
        ssb = s_ref[3, pl.ds(rb, ng), :]
        return (c1f * hf + c2f * hsf + sf, c1f * hsf - c2f * hf + ssf,
                c1b * hb + c2b * hsb + sb, c1b * hsb - c2b * hb + ssb)

    z = jnp.zeros((ng, 128), F32)
    carry = lax.fori_loop(0, n_ctx, lambda i, c: step(n_lat + i, n_lat + n_ctx - 1 - i, c), (z, z, z, z))
    lax.fori_loop(0, n_lat, lambda i, c: step(i, n_lat - 1 - i, c), carry)
    for g in range(ng):
        ug = ut_ref[g]
        hp = jnp.concatenate([hp_ref[0, pl.ds(g, nc, stride=ng), :], hp_ref[1, pl.ds(g, nc, stride=ng), :]], axis=1)
        ut_ref[g] = dot(ug.astype(BF16), tz_ref[g]) + dot(hp.astype(BF16), mcat_ref[g]) + ug * dvec_ref[g]
    for j in range(2):
        for s in range(q):
            rows = jnp.concatenate([ut_ref[gpl * j + gg, :, s * gch:(s + 1) * gch] for gg in range(gpl)], axis=1)
            y_ref[0, j, pl.ds(s, nc, stride=q), :] = rows


def _s5_scan(s5u, ncat, tz, mcat, coef, dvec, n_lat, n_ctx):
    bsz, t_all, _ = s5u.shape
    nc = n_lat + n_ctx
    ng = S5_GROUPS
    const = lambda shp: pl.BlockSpec(shp, lambda b: (0,) * len(shp), pipeline_mode=pl.Buffered(1))
    return pl.pallas_call(
        functools.partial(_s5_kernel, n_lat, n_ctx),
        grid=(bsz,),
        in_specs=[pl.BlockSpec((1, t_all, 128), lambda b: (b, 0, 0)),
                  pl.BlockSpec((1, t_all, 128), lambda b: (b, 0, 1)),
                  const((ng, 256, 512)), const((ng, 256, 256)), const((ng, 256, 256)),
                  const((4, ng, 128)), const((ng, 1, 256))],
        out_specs=pl.BlockSpec((1, 2, t_all, 128), lambda b: (b, 0, 0, 0)),
        out_shape=jax.ShapeDtypeStruct((bsz, 2, t_all, 128), F32),
        scratch_shapes=[pltpu.VMEM((ng, nc, 256), F32), pltpu.VMEM((4, nc * ng, 128), F32),
                        pltpu.VMEM((2, nc * ng, 128), F32)],
        compiler_params=_cparams(("parallel",), V7X_VMEM_LIMIT_MB),
        name="s5_scan",
    )(s5u, s5u, ncat, tz, mcat, coef, dvec)


def _s5_weights(lam_re, lam_im, log_step, b_re, b_im, c_re, c_im):
    q = S5_CHUNK
    lam = lax.complex(jnp.minimum(lam_re.astype(F32), S5_MAX_RE), lam_im.astype(F32))
    step = jnp.exp(log_step.astype(F32))[..., None]
    lam_bar = jnp.exp(lam * step)
    b_bar = ((lam_bar - 1.0) / lam)[..., None] * lax.complex(b_re.astype(F32), b_im.astype(F32))
    c_mat = lax.complex(c_re.astype(F32), c_im.astype(F32))
    taus = jnp.arange(q + 1, dtype=F32)
    pw = jnp.exp((lam * step)[None] * taus[:, None, None, None])
    kern = jnp.real(jnp.einsum('dghp,tdgp,dgpk->tdghk', c_mat, pw[:q], b_bar, precision=HI))
    tt = jnp.arange(q)
    lag = tt[None, :] - tt[:, None]
    kf = jnp.where((lag >= 0)[..., None, None, None], kern[jnp.clip(lag, 0, q - 1), 0], 0.0)
    kb = jnp.where((lag <= 0)[..., None, None, None], kern[jnp.clip(-lag, 0, q - 1), 1], 0.0)
    tz = jnp.transpose(kf + kb, (2, 0, 4, 1, 3)).reshape(S5_GROUPS, q * S5_GROUP_CH, q * S5_GROUP_CH)
    nf = jnp.einsum('sgp,gpk->gskp', pw[q - 1 - tt, 0], b_bar[0]).reshape(S5_GROUPS, q * S5_GROUP_CH, S5_STATE)
    nb = jnp.einsum('sgp,gpk->gskp', pw[tt, 1], b_bar[1]).reshape(S5_GROUPS, q * S5_GROUP_CH, S5_STATE)
    ncat = jnp.concatenate([jnp.real(nf), jnp.imag(nf), jnp.real(nb), jnp.imag(nb),
                            jnp.imag(nf), jnp.real(nf), jnp.imag(nb), jnp.real(nb)], axis=-1)
    mf = jnp.einsum('ghp,tgp->gpth', c_mat[0], pw[tt + 1, 0]).reshape(S5_GROUPS, S5_STATE, q * S5_GROUP_CH)
    mb = jnp.einsum('ghp,tgp->gpth', c_mat[1], pw[q - tt, 1]).reshape(S5_GROUPS, S5_STATE, q * S5_GROUP_CH)
    mcat = jnp.concatenate([jnp.real(mf), -jnp.imag(mf), jnp.real(mb), -jnp.imag(mb)], axis=1)
    lq = pw[q]
    lr, li = jnp.real(lq), jnp.imag(lq)
    coef = jnp.stack([jnp.concatenate([lr[0], lr[0]], axis=-1), jnp.concatenate([-li[0], li[0]], axis=-1),
                      jnp.concatenate([lr[1], lr[1]], axis=-1), jnp.concatenate([-li[1], li[1]], axis=-1)])
    return ncat.astype(BF16), tz.astype(BF16), mcat.astype(BF16), coef.astype(F32)


def _ssd_kernel(xbc_ref, dt_ref, tri_ref, par_ref, exp_ref, y_ref, st_ref):
    @pl.when(pl.program_id(2) == 0)
    def _():
        st_ref[...] = jnp.zeros_like(st_ref)

    tri = tri_ref[0]
    mask = tri > 0
    xs = xbc_ref[0, :, 0:512].astype(F32)
    dt_c = _softplus(dt_ref[0] + par_ref[0, 0:1, :])
    a_c = par_ref[0, 1:2, :] * dt_c
    cs = _dot_sel_lhs(tri, a_c)
    tot = jnp.sum(a_c, axis=0, keepdims=True)
    cs_t = cs.T
    expand = lambda t: jnp.dot(t.astype(BF16), exp_ref[...], preferred_element_type=F32)
    xdt = xs * expand(dt_c)
    xdt_b = xdt.astype(BF16)
    xd_end = (xdt * expand(jnp.exp(tot - cs))).astype(BF16)
    e_cs = expand(jnp.exp(cs))
    e_tot = jnp.exp(_dot_sel_rhs(jnp.broadcast_to(tot, (16, 128)), exp_ref[...]))[0:1]
    lane = lax.broadcasted_iota(jnp.int32, (SSD_CHUNK, 128), 1)
    lo = lane < SSD_HEADDIM
    zero_b = jnp.zeros((SSD_CHUNK, 128), BF16)
    for g in range(SSD_GROUPS):
        bm_b = xbc_ref[0, :, 512 + g * 128:512 + (g + 1) * 128]
        cm_b = xbc_ref[0, :, 768 + g * 128:768 + (g + 1) * 128]
        bm = bm_b.astype(F32)
        cb = lax.dot_general(cm_b, bm_b, (((1,), (1,)), ((), ())), preferred_element_type=F32)
        st_g = st_ref[:, g * 256:(g + 1) * 256]
        y_off = jnp.dot(cm_b, st_g.astype(BF16), preferred_element_type=F32) * e_cs[:, g * 256:(g + 1) * 256]
        for j in range(2):
            c0 = g * 256 + j * 128
            x_pair = xdt_b[:, c0:c0 + 128]
            acc = y_off[:, j * 128:(j + 1) * 128]
            for hh in range(2):
                head = c0 // SSD_HEADDIM + hh
                col = cs[:, head:head + 1]
                row = cs_t[head:head + 1, :]
                decay = jnp.exp(jnp.where(mask, col - row, -1e30))
                gm = (cb * decay).astype(BF16)
                xh = jnp.where(lo if hh == 0 else jnp.logical_not(lo), x_pair, zero_b)
                acc = acc + jnp.dot(gm, xh, preferred_element_type=F32)
            y_ref[0, 0, :, c0:c0 + 128] = acc.astype(y_ref.dtype)
        st_ref[:, g * 256:(g + 1) * 256] = (
            st_g * e_tot[:, g * 256:(g + 1) * 256]
            + jnp.dot(bm.T.astype(BF16), xd_end[:, g * 256:(g + 1) * 256], preferred_element_type=F32))


def _ssd_scan(xbc_c, dt_raw, tri, par, expand, n_lat, n_ctx):
    bsz, t_all, _ = xbc_c.shape
    nc = n_lat + n_ctx

    def chunk(d, i):
        fwd = jnp.where(i < n_ctx, n_lat + i, i - n_ctx)
        bwd = jnp.where(i < n_ctx, nc - 1 - i, nc - 1 - i)
        return jnp.where(d == 0, fwd, bwd)

    return pl.pallas_call(
        _ssd_kernel,
        grid=(bsz, 2, nc),
        in_specs=[pl.BlockSpec((1, SSD_CHUNK, SSD_XBC), lambda b, d, i: (b, chunk(d, i), 0)),
                  pl.BlockSpec((1, SSD_CHUNK, 128), lambda b, d, i: (b, chunk(d, i), 0)),
                  pl.BlockSpec((1, SSD_CHUNK, SSD_CHUNK), lambda b, d, i: (d, 0, 0)),
                  pl.BlockSpec((1, 8, 128), lambda b, d, i: (d, 0, 0)),
                  pl.BlockSpec((128, SSD_INNER), lambda b, d, i: (0, 0))],
        out_specs=pl.BlockSpec((1, 1, SSD_CHUNK, SSD_INNER), lambda b, d, i: (b, d, chunk(d, i), 0)),
        out_shape=jax.ShapeDtypeStruct((bsz, 2, t_all, SSD_INNER), BF16),
        scratch_shapes=[pltpu.VMEM((SSD_STATE, SSD_INNER), F32)],
        compiler_params=_cparams(("parallel", "parallel", "arbitrary")),
        name="ssd_scan",
    )(xbc_c, dt_raw, tri, par, expand)


def _hy_lat_kernel(k1n, kgrp, v_ref, x1_ref, x2_ref, fa_ref, ga_ref, mf_ref, mi_ref, kr_ref, ki_ref, bias_ref,
                   o_ref, z_ref, a2_ref, b2_ref):
    order = pl.program_id(1)
    ph = pl.program_id(2)
    n2 = FFT_N2
    kp, h1 = fa_ref.shape
    ks = ga_ref.shape[1]
    seq = h1 * n2
    ngroups = k1n // kgrp
    dot = functools.partial(jnp.dot, preferred_element_type=F32)

    @pl.when(jnp.logical_and(order == 0, ph == 0))
    def _():
        for j in range(2):
            z_ref[j] = v_ref[0, :, j * 128:(j + 1) * 128].astype(F32)
            b2_ref[j, 2 * k1n * HY_B_PITCH:ks * HY_B_PITCH, :] = jnp.zeros(((ks - 2 * k1n) * HY_B_PITCH, 128), F32)

    @pl.when(ph == 0)
    def _():
        fa = fa_ref[...]

        def body(i, carry):
            u = jnp.concatenate([z_ref[j, pl.ds(i, h1, stride=n2), :] for j in range(2)], axis=1)
            r = dot(fa, u.astype(BF16))
            row = pl.multiple_of(i * HY_A_PITCH, 8)
            for j in range(2):
                a2_ref[j, pl.ds(row, kp), :] = r[:, j * 128:(j + 1) * 128]
            return carry

        lax.fori_loop(0, n2, body, 0, unroll=8)

    @pl.when(jnp.logical_and(ph >= 1, ph <= ngroups))
    def _():
        for t in range(kgrp):
            k1 = (ph - 1) * kgrp + t
            are = jnp.concatenate([a2_ref[j, pl.ds(2 * k1, n2, stride=HY_A_PITCH), :] for j in range(2)], axis=1)
            aim = jnp.concatenate([a2_ref[j, pl.ds(2 * k1 + 1, n2, stride=HY_A_PITCH), :] for j in range(2)], axis=1)
            x = dot(mf_ref[t], jnp.concatenate([are, aim], axis=0).astype(BF16))
            xr, xi = x[:n2], x[n2:]
            kr, ki = kr_ref[0, t], ki_ref[0, t]
            y = jnp.concatenate([xr * kr - xi * ki, xr * ki + xi * kr], axis=0).astype(BF16)
            bq = dot(mi_ref[t], y)
            row = pl.multiple_of(2 * k1 * HY_B_PITCH, 8)
            for j in range(2):
                b2_ref[j, pl.ds(row, n2), :] = bq[:n2, j * 128:(j + 1) * 128]
                b2_ref[j, pl.ds(row + HY_B_PITCH, n2), :] = bq[n2:, j * 128:(j + 1) * 128]

    @pl.when(ph == ngroups + 1)
    def _():
        ga = ga_ref[...]

        def body(i, carry):
            bs = jnp.concatenate([b2_ref[j, pl.ds(i, ks, stride=HY_B_PITCH), :] for j in range(2)], axis=1)
            y = dot(ga, bs.astype(BF16))
            for j in range(2):
                o_ref[0, j, pl.ds(i, h1, stride=n2), :] = y[:, j * 128:(j + 1) * 128]
            return carry

        lax.fori_loop(0, n2, body, 0, unroll=8)
        rb = min(256, seq)

        def gate(i, carry):
            r0 = pl.multiple_of(i * rb, 8)
            for j in range(2):
                conv = o_ref[0, j, pl.ds(r0, rb), :] + z_ref[j, pl.ds(r0, rb), :] * bias_ref[0, :, j * 128:(j + 1) * 128]
                x1 = x1_ref[0, pl.ds(r0, rb), j * 128:(j + 1) * 128].astype(F32)
                x2 = x2_ref[0, pl.ds(r0, rb), j * 128:(j + 1) * 128].astype(F32)
                z_ref[j, pl.ds(r0, rb), :] = x1 * conv
                o_ref[0, j, pl.ds(r0, rb), :] = x2 * conv
            return carry

        lax.fori_loop(0, seq // rb, gate, 0)


def _hyena_lat(v, x1, x2, fa, ga, mfwd, minv, kr, ki, bias, seq):
    bsz, t_all, w = v.shape
    n2 = FFT_N2
    kp, h1 = fa.shape
    ks = ga.shape[1]
    k1n = mfwd.shape[0]
    kgrp = 3 if k1n % 3 == 0 else 1
    ngroups = k1n // kgrp
    tok = pl.BlockSpec((1, seq, w), lambda b, o, p: (b, 0, 0))
    full = lambda shp: pl.BlockSpec(shp, lambda b, o, p: (0,) * len(shp))
    grp = lambda p: jnp.clip(p - 1, 0, ngroups - 1)
    mat = pl.BlockSpec((kgrp, 2 * n2, 2 * n2), lambda b, o, p: (grp(p), 0, 0))
    spec = pl.BlockSpec((1, kgrp, n2, w), lambda b, o, p: (o, grp(p), 0, 0))
    return pl.pallas_call(
        functools.partial(_hy_lat_kernel, k1n, kgrp),
        grid=(bsz, HY_ORDER, ngroups + 2),
        in_specs=[tok, tok, tok, full((kp, h1)), full((h1, ks)), mat, mat, spec, spec,
                  pl.BlockSpec((1, 1, w), lambda b, o, p: (o, 0, 0))],
        out_specs=pl.BlockSpec((1, 2, seq, 128), lambda b, o, p: (b, 0, 0, 0)),
        out_shape=jax.ShapeDtypeStruct((bsz, 2, t_all, 128), F32),
        scratch_shapes=[pltpu.VMEM((2, seq, 128), F32),
                        pltpu.VMEM((2, n2 * HY_A_PITCH, 128), F32),
                        pltpu.VMEM((2, ks * HY_B_PITCH, 128), F32)],
        compiler_params=_cparams(("parallel", "arbitrary", "arbitrary"), V7X_VMEM_LIMIT_MB),
        name="hyena_lat",
    )(v, x1, x2, fa, ga, mfwd, minv, kr, ki, bias)


def _hyena_lat_consts(seq):
    n = 2 * seq
    n2 = FFT_N2
    n1 = n // n2
    h1 = n1 // 2
    k1n = n1 // 2 + 1
    kp = -(-2 * k1n // 8) * 8
    ks = -(-2 * k1n // 16) * 16
    assert kp <= HY_A_PITCH and n2 <= HY_B_PITCH
    k1 = np.arange(k1n)
    m1 = np.arange(h1)
    ang = 2.0 * np.pi * np.outer(k1, m1) / n1
    fa = np.zeros((kp, h1))
    fa[0:2 * k1n:2] = np.cos(ang)
    fa[1:2 * k1n:2] = -np.sin(ang)
    wgt = np.where((k1 == 0) | (k1 == n1 // 2), 1.0, 2.0) / n
    ga = np.zeros((h1, ks))
    ga[:, 0:2 * k1n:2] = (np.cos(ang) * wgt[:, None]).T
    ga[:, 1:2 * k1n:2] = (-np.sin(ang) * wgt[:, None]).T
    k2 = np.arange(n2)
    m2 = np.arange(n2)
    kk = k1[:, None, None] + n1 * k2[None, :, None]
    th = 2.0 * np.pi * ((kk * m2[None, None, :]) % n) / n
    mc, ms = np.cos(th), np.sin(th)
    mfwd = np.concatenate([np.concatenate([mc, ms], axis=2), np.concatenate([-ms, mc], axis=2)], axis=1)
    mct, mst = np.transpose(mc, (0, 2, 1)), np.transpose(ms, (0, 2, 1))
    minv = np.concatenate([np.concatenate([mct, -mst], axis=2), np.concatenate([mst, mct], axis=2)], axis=1)
    kidx = (k1[:, None] + n1 * k2[None, :])
    return (jnp.asarray(fa, BF16), jnp.asarray(ga, BF16), jnp.asarray(mfwd, BF16), jnp.asarray(minv, BF16), kidx)


def _hy_ctx_kernel(v_ref, x1_ref, x2_ref, fc_ref, fs_ref, gc_ref, gs_ref, kr_ref, ki_ref, bias_ref, zlat_ref, z_ref):
    del zlat_ref
    d = functools.partial(jnp.dot, preferred_element_type=F32)

    def conv(u, o):
        ub = u.astype(BF16)
        cr, ci = d(fc_ref[...], ub), d(fs_ref[...], ub)
        kr, ki = kr_ref[o], ki_ref[o]
        pr = (cr * kr - ci * ki).astype(BF16)
        pi = (cr * ki + ci * kr).astype(BF16)
        return d(gc_ref[...], pr) + d(gs_ref[...], pi) + u * bias_ref[o]

    z = x1_ref[0].astype(F32) * conv(v_ref[0].astype(F32), 0)
    z = x2_ref[0].astype(F32) * conv(z, 1)
    for j in range(2):
        z_ref[0, j] = z[:, j * 128:(j + 1) * 128]


def _hyena_ctx(v, x1, x2, fc, fs, gc, gs, kr, ki, bias, z_all, seq, ctx_len):
    bsz, _, w = v.shape
    blk = seq // ctx_len
    kpad = fc.shape[0]
    tok = pl.BlockSpec((1, ctx_len, w), lambda b: (b, blk, 0))
    full = lambda shp: pl.BlockSpec(shp, lambda b: (0,) * len(shp))
    return pl.pallas_call(
        _hy_ctx_kernel, grid=(bsz,),
        in_specs=[tok, tok, tok, full((kpad, ctx_len)), full((kpad, ctx_len)), full((ctx_len, kpad)),
                  full((ctx_len, kpad)), full((2, kpad, w)), full((2, kpad, w)), full((2, 1, w)),
                  pl.BlockSpec(memory_space=pl.ANY)],
        out_specs=pl.BlockSpec((1, 2, ctx_len, 128), lambda b: (b, 0, blk, 0)),
        out_shape=jax.ShapeDtypeStruct(z_all.shape, F32),
        input_output_aliases={10: 0},
        compiler_params=_cparams(("parallel",)),
        name="hyena_ctx",
    )(v, x1, x2, fc, fs, gc, gs, kr, ki, bias, z_all)


def _hyena_ctx_consts(ctx_len):
    n = 2 * ctx_len
    nk = ctx_len + 1
    kpad = -(-nk // 128) * 128
    k = np.arange(nk)
    m = np.arange(ctx_len)
    ang = 2.0 * np.pi * np.outer(k, m) / n
    fc = np.zeros((kpad, ctx_len))
    fs = np.zeros((kpad, ctx_len))
    fc[:nk] = np.cos(ang)
    fs[:nk] = -np.sin(ang)
    wgt = np.where((k == 0) | (k == ctx_len), 1.0, 2.0) / n
    gc = np.zeros((ctx_len, kpad))
    gs = np.zeros((ctx_len, kpad))
    gc[:, :nk] = (np.cos(ang) * wgt[:, None]).T
    gs[:, :nk] = (-np.sin(ang) * wgt[:, None]).T
    return tuple(jnp.asarray(t, BF16) for t in (fc, fs, gc, gs)), kpad


def _hyena_filter_spectrum(length, w1, b1, w2, b2, w3, freq, decay):
    pos = jnp.arange(length, dtype=F32)
    t = pos / max(length - 1, 1)
    bands = jnp.linspace(1e-4, HY_BANDS - 1, HY_BANDS, dtype=F32)
    ang = (2.0 * math.pi / length) * pos[:, None] * bands
    feats = jnp.concatenate([t[:, None], jnp.cos(ang), -jnp.sin(ang)], axis=-1)
    freq = freq.astype(F32)
    mm = functools.partial(jnp.matmul, precision=HI)
    hid = jnp.sin(freq * (mm(feats, w1.astype(F32)) + b1.astype(F32)))
    hid = jnp.sin(freq * (mm(hid, w2.astype(F32)) + b2.astype(F32)))
    h = mm(hid, w3.astype(F32)) * jnp.exp(-t[:, None] * jnp.abs(decay.astype(F32)))
    h = h.reshape(length, 2, HY_ORDER, HY_WIDTH)
    h_fwd, h_bwd = h[:, 0], h[:, 1]
    l1 = (jnp.abs(h_fwd[0] + h_bwd[0]) + jnp.sum(jnp.abs(h_fwd[1:]), axis=0) + jnp.sum(jnp.abs(h_bwd[1:]), axis=0))
    spec = jnp.fft.rfft(h_fwd, n=2 * length, axis=0) + jnp.conj(jnp.fft.rfft(h_bwd, n=2 * length, axis=0))
    return spec / l1


def _mixout_kernel(nt_lat, xl_ref, xc_ref, m_ref, s5y_ref, yf_ref, yb_ref, xs_ref, z_ref, hy_ref,
                   wglu_ref, bglu_ref, vec512_ref, wout_ref, ln_ref, o_ref):
    m = m_ref[0, 0]
    d = functools.partial(jnp.dot, preferred_element_type=F32)
    y5 = _gelu_tanh(jnp.concatenate([s5y_ref[0, 0], s5y_ref[0, 1]], axis=1))
    y5 = y5 * _sigmoid(d(y5.astype(BF16), wglu_ref[...]) + bglu_ref[...])
    ys = yf_ref[0, 0].astype(F32) + yb_ref[0, 0].astype(F32) + vec512_ref[0:1, :] * xs_ref[0].astype(F32)
    gsd = ys * _silu(z_ref[0].astype(F32))
    gsd = gsd * lax.rsqrt(jnp.mean(gsd * gsd, axis=-1, keepdims=True) + LN_EPS) * vec512_ref[1:2, :]
    mix = (d(y5.astype(BF16), wout_ref[0:256, :]) + d(gsd.astype(BF16), wout_ref[256:768, :])
           + d(hy_ref[0, 0].astype(BF16), wout_ref[768:896, :]) + d(hy_ref[0, 1].astype(BF16), wout_ref[896:1024, :]))
    r = ALPHA * _rows_of(nt_lat, xl_ref, xc_ref) + m[2:3] * mix
    o_ref[0] = _standardise(r) * ln_ref[0:1, :] + ln_ref[1:2, :]


def _mixout(x_lat, x_ctx, ctx_blk, mods, s5y, yssd, xbc_c, z, hy, wglu, bglu, vec512, wout, ln, nt_lat, rows):
    bsz = x_lat.shape[0]
    nt = rows // ROW_TILE
    tok = lambda w: pl.BlockSpec((1, ROW_TILE, w), lambda b, i: (b, i, 0))
    halves = pl.BlockSpec((1, 2, ROW_TILE, 128), lambda b, i: (b, 0, i, 0))
    full = lambda shp: pl.BlockSpec(shp, lambda b, i: (0,) * len(shp))
    return pl.pallas_call(
        functools.partial(_mixout_kernel, nt_lat), grid=(bsz, nt),
        in_specs=_row_specs(nt_lat, ctx_blk, D_MODEL) + [
                  pl.BlockSpec((1, 1, 6, D_MODEL), lambda b, i: (b, jnp.where(i < nt_lat, 0, 1), 0, 0)),
                  halves,
                  pl.BlockSpec((1, 1, ROW_TILE, SSD_INNER), lambda b, i: (b, 0, i, 0)),
                  pl.BlockSpec((1, 1, ROW_TILE, SSD_INNER), lambda b, i: (b, 1, i, 0)),
                  tok(SSD_INNER), tok(SSD_INNER), halves,
                  full((256, 256)), full((1, 256)), full((2, 512)), full((D_MODEL, D_MODEL)), full((2, D_MODEL))],
        out_specs=tok(D_MODEL),
        out_shape=jax.ShapeDtypeStruct((bsz, rows, D_MODEL), F32),
        compiler_params=_cparams(("parallel", "parallel"), 40),
        name="mixout",
    )(x_lat, x_ctx, mods, s5y, yssd, yssd, xbc_c, z, hy, wglu, bglu, vec512, wout, ln)


def _ffn_kernel(x_ref, m_ref, win_ref, wout_ref, ln_ref, o_ref):
    m = m_ref[0, 0]
    x = x_ref[0]
    d = functools.partial(jnp.dot, preferred_element_type=F32)
    h = (_standardise(x) * (1.0 + m[4:5]) + m[3:4]).astype(BF16)
    gate = d(h, win_ref[:, 0:FFN_HIDDEN])
    up = d(h, win_ref[:, FFN_HIDDEN:2 * FFN_HIDDEN])
    act = (_silu(gate) * up).astype(BF16)
    r = ALPHA * x + m[5:6] * d(act, wout_ref[...])
    o_ref[0] = _standardise(r) * ln_ref[0:1, :] + ln_ref[1:2, :]


def _ffn(x_all, mods, win, wout, ln, nt_lat):
    bsz, t_all, _ = x_all.shape
    nt = t_all // ROW_TILE
    tok = pl.BlockSpec((1, ROW_TILE, D_MODEL), lambda b, i: (b, i, 0))
    const = lambda shp: pl.BlockSpec(shp, lambda b, i: (0, 0), pipeline_mode=pl.Buffered(1))
    return pl.pallas_call(
        _ffn_kernel, grid=(bsz, nt),
        in_specs=[tok,
                  pl.BlockSpec((1, 1, 6, D_MODEL), lambda b, i: (b, jnp.where(i < nt_lat, 0, 1), 0, 0)),
                  const((D_MODEL, 2 * FFN_HIDDEN)), const((FFN_HIDDEN, D_MODEL)),
                  pl.BlockSpec((2, D_MODEL), lambda b, i: (0, 0))],
        out_specs=tok,
        out_shape=jax.ShapeDtypeStruct((bsz, t_all, D_MODEL), F32),
        compiler_params=_cparams(("parallel", "parallel"), V7X_VMEM_LIMIT_MB),
        name="ffn",
    )(x_all, mods, win, wout, ln)


def _layer(x_lat, x_ctx, ctx_blk, mods, p, seq, ctx_len, want_ctx):
    nt_lat = seq // ROW_TILE
    t_all = seq + ctx_len
    s5u, z, xbc, hy, dt_raw = _inproj(x_lat, x_ctx, ctx_blk, mods, p["w_in"], nt_lat)

    s5y = _s5_scan(s5u, p["s5_ncat"], p["s5_tz"], p["s5_mcat"], p["s5_coef"], p["s5_dvec"],
                   seq // S5_CHUNK, ctx_len // S5_CHUNK)

    (xbc_c,) = _dwconv(xbc, p["ssd_conv_w"], p["ssd_conv_b"], SSD_CONV, True, nt_lat, 1, BF16)
    yssd = _ssd_scan(xbc_c, dt_raw, p["ssd_tri"], p["ssd_par"], p["ssd_expand"],
                     seq // SSD_CHUNK, ctx_len // SSD_CHUNK)

    v, x1, x2 = _dwconv(hy, p["hy_conv_w"], p["hy_conv_b"], HY_SHORT, False, nt_lat, 3, BF16)
    hyz = _hyena_lat(v, x1, x2, p["hy_fa"], p["hy_ga"], p["hy_mfwd"], p["hy_minv"],
                     p["hy_kr"], p["hy_ki"], p["hy_bias"], seq)
    if want_ctx:
        hyz = _hyena_ctx(v, x1, x2, *p["hy_ctx_mats"], p["hy_ctx_kr"], p["hy_ctx_ki"], p["hy_bias"], hyz, seq, ctx_len)

    rows = t_all if want_ctx else seq
    x1_all = _mixout(x_lat, x_ctx, ctx_blk, mods, s5y, yssd, xbc_c, z, hyz, p["s5_wglu"], p["s5_bglu"], p["ssd_vec"],
                     p["w_out"], p["ln1"], nt_lat, rows)
    return _ffn(x1_all, mods, p["ffn_w_in"], p["ffn_w_out"], p["ln2"], nt_lat)


def kernel(x, c, ctx, c_ctx, w_mod, b_mod, w_in, s5_lam_re, s5_lam_im, s5_log_step, s5_b_re, s5_b_im, s5_c_re, s5_c_im, s5_d, s5_w_glu, s5_b_glu, ssd_conv_w, ssd_conv_b, ssd_dt_bias, ssd_a_log, ssd_d, ssd_norm_w, hy_conv_w, hy_conv_b, hy_w1, hy_b1, hy_w2, hy_b2, hy_w3, hy_freq, hy_decay, hy_bias, w_out, ln1_g, ln1_b, ffn_w_in, ffn_w_out, ln2_g, ln2_b):
    bsz, seq, _ = x.shape
    ctx_len = ctx.shape[1]
    assert bsz == 8 and seq % ROW_TILE == 0 and ctx_len == ROW_TILE

    fa, ga, mfwd, minv, kidx = _hyena_lat_consts(seq)
    ctx_mats, kpad_ctx = _hyena_ctx_consts(ctx_len)
    tt = np.arange(SSD_CHUNK)
    tri = jnp.asarray(np.stack([tt[None, :] <= tt[:, None], tt[None, :] >= tt[:, None]]), BF16)
    expand = jnp.asarray(np.repeat(np.eye(128, SSD_HEADS), SSD_HEADDIM, axis=1)[:, :SSD_INNER], BF16)
    cvec = jnp.zeros((16, D_MODEL), F32).at[:bsz].set(c.astype(F32)).at[bsz].set(c_ctx.astype(F32))

    x_lat, x_ctx, ctx_blk = x.astype(F32), ctx.astype(F32), 0
    for l in range(DEPTH):
        want_ctx = l < DEPTH - 1
        mod16 = _modulation(cvec, w_mod[l].astype(F32), b_mod[l].astype(F32).reshape(1, -1))
        mods = jnp.stack([mod16[:bsz].reshape(bsz, 6, D_MODEL),
                          jnp.broadcast_to(mod16[bsz].reshape(1, 6, D_MODEL), (bsz, 6, D_MODEL))], axis=1)
        wl = w_in[l]
        w_p = jnp.concatenate([wl[:, 0:256], wl[:, 256:768], wl[:, 768:1792], wl[:, 1800:2568], wl[:, 1792:1800],
                               jnp.zeros((D_MODEL, 120), wl.dtype)], axis=1).astype(BF16)
        ncat, tz, mcat, coef = _s5_weights(s5_lam_re[l], s5_lam_im[l], s5_log_step[l], s5_b_re[l], s5_b_im[l],
                                           s5_c_re[l], s5_c_im[l])
        rep = lambda t: jnp.repeat(t.astype(F32), SSD_HEADDIM, axis=-1)
        par = jnp.zeros((2, 8, 128), F32)
        par = par.at[:, 0, :SSD_HEADS].set(ssd_dt_bias[l].astype(F32))
        par = par.at[:, 1, :SSD_HEADS].set(-jnp.exp(ssd_a_log[l].astype(F32)))
        spec = _hyena_filter_spectrum(seq, hy_w1[l], hy_b1[l], hy_w2[l], hy_b2[l], hy_w3[l], hy_freq[l], hy_decay[l])
        lo = spec[jnp.asarray(np.minimum(kidx, 2 * seq - kidx))]
        kr = jnp.transpose(jnp.real(lo), (2, 0, 1, 3))
        ki = jnp.transpose(jnp.where(jnp.asarray(kidx > seq)[..., None, None], -jnp.imag(lo), jnp.imag(lo)), (2, 0, 1, 3))
        p = dict(
            w_in=w_p, s5_ncat=ncat, s5_tz=tz, s5_mcat=mcat, s5_coef=coef,
            ssd_conv_w=ssd_conv_w[l].astype(F32), ssd_conv_b=ssd_conv_b[l].astype(F32),
            ssd_tri=tri, ssd_par=par, ssd_expand=expand,
            hy_conv_w=hy_conv_w[l].astype(F32), hy_conv_b=hy_conv_b[l].astype(F32),
            hy_fa=fa, hy_ga=ga, hy_mfwd=mfwd, hy_minv=minv, hy_kr=kr.astype(F32), hy_ki=ki.astype(F32),
            hy_bias=hy_bias[l].astype(F32)[:, None, :],
            s5_wglu=s5_w_glu[l].astype(BF16),
            s5_bglu=s5_b_glu[l].astype(F32).reshape(1, S5_WIDTH),
            s5_dvec=jnp.tile(s5_d[l].astype(F32).reshape(S5_GROUPS, 1, S5_GROUP_CH), (1, 1, S5_CHUNK)),
            ssd_vec=jnp.stack([rep(ssd_d[l]), ssd_norm_w[l].astype(F32)]),
            w_out=w_out[l].astype(BF16), ln1=jnp.stack([ln1_g[l], ln1_b[l]]).astype(F32),
            ffn_w_in=ffn_w_in[l].astype(BF16), ffn_w_out=ffn_w_out[l].astype(BF16),
            ln2=jnp.stack([ln2_g[l], ln2_b[l]]).astype(F32),
        )
        if want_ctx:
            cspec = _hyena_filter_spectrum(ctx_len, hy_w1[l], hy_b1[l], hy_w2[l], hy_b2[l], hy_w3[l],
                                           hy_freq[l], hy_decay[l])
            cspec = jnp.transpose(cspec, (1, 0, 2))
            padk = ((0, 0), (0, kpad_ctx - cspec.shape[1]), (0, 0))
            p.update(hy_ctx_mats=ctx_mats, hy_ctx_kr=jnp.pad(jnp.real(cspec), padk).astype(F32),
                     hy_ctx_ki=jnp.pad(jnp.imag(cspec), padk).astype(F32))
        x_lat = _layer(x_lat, x_ctx, ctx_blk, mods, p, seq, ctx_len, want_ctx)
        x_ctx, ctx_blk = x_lat, seq // ROW_TILE
    return x_lat.astype(x.dtype)
```

```python
import functools
import math

import numpy as np
import jax
import jax.numpy as jnp
from jax import lax
from jax.experimental import pallas as pl
from jax.experimental.pallas import tpu as pltpu

F32 = jnp.float32
BF16 = jnp.bfloat16
HI = lax.Precision.HIGHEST

D_MODEL = 1024
DEPTH = 2
S5_WIDTH = 256
S5_GROUP_CH = 16
S5_GROUPS = 16
S5_STATE = 64
S5_MAX_RE = -1e-4
S5_CHUNK = 16
SSD_INNER = 512
SSD_HEADDIM = 64
SSD_HEADS = 8
SSD_GROUPS = 2
SSD_STATE = 128
SSD_CONV = 5
SSD_CHUNK = 128
SSD_CHUNKS_PER_STEP = 2
SSD_XBC = 1024
HY_WIDTH = 256
HY_ORDER = 2
HY_SHORT = 3
HY_BANDS = 16
HY_IN = 768
MIX_IN_PAD = 2688
FFN_HIDDEN = 2816
ALPHA = (2 * DEPTH) ** 0.25
LN_EPS = 1e-6

ROW_TILE = 256
FFN_ROW_TILE = 512
FFT_N2 = 128
HY_A_PITCH = 72
HY_B_PITCH = 136
V7X_VMEM_LIMIT_MB = 56


def _cparams(sem, vmem_mb=None):
    kw = dict(dimension_semantics=sem)
    if vmem_mb is not None:
        kw["vmem_limit_bytes"] = vmem_mb * 2 ** 20
    return pltpu.CompilerParams(**kw)


def _standardise(x):
    mu = jnp.mean(x, axis=-1, keepdims=True)
    xc = x - mu
    var = jnp.mean(xc * xc, axis=-1, keepdims=True)
    return xc * lax.rsqrt(var + LN_EPS)


def _sigmoid(x):
    return 1.0 / (1.0 + jnp.exp(-x))


def _silu(x):
    return x * _sigmoid(x)


def _gelu_tanh(x):
    return 0.5 * x * (1.0 + jnp.tanh(0.7978845608028654 * (x + 0.044715 * (x * x * x))))


def _softplus(x):
    return jnp.maximum(x, 0.0) + jnp.log(1.0 + jnp.exp(-jnp.abs(x)))


def _bdot(a, b):
    return jnp.dot(a.astype(BF16), b.astype(BF16), preferred_element_type=F32)


def _split3(a):
    a1 = a.astype(BF16)
    r1 = a - a1.astype(F32)
    a2 = r1.astype(BF16)
    a3 = (r1 - a2.astype(F32)).astype(BF16)
    return a1, a2, a3


def _dot_sel_lhs(sel, a):
    a1, a2, a3 = _split3(a)
    d = functools.partial(jnp.dot, preferred_element_type=F32)
    return d(sel, a1) + d(sel, a2) + d(sel, a3)


def _dot_sel_rhs(a, sel):
    a1, a2, a3 = _split3(a)
    d = functools.partial(jnp.dot, preferred_element_type=F32)
    return d(a1, sel) + d(a2, sel) + d(a3, sel)


def _mod_kernel(c_ref, w_ref, b_ref, o_ref):
    ca = _silu(c_ref[...])
    c1, c2, c3 = _split3(ca)
    w1, w2, w3 = _split3(w_ref[...])
    d = functools.partial(jnp.dot, preferred_element_type=F32)
    acc = d(c1, w1) + d(c1, w2) + d(c2, w1) + d(c1, w3) + d(c2, w2) + d(c3, w1)
    o_ref[...] = acc + b_ref[...]


def _modulation(cvec, w, b):
    n = w.shape[1]
    tn = 1536
    return pl.pallas_call(
        _mod_kernel,
        grid=(n // tn,),
        in_specs=[pl.BlockSpec((16, D_MODEL), lambda j: (0, 0)),
                  pl.BlockSpec((D_MODEL, tn), lambda j: (0, j)),
                  pl.BlockSpec((1, tn), lambda j: (0, j))],
        out_specs=pl.BlockSpec((16, tn), lambda j: (0, j)),
        out_shape=jax.ShapeDtypeStruct((16, n), F32),
        compiler_params=_cparams(("arbitrary",), 40),
        name="modulation",
    )(cvec, w, b)


def _rows_of(nt_lat, xl_ref, xc_ref):
    return jnp.where(pl.program_id(1) < nt_lat, xl_ref[0], xc_ref[0])


def _row_specs(nt_lat, ctx_blk, width):
    return [pl.BlockSpec((1, ROW_TILE, width), lambda b, i: (b, jnp.minimum(i, nt_lat - 1), 0)),
            pl.BlockSpec((1, ROW_TILE, width), lambda b, i: (b, ctx_blk, 0))]


def _inproj_kernel(nt_lat, xl_ref, xc_ref, m_ref, w_ref, s5_ref, z_ref, xbc_ref, hy_ref, dt_ref):
    m = m_ref[0, 0]
    h = (_standardise(_rows_of(nt_lat, xl_ref, xc_ref)) * (1.0 + m[1:2]) + m[0:1]).astype(BF16)
    d = functools.partial(jnp.dot, preferred_element_type=F32)
    s5_ref[0] = d(h, w_ref[:, 0:256])
    z_ref[0] = d(h, w_ref[:, 256:768]).astype(z_ref.dtype)
    xbc_ref[0] = d(h, w_ref[:, 768:1792])
    hy_ref[0] = d(h, w_ref[:, 1792:2560])
    dt_ref[0] = d(h, w_ref[:, 2560:2688])


def _inproj(x_lat, x_ctx, ctx_blk, mods, w_p, nt_lat):
    bsz = x_lat.shape[0]
    nt = nt_lat + 1
    t_all = nt * ROW_TILE
    widths = (256, 512, 1024, 768, 128)
    return pl.pallas_call(
        functools.partial(_inproj_kernel, nt_lat),
        grid=(bsz, nt),
        in_specs=_row_specs(nt_lat, ctx_blk, D_MODEL) + [
                  pl.BlockSpec((1, 1, 6, D_MODEL), lambda b, i: (b, jnp.where(i < nt_lat, 0, 1), 0, 0)),
                  pl.BlockSpec((D_MODEL, MIX_IN_PAD), lambda b, i: (0, 0))],
        out_specs=[pl.BlockSpec((1, ROW_TILE, w), lambda b, i: (b, i, 0)) for w in widths],
        out_shape=[jax.ShapeDtypeStruct((bsz, t_all, w), dt) for w, dt in zip(widths, (F32, BF16, F32, F32, F32))],
        compiler_params=_cparams(("parallel", "parallel"), 40),
        name="inproj",
    )(x_lat, x_ctx, mods, w_p)


def _dwconv_kernel(taps, act, nt_lat, n_out, xm_ref, xp_ref, xn_ref, w_ref, b_ref, *rest):
    o_refs, ext_ref = rest[:n_out], rest[n_out]
    i = pl.program_id(1)
    tm = xm_ref.shape[1]
    ch = xm_ref.shape[2]
    has_prev = jnp.logical_and(i > 0, i < nt_lat)
    has_next = i < nt_lat - 1
    ext_ref[0:8, :] = jnp.where(has_prev, xp_ref[0], 0.0)
    ext_ref[8:8 + tm, :] = xm_ref[0]
    ext_ref[8 + tm:16 + tm, :] = jnp.where(has_next, xn_ref[0], 0.0)
    pad = taps // 2
    rb = 64
    wo = ch // n_out
    for c0 in range(0, ch, 128):
        wk = [w_ref[k:k + 1, c0:c0 + 128] for k in range(taps)]
        bias = b_ref[0:1, c0:c0 + 128]
        for r0 in range(0, tm, rb):
            acc = bias + wk[0] * ext_ref[8 - pad + r0:8 - pad + r0 + rb, c0:c0 + 128]
            for k in range(1, taps):
                acc = acc + wk[k] * ext_ref[8 - pad + k + r0:8 - pad + k + r0 + rb, c0:c0 + 128]
            if act:
                acc = _silu(acc)
            o_ref = o_refs[c0 // wo]
            o_ref[0, r0:r0 + rb, (c0 % wo):(c0 % wo) + 128] = acc.astype(o_ref.dtype)


def _dwconv(x, w, b, taps, act, nt_lat, n_out, out_dtype):
    bsz, t_all, ch = x.shape
    nt = t_all // ROW_TILE
    r8 = ROW_TILE // 8
    last8 = t_all // 8 - 1
    wo = ch // n_out
    return pl.pallas_call(
        functools.partial(_dwconv_kernel, taps, act, nt_lat, n_out),
        grid=(bsz, nt),
        in_specs=[pl.BlockSpec((1, ROW_TILE, ch), lambda b_, i: (b_, i, 0)),
                  pl.BlockSpec((1, 8, ch), lambda b_, i: (b_, jnp.maximum(i * r8 - 1, 0), 0)),
                  pl.BlockSpec((1, 8, ch), lambda b_, i: (b_, jnp.minimum((i + 1) * r8, last8), 0)),
                  pl.BlockSpec((taps, ch), lambda b_, i: (0, 0)),
                  pl.BlockSpec((1, ch), lambda b_, i: (0, 0))],
        out_specs=[pl.BlockSpec((1, ROW_TILE, wo), lambda b_, i: (b_, i, 0)) for _ in range(n_out)],
        out_shape=[jax.ShapeDtypeStruct((bsz, t_all, wo), out_dtype) for _ in range(n_out)],
        scratch_shapes=[pltpu.VMEM((ROW_TILE + 16, ch), F32)],
        compiler_params=_cparams(("parallel", "parallel")),
        name="dwconv%d" % taps,
    )(x, x, x, w, b.reshape(1, ch))


def _s5_kernel(n_lat, n_ctx, u0_ref, u1_ref, ncat_ref, tz_ref, mcat_ref, coef_ref, dvec_ref, y_ref,
               ut_ref, s_ref, hp_ref):
    q, gch, ng = S5_CHUNK, S5_GROUP_CH, S5_GROUPS
    nc = n_lat + n_ctx
    dot = functools.partial(jnp.dot, preferred_element_type=F32)
    u_refs = (u0_ref, u1_ref)
    gpl = 128 // gch
    for j in range(2):
        for s in range(q):
            rows = u_refs[j][0, pl.ds(s, nc, stride=q), :]
            for gg in range(gpl):
                ut_ref[gpl * j + gg, :, s * gch:(s + 1) * gch] = rows[:, gg * gch:(gg + 1) * gch]
    for g in range(ng):
        sg = dot(ut_ref[g].astype(BF16), ncat_ref[g])
        for k in range(4):
            s_ref[k, pl.ds(g, nc, stride=ng), :] = sg[:, k * 128:(k + 1) * 128]
    c1f, c2f, c1b, c2b = coef_ref[0], coef_ref[1], coef_ref[2], coef_ref[3]

    def step(cf, cb, carry):
        hf, hsf, hb, hsb = carry
        rf = pl.multiple_of(cf * ng, ng)
        rb = pl.multiple_of(cb * ng, ng)
        hp_ref[0, pl.ds(rf, ng), :] = hf
        hp_ref[1, pl.ds(rb, ng), :] = hb
        sf = s_ref[0, pl.ds(rf, ng), :]
        sb = s_ref[1, pl.ds(rb, ng), :]
        ssf = s_ref[2, pl.ds(rf, ng), :]
        ssb = s_ref[3, pl.ds(rb, ng), :]
        return (c1f * hf + c2f * hsf + sf, c1f * hsf - c2f * hf + ssf,
                c1b * hb + c2b * hsb + sb, c1b * hsb - c2b * hb + ssb)

    z = jnp.zeros((ng, 128), F32)
    carry = lax.fori_loop(0, n_ctx, lambda i, c: step(n_lat + i, n_lat + n_ctx - 1 - i, c), (z, z, z, z))
    lax.fori_loop(0, n_lat, lambda i, c: step(i, n_lat - 1 - i, c), carry)
    for g in range(ng):
        ug = ut_ref[g]
        hp = jnp.concatenate([hp_ref[0, pl.ds(g, nc, stride=ng), :], hp_ref[1, pl.ds(g, nc, stride=ng), :]], axis=1)
        ut_ref[g] = dot(ug.astype(BF16), tz_ref[g]) + dot(hp.astype(BF16), mcat_ref[g]) + ug * dvec_ref[g]
    for j in range(2):
        for s in range(q):
            rows = jnp.concatenate([ut_ref[gpl * j + gg, :, s * gch:(s + 1) * gch] for gg in range(gpl)], axis=1)
            y_ref[0, j, pl.ds(s, nc, stride=q), :] = rows


def _s5_scan(s5u, ncat, tz, mcat, coef, dvec, n_lat, n_ctx):
    bsz, t_all, _ = s5u.shape
    nc = n_lat + n_ctx
    ng = S5_GROUPS
    const = lambda shp: pl.BlockSpec(shp, lambda b: (0,) * len(shp), pipeline_mode=pl.Buffered(1))
    return pl.pallas_call(
        functools.partial(_s5_kernel, n_lat, n_ctx),
        grid=(bsz,),
        in_specs=[pl.BlockSpec((1, t_all, 128), lambda b: (b, 0, 0)),
                  pl.BlockSpec((1, t_all, 128), lambda b: (b, 0, 1)),
                  const((ng, 256, 512)), const((ng, 256, 256)), const((ng, 256, 256)),
                  const((4, ng, 128)), const((ng, 1, 256))],
        out_specs=pl.BlockSpec((1, 2, t_all, 128), lambda b: (b, 0, 0, 0)),
        out_shape=jax.ShapeDtypeStruct((bsz, 2, t_all, 128), F32),
        scratch_shapes=[pltpu.VMEM((ng, nc, 256), F32), pltpu.VMEM((4, nc * ng, 128), F32),
                        pltpu.VMEM((2, nc * ng, 128), F32)],
        compiler_params=_cparams(("parallel",), V7X_VMEM_LIMIT_MB),
        name="s5_scan",
    )(s5u, s5u, ncat, tz, mcat, coef, dvec)


def _s5_weights(lam_re, lam_im, log_step, b_re, b_im, c_re, c_im):
    q, ng, gch, ns = S5_CHUNK, S5_GROUPS, S5_GROUP_CH, S5_STATE
    a_re = jnp.minimum(lam_re.astype(F32), S5_MAX_RE)
    a_im = lam_im.astype(F32)
    step = jnp.exp(log_step.astype(F32))[..., None]
    taus = jnp.arange(q + 1, dtype=F32)[:, None, None, None]
    mag = jnp.exp(taus * (a_re * step))
    pr, pi = mag * jnp.cos(taus * (a_im * step)), mag * jnp.sin(taus * (a_im * step))
    nr, ni, den = pr[1] - 1.0, pi[1], a_re * a_re + a_im * a_im
    fr, fi = ((nr * a_re + ni * a_im) / den)[..., None], ((ni * a_re - nr * a_im) / den)[..., None]
    b_r, b_i = b_re.astype(F32), b_im.astype(F32)
    bb_r, bb_i = fr * b_r - fi * b_i, fr * b_i + fi * b_r
    c_r = jnp.swapaxes(c_re.astype(F32), -1, -2)
    c_i = jnp.swapaxes(c_im.astype(F32), -1, -2)
    cp_r, cp_i = c_r * pr[..., None] - c_i * pi[..., None], c_r * pi[..., None] + c_i * pr[..., None]
    pb_r, pb_i = pr[..., None] * bb_r - pi[..., None] * bb_i, pr[..., None] * bb_i + pi[..., None] * bb_r
    kern = (jnp.einsum('tdgph,dgpk->tdghk', cp_r[:q], bb_r, precision=HI)
            - jnp.einsum('tdgph,dgpk->tdghk', cp_i[:q], bb_i, precision=HI))
    tt = np.arange(q)
    lag = tt[None, :] - tt[:, None]
    sel_f = jnp.asarray((lag[..., None] == tt).astype(np.float32))
    sel_b = jnp.asarray((-lag[..., None] == tt).astype(np.float32))
    tz = (jnp.einsum('stu,ughk->gskth', sel_f, kern[:, 0], precision=HI)
          + jnp.einsum('stu,ughk->gskth', sel_b, kern[:, 1], precision=HI)).reshape(ng, q * gch, q * gch)
    rows_sk = lambda t: jnp.transpose(t, (1, 0, 3, 2)).reshape(ng, q * gch, ns)
    nf_r, nf_i = rows_sk(pb_r[q - 1::-1, 0]), rows_sk(pb_i[q - 1::-1, 0])
    nb_r, nb_i = rows_sk(pb_r[:q, 1]), rows_sk(pb_i[:q, 1])
    ncat = jnp.concatenate([nf_r, nf_i, nb_r, nb_i, nf_i, nf_r, nb_i, nb_r], axis=-1)
    cols_th = lambda t: jnp.transpose(t, (1, 2, 0, 3)).reshape(ng, ns, q * gch)
    mcat = jnp.concatenate([cols_th(cp_r[1:, 0]), -cols_th(cp_i[1:, 0]),
                            cols_th(cp_r[q:0:-1, 1]), -cols_th(cp_i[q:0:-1, 1])], axis=1)
    lr, li = pr[q], pi[q]
    coef = jnp.stack([jnp.concatenate([lr[0], lr[0]], axis=-1), jnp.concatenate([-li[0], li[0]], axis=-1),
                      jnp.concatenate([lr[1], lr[1]], axis=-1), jnp.concatenate([-li[1], li[1]], axis=-1)])
    return ncat.astype(BF16), tz.astype(BF16), mcat.astype(BF16), coef.astype(F32)


def _ssd_kernel(cps, xbc_ref, dt_ref, tri_ref, par_ref, exp_ref, y_ref, st_ref):
    direction = pl.program_id(1)

    @pl.when(pl.program_id(2) == 0)
    def _():
        st_ref[...] = jnp.zeros_like(st_ref)

    tri = tri_ref[0]
    mask = tri > 0
    lane = lax.broadcasted_iota(jnp.int32, (SSD_CHUNK, 128), 1)
    lo = lane < SSD_HEADDIM
    zero_b = jnp.zeros((SSD_CHUNK, 128), BF16)
    expand = lambda t: jnp.dot(t.astype(BF16), exp_ref[...], preferred_element_type=F32)
    state = [st_ref[:, g * 256:(g + 1) * 256] for g in range(SSD_GROUPS)]
    for c in range(cps):
        sub = jnp.where(direction == 0, c, cps - 1 - c)
        rows = pl.ds(pl.multiple_of(sub * SSD_CHUNK, SSD_CHUNK), SSD_CHUNK)
        xs = xbc_ref[0, rows, 0:512].astype(F32)
        dt_c = _softplus(dt_ref[0, rows, :] + par_ref[0, 0:1, :])
        a_c = par_ref[0, 1:2, :] * dt_c
        cs = _dot_sel_lhs(tri, a_c)
        tot = jnp.sum(a_c, axis=0, keepdims=True)
        cs_t = cs.T
        xdt = xs * expand(dt_c)
        xdt_b = xdt.astype(BF16)
        xd_end = (xdt * expand(jnp.exp(tot - cs))).astype(BF16)
        e_cs = expand(jnp.exp(cs))
        e_tot = jnp.exp(_dot_sel_rhs(jnp.broadcast_to(tot, (16, 128)), exp_ref[...]))[0:1]
        for g in range(SSD_GROUPS):
            bm_b = xbc_ref[0, rows, 512 + g * 128:512 + (g + 1) * 128]
            cm_b = xbc_ref[0, rows, 768 + g * 128:768 + (g + 1) * 128]
            cb = lax.dot_general(cm_b, bm_b, (((1,), (1,)), ((), ())), preferred_element_type=F32)
            y_off = jnp.dot(cm_b, state[g].astype(BF16), preferred_element_type=F32) * e_cs[:, g * 256:(g + 1) * 256]
            for j in range(2):
                c0 = g * 256 + j * 128
                x_pair = xdt_b[:, c0:c0 + 128]
                acc = y_off[:, j * 128:(j + 1) * 128]
                for hh in range(2):
                    head = c0 // SSD_HEADDIM + hh
                    decay = jnp.exp(jnp.where(mask, cs[:, head:head + 1] - cs_t[head:head + 1, :], -1e30))
                    gm = (cb * decay).astype(BF16)
                    xh = jnp.where(lo if hh == 0 else jnp.logical_not(lo), x_pair, zero_b)
                    acc = acc + jnp.dot(gm, xh, preferred_element_type=F32)
                y_ref[0, 0, rows, c0:c0 + 128] = acc.astype(y_ref.dtype)
            state[g] = (state[g] * e_tot[:, g * 256:(g + 1) * 256]
                        + jnp.dot(bm_b.astype(F32).T.astype(BF16), xd_end[:, g * 256:(g + 1) * 256],
                                  preferred_element_type=F32))
    for g in range(SSD_GROUPS):
        st_ref[:, g * 256:(g + 1) * 256] = state[g]


def _ssd_scan(xbc_c, dt_raw, tri, par, expand, n_lat, n_ctx):
    bsz, t_all, _ = xbc_c.shape
    cps = SSD_CHUNKS_PER_STEP
    assert n_lat % cps == 0 and n_ctx % cps == 0
    n_lat, n_ctx = n_lat // cps, n_ctx // cps
    nc = n_lat + n_ctx
    rows = cps * SSD_CHUNK

    def blk(d, i):
        fwd = jnp.where(i < n_ctx, n_lat + i, i - n_ctx)
        return jnp.where(d == 0, fwd, nc - 1 - i)

    return pl.pallas_call(
        functools.partial(_ssd_kernel, cps),
        grid=(bsz, 2, nc),
        in_specs=[pl.BlockSpec((1, rows, SSD_XBC), lambda b, d, i: (b, blk(d, i), 0)),
                  pl.BlockSpec((1, rows, 128), lambda b, d, i: (b, blk(d, i), 0)),
                  pl.BlockSpec((1, SSD_CHUNK, SSD_CHUNK), lambda b, d, i: (d, 0, 0)),
                  pl.BlockSpec((1, 8, 128), lambda b, d, i: (d, 0, 0)),
                  pl.BlockSpec((128, SSD_INNER), lambda b, d, i: (0, 0))],
        out_specs=pl.BlockSpec((1, 1, rows, SSD_INNER), lambda b, d, i: (b, d, blk(d, i), 0)),
        out_shape=jax.ShapeDtypeStruct((bsz, 2, t_all, SSD_INNER), BF16),
        scratch_shapes=[pltpu.VMEM((SSD_STATE, SSD_INNER), F32)],
        compiler_params=_cparams(("parallel", "parallel", "arbitrary")),
        name="ssd_scan",
    )(xbc_c, dt_raw, tri, par, expand)


def _hy_lat_kernel(k1n, kgrp, v_ref, x1_ref, x2_ref, fa_ref, ga_ref, mf_ref, mi_ref, kr_ref, ki_ref, bias_ref,
                   o_ref, z_ref, a2_ref, b2_ref):
    order = pl.program_id(1)
    ph = pl.program_id(2)
    n2 = FFT_N2
    kp, h1 = fa_ref.shape
    ks = ga_ref.shape[1]
    seq = h1 * n2
    ngroups = k1n // kgrp
    dot = functools.partial(jnp.dot, preferred_element_type=F32)

    @pl.when(jnp.logical_and(order == 0, ph == 0))
    def _():
        for j in range(2):
            z_ref[j] = v_ref[0, :, j * 128:(j + 1) * 128].astype(F32)
            b2_ref[j, 2 * k1n * HY_B_PITCH:ks * HY_B_PITCH, :] = jnp.zeros(((ks - 2 * k1n) * HY_B_PITCH, 128), F32)

    @pl.when(ph == 0)
    def _():
        fa = fa_ref[...]

        def body(i, carry):
            u = jnp.concatenate([z_ref[j, pl.ds(i, h1, stride=n2), :] for j in range(2)], axis=1)
            r = dot(fa, u.astype(BF16))
            row = pl.multiple_of(i * HY_A_PITCH, 8)
            for j in range(2):
                a2_ref[j, pl.ds(row, kp), :] = r[:, j * 128:(j + 1) * 128]
            return carry

        lax.fori_loop(0, n2, body, 0, unroll=8)

    @pl.when(jnp.logical_and(ph >= 1, ph <= ngroups))
    def _():
        for t in range(kgrp):
            k1 = (ph - 1) * kgrp + t
            are = jnp.concatenate([a2_ref[j, pl.ds(2 * k1, n2, stride=HY_A_PITCH), :] for j in range(2)], axis=1)
            aim = jnp.concatenate([a2_ref[j, pl.ds(2 * k1 + 1, n2, stride=HY_A_PITCH), :] for j in range(2)], axis=1)
            x = dot(mf_ref[t], jnp.concatenate([are, aim], axis=0).astype(BF16))
            xr, xi = x[:n2], x[n2:]
            kr, ki = kr_ref[t], ki_ref[t]
            y = jnp.concatenate([xr * kr - xi * ki, xr * ki + xi * kr], axis=0).astype(BF16)
            bq = dot(mi_ref[t], y)
            row = pl.multiple_of(2 * k1 * HY_B_PITCH, 8)
            for j in range(2):
                b2_ref[j, pl.ds(row, n2), :] = bq[:n2, j * 128:(j + 1) * 128]
                b2_ref[j, pl.ds(row + HY_B_PITCH, n2), :] = bq[n2:, j * 128:(j + 1) * 128]

    @pl.when(ph == ngroups + 1)
    def _():
        ga = ga_ref[...]

        def body(i, carry):
            bs = jnp.concatenate([b2_ref[j, pl.ds(i, ks, stride=HY_B_PITCH), :] for j in range(2)], axis=1)
            y = dot(ga, bs.astype(BF16))
            for j in range(2):
                o_ref[0, j, pl.ds(i, h1, stride=n2), :] = y[:, j * 128:(j + 1) * 128]
            return carry

        lax.fori_loop(0, n2, body, 0, unroll=8)
        rb = min(256, seq)

        def gate(i, carry):
            r0 = pl.multiple_of(i * rb, 8)
            for j in range(2):
                conv = o_ref[0, j, pl.ds(r0, rb), :] + z_ref[j, pl.ds(r0, rb), :] * bias_ref[0, :, j * 128:(j + 1) * 128]
                x1 = x1_ref[0, pl.ds(r0, rb), j * 128:(j + 1) * 128].astype(F32)
                x2 = x2_ref[0, pl.ds(r0, rb), j * 128:(j + 1) * 128].astype(F32)
                z_ref[j, pl.ds(r0, rb), :] = x1 * conv
                o_ref[0, j, pl.ds(r0, rb), :] = x2 * conv
            return carry

        lax.fori_loop(0, seq // rb, gate, 0)


def _hy_spec_kernel(k1n, ts_ref, td_ref, fa_ref, mf_ref, kr_ref, ki_ref, a2_ref):
    n2 = FFT_N2
    kp, h1 = fa_ref.shape
    dot = functools.partial(jnp.dot, preferred_element_type=F32)
    fa = fa_ref[...]

    def body(i, carry):
        u = jnp.concatenate([ts_ref[pl.ds(i, h1, stride=n2), :], td_ref[pl.ds(i, h1, stride=n2), :]], axis=1)
        r = dot(fa, u.astype(BF16))
        row = pl.multiple_of(i * HY_A_PITCH, 8)
        for j in range(2):
            a2_ref[j, pl.ds(row, kp), :] = r[:, j * 128:(j + 1) * 128]
        return carry

    lax.fori_loop(0, n2, body, 0, unroll=8)

    def per_k1(k1, carry):
        are = jnp.concatenate([a2_ref[j, pl.ds(2 * k1, n2, stride=HY_A_PITCH), :] for j in range(2)], axis=1)
        aim = jnp.concatenate([a2_ref[j, pl.ds(2 * k1 + 1, n2, stride=HY_A_PITCH), :] for j in range(2)], axis=1)
        x = dot(mf_ref[k1], jnp.concatenate([are, aim], axis=0).astype(BF16))
        kr_ref[k1] = x[:n2, 0:128]
        ki_ref[k1] = x[n2:, 128:256]
        return carry

    lax.fori_loop(0, k1n, per_k1, 0)


def _hyena_spectrum(tsum, tdiff, fa, mfwd):
    seq, lanes = tsum.shape
    n2 = FFT_N2
    kp, h1 = fa.shape
    k1n = mfwd.shape[0]
    taps = pl.BlockSpec((seq, 128), lambda j: (0, j))
    out = pl.BlockSpec((k1n, n2, 128), lambda j: (0, 0, j))
    return pl.pallas_call(
        functools.partial(_hy_spec_kernel, k1n),
        grid=(lanes // 128,),
        in_specs=[taps, taps, pl.BlockSpec((kp, h1), lambda j: (0, 0)),
                  pl.BlockSpec((k1n, 2 * n2, 2 * n2), lambda j: (0, 0, 0))],
        out_specs=[out, out],
        out_shape=[jax.ShapeDtypeStruct((k1n, n2, lanes), F32)] * 2,
        scratch_shapes=[pltpu.VMEM((2, n2 * HY_A_PITCH, 128), F32)],
        compiler_params=_cparams(("parallel",), 40),
        name="hyena_spectrum",
    )(tsum, tdiff, fa, mfwd)


def _hyena_lat(v, x1, x2, fa, ga, mfwd, minv, kr, ki, bias, seq):
    bsz, t_all, w = v.shape
    n2 = FFT_N2
    kp, h1 = fa.shape
    ks = ga.shape[1]
    k1n = mfwd.shape[0]
    kgrp = 3 if k1n % 3 == 0 else 1
    ngroups = k1n // kgrp
    tok = pl.BlockSpec((1, seq, w), lambda b, o, p: (b, 0, 0))
    full = lambda shp: pl.BlockSpec(shp, lambda b, o, p: (0,) * len(shp))
    grp = lambda p: jnp.clip(p - 1, 0, ngroups - 1)
    mat = pl.BlockSpec((kgrp, 2 * n2, 2 * n2), lambda b, o, p: (grp(p), 0, 0))
    spec = pl.BlockSpec((kgrp, n2, w), lambda b, o, p: (grp(p), 0, o))
    return pl.pallas_call(
        functools.partial(_hy_lat_kernel, k1n, kgrp),
        grid=(bsz, HY_ORDER, ngroups + 2),
        in_specs=[tok, tok, tok, full((kp, h1)), full((h1, ks)), mat, mat, spec, spec,
                  pl.BlockSpec((1, 1, w), lambda b, o, p: (o, 0, 0))],
        out_specs=pl.BlockSpec((1, 2, seq, 128), lambda b, o, p: (b, 0, 0, 0)),
        out_shape=jax.ShapeDtypeStruct((bsz, 2, t_all, 128), F32),
        scratch_shapes=[pltpu.VMEM((2, seq, 128), F32),
                        pltpu.VMEM((2, n2 * HY_A_PITCH, 128), F32),
                        pltpu.VMEM((2, ks * HY_B_PITCH, 128), F32)],
        compiler_params=_cparams(("parallel", "arbitrary", "arbitrary"), V7X_VMEM_LIMIT_MB),
        name="hyena_lat",
    )(v, x1, x2, fa, ga, mfwd, minv, kr, ki, bias)


def _hyena_lat_consts(seq):
    n = 2 * seq
    n2 = FFT_N2
    n1 = n // n2
    h1 = n1 // 2
    k1n = n1 // 2 + 1
    kp = -(-2 * k1n // 8) * 8
    ks = -(-2 * k1n // 16) * 16
    assert kp <= HY_A_PITCH and n2 <= HY_B_PITCH
    k1 = np.arange(k1n)
    m1 = np.arange(h1)
    ang = 2.0 * np.pi * np.outer(k1, m1) / n1
    fa = np.zeros((kp, h1))
    fa[0:2 * k1n:2] = np.cos(ang)
    fa[1:2 * k1n:2] = -np.sin(ang)
    wgt = np.where((k1 == 0) | (k1 == n1 // 2), 1.0, 2.0) / n
    ga = np.zeros((h1, ks))
    ga[:, 0:2 * k1n:2] = (np.cos(ang) * wgt[:, None]).T
    ga[:, 1:2 * k1n:2] = (-np.sin(ang) * wgt[:, None]).T
    k2 = np.arange(n2)
    m2 = np.arange(n2)
    kk = k1[:, None, None] + n1 * k2[None, :, None]
    th = 2.0 * np.pi * ((kk * m2[None, None, :]) % n) / n
    mc, ms = np.cos(th), np.sin(th)
    mfwd = np.concatenate([np.concatenate([mc, ms], axis=2), np.concatenate([-ms, mc], axis=2)], axis=1)
    mct, mst = np.transpose(mc, (0, 2, 1)), np.transpose(ms, (0, 2, 1))
    minv = np.concatenate([np.concatenate([mct, -mst], axis=2), np.concatenate([mst, mct], axis=2)], axis=1)
    return tuple(jnp.asarray(t, BF16) for t in (fa, ga, mfwd, minv))


def _hy_ctx_kernel(v_ref, x1_ref, x2_ref, fc_ref, fs_ref, gc_ref, gs_ref, ts_ref, td_ref, bias_ref, zlat_ref, z_ref):
    del zlat_ref
    d = functools.partial(jnp.dot, preferred_element_type=F32)
    w = v_ref.shape[2]
    kr_all = d(fc_ref[...], ts_ref[...].astype(BF16))
    ki_all = d(fs_ref[...], td_ref[...].astype(BF16))

    def conv(u, o):
        ub = u.astype(BF16)
        cr, ci = d(fc_ref[...], ub), d(fs_ref[...], ub)
        kr, ki = kr_all[:, o * w:(o + 1) * w], ki_all[:, o * w:(o + 1) * w]
        pr = (cr * kr - ci * ki).astype(BF16)
        pi = (cr * ki + ci * kr).astype(BF16)
        return d(gc_ref[...], pr) + d(gs_ref[...], pi) + u * bias_ref[o]

    z = x1_ref[0].astype(F32) * conv(v_ref[0].astype(F32), 0)
    z = x2_ref[0].astype(F32) * conv(z, 1)
    for j in range(2):
        z_ref[0, j] = z[:, j * 128:(j + 1) * 128]


def _hyena_ctx(v, x1, x2, fc, fs, gc, gs, tsum, tdiff, bias, z_all, seq, ctx_len):
    bsz, _, w = v.shape
    blk = seq // ctx_len
    kpad = fc.shape[0]
    tok = pl.BlockSpec((1, ctx_len, w), lambda b: (b, blk, 0))
    full = lambda shp: pl.BlockSpec(shp, lambda b: (0,) * len(shp))
    return pl.pallas_call(
        _hy_ctx_kernel, grid=(bsz,),
        in_specs=[tok, tok, tok, full((kpad, ctx_len)), full((kpad, ctx_len)), full((ctx_len, kpad)),
                  full((ctx_len, kpad)), full((ctx_len, HY_ORDER * w)), full((ctx_len, HY_ORDER * w)), full((2, 1, w)),
                  pl.BlockSpec(memory_space=pl.ANY)],
        out_specs=pl.BlockSpec((1, 2, ctx_len, 128), lambda b: (b, 0, blk, 0)),
        out_shape=jax.ShapeDtypeStruct(z_all.shape, F32),
        input_output_aliases={10: 0},
        compiler_params=_cparams(("parallel",)),
        name="hyena_ctx",
    )(v, x1, x2, fc, fs, gc, gs, tsum, tdiff, bias, z_all)


def _hyena_ctx_consts(ctx_len):
    n = 2 * ctx_len
    nk = ctx_len + 1
    kpad = -(-nk // 128) * 128
    k = np.arange(nk)
    m = np.arange(ctx_len)
    ang = 2.0 * np.pi * np.outer(k, m) / n
    fc = np.zeros((kpad, ctx_len))
    fs = np.zeros((kpad, ctx_len))
    fc[:nk] = np.cos(ang)
    fs[:nk] = -np.sin(ang)
    wgt = np.where((k == 0) | (k == ctx_len), 1.0, 2.0) / n
    gc = np.zeros((ctx_len, kpad))
    gs = np.zeros((ctx_len, kpad))
    gc[:, :nk] = (np.cos(ang) * wgt[:, None]).T
    gs[:, :nk] = (-np.sin(ang) * wgt[:, None]).T
    return tuple(jnp.asarray(t, BF16) for t in (fc, fs, gc, gs))


def _hyena_filter_taps(length, w1, b1, w2, b2, w3, freq, decay):
    pos = jnp.arange(length, dtype=F32)
    t = pos / max(length - 1, 1)
    bands = jnp.linspace(1e-4, HY_BANDS - 1, HY_BANDS, dtype=F32)
    ang = (2.0 * math.pi / length) * pos[:, None] * bands
    feats = jnp.concatenate([t[:, None], jnp.cos(ang), -jnp.sin(ang)], axis=-1)
    freq = freq.astype(F32)
    mm = functools.partial(jnp.matmul, precision=HI)
    hid = jnp.sin(freq * (mm(feats, w1.astype(F32)) + b1.astype(F32)))
    hid = jnp.sin(freq * (mm(hid, w2.astype(F32)) + b2.astype(F32)))
    h = mm(hid, w3.astype(F32)) * jnp.exp(-t[:, None] * jnp.abs(decay.astype(F32)))
    h = h.reshape(length, 2, HY_ORDER, HY_WIDTH)
    h_fwd, h_bwd = h[:, 0], h[:, 1]
    l1 = (jnp.abs(h_fwd[0] + h_bwd[0]) + jnp.sum(jnp.abs(h_fwd[1:]), axis=0) + jnp.sum(jnp.abs(h_bwd[1:]), axis=0))
    flat = lambda t: (t / l1).reshape(length, HY_ORDER * HY_WIDTH)
    return flat(h_fwd + h_bwd), flat(h_fwd - h_bwd)


def _mixout_kernel(nt_lat, xl_ref, xc_ref, m_ref, s5y_ref, yf_ref, yb_ref, xs_ref, z_ref, hy_ref,
                   wglu_ref, bglu_ref, vec512_ref, wout_ref, ln_ref, o_ref):
    m = m_ref[0, 0]
    d = functools.partial(jnp.dot, preferred_element_type=F32)
    y5 = _gelu_tanh(jnp.concatenate([s5y_ref[0, 0], s5y_ref[0, 1]], axis=1))
    y5 = y5 * _sigmoid(d(y5.astype(BF16), wglu_ref[...]) + bglu_ref[...])
    ys = yf_ref[0, 0].astype(F32) + yb_ref[0, 0].astype(F32) + vec512_ref[0:1, :] * xs_ref[0].astype(F32)
    gsd = ys * _silu(z_ref[0].astype(F32))
    gsd = gsd * lax.rsqrt(jnp.mean(gsd * gsd, axis=-1, keepdims=True) + LN_EPS) * vec512_ref[1:2, :]
    mix = (d(y5.astype(BF16), wout_ref[0:256, :]) + d(gsd.astype(BF16), wout_ref[256:768, :])
           + d(hy_ref[0, 0].astype(BF16), wout_ref[768:896, :]) + d(hy_ref[0, 1].astype(BF16), wout_ref[896:1024, :]))
    r = ALPHA * _rows_of(nt_lat, xl_ref, xc_ref) + m[2:3] * mix
    o_ref[0] = _standardise(r) * ln_ref[0:1, :] + ln_ref[1:2, :]


def _mixout(x_lat, x_ctx, ctx_blk, mods, s5y, yssd, xbc_c, z, hy, wglu, bglu, vec512, wout, ln, nt_lat, rows):
    bsz = x_lat.shape[0]
    nt = rows // ROW_TILE
    tok = lambda w: pl.BlockSpec((1, ROW_TILE, w), lambda b, i: (b, i, 0))
    halves = pl.BlockSpec((1, 2, ROW_TILE, 128), lambda b, i: (b, 0, i, 0))
    full = lambda shp: pl.BlockSpec(shp, lambda b, i: (0,) * len(shp))
    return pl.pallas_call(
        functools.partial(_mixout_kernel, nt_lat), grid=(bsz, nt),
        in_specs=_row_specs(nt_lat, ctx_blk, D_MODEL) + [
                  pl.BlockSpec((1, 1, 6, D_MODEL), lambda b, i: (b, jnp.where(i < nt_lat, 0, 1), 0, 0)),
                  halves,
                  pl.BlockSpec((1, 1, ROW_TILE, SSD_INNER), lambda b, i: (b, 0, i, 0)),
                  pl.BlockSpec((1, 1, ROW_TILE, SSD_INNER), lambda b, i: (b, 1, i, 0)),
                  tok(SSD_INNER), tok(SSD_INNER), halves,
                  full((256, 256)), full((1, 256)), full((2, 512)), full((D_MODEL, D_MODEL)), full((2, D_MODEL))],
        out_specs=tok(D_MODEL),
        out_shape=jax.ShapeDtypeStruct((bsz, rows, D_MODEL), F32),
        compiler_params=_cparams(("parallel", "parallel"), 40),
        name="mixout",
    )(x_lat, x_ctx, mods, s5y, yssd, yssd, xbc_c, z, hy, wglu, bglu, vec512, wout, ln)


def _ffn_kernel(x_ref, m_ref, win_ref, wout_ref, ln_ref, o_ref):
    m = m_ref[0, 0]
    x = x_ref[0]
    d = functools.partial(jnp.dot, preferred_element_type=F32)
    h = (_standardise(x) * (1.0 + m[4:5]) + m[3:4]).astype(BF16)
    gate = d(h, win_ref[:, 0:FFN_HIDDEN])
    up = d(h, win_ref[:, FFN_HIDDEN:2 * FFN_HIDDEN])
    act = (_silu(gate) * up).astype(BF16)
    r = ALPHA * x + m[5:6] * d(act, wout_ref[...])
    o_ref[0] = _standardise(r) * ln_ref[0:1, :] + ln_ref[1:2, :]


def _ffn(x_all, mods, win, wout, ln, nt_lat):
    bsz, t_all, _ = x_all.shape
    tile = FFN_ROW_TILE if t_all % FFN_ROW_TILE == 0 else ROW_TILE
    nt = t_all // tile
    lat_tiles = nt_lat * ROW_TILE // tile
    tok = pl.BlockSpec((1, tile, D_MODEL), lambda b, i: (b, i, 0))
    const = lambda shp: pl.BlockSpec(shp, lambda b, i: (0, 0), pipeline_mode=pl.Buffered(1))
    return pl.pallas_call(
        _ffn_kernel, grid=(bsz, nt),
        in_specs=[tok,
                  pl.BlockSpec((1, 1, 6, D_MODEL), lambda b, i: (b, jnp.where(i < lat_tiles, 0, 1), 0, 0)),
                  const((D_MODEL, 2 * FFN_HIDDEN)), const((FFN_HIDDEN, D_MODEL)),
                  pl.BlockSpec((2, D_MODEL), lambda b, i: (0, 0))],
        out_specs=tok,
        out_shape=jax.ShapeDtypeStruct((bsz, t_all, D_MODEL), F32),
        compiler_params=_cparams(("parallel", "parallel"), V7X_VMEM_LIMIT_MB),
        name="ffn",
    )(x_all, mods, win, wout, ln)


def _layer(x_lat, x_ctx, ctx_blk, mods, p, seq, ctx_len, want_ctx):
    nt_lat = seq // ROW_TILE
    t_all = seq + ctx_len
    s5u, z, xbc, hy, dt_raw = _inproj(x_lat, x_ctx, ctx_blk, mods, p["w_in"], nt_lat)

    s5y = _s5_scan(s5u, p["s5_ncat"], p["s5_tz"], p["s5_mcat"], p["s5_coef"], p["s5_dvec"],
                   seq // S5_CHUNK, ctx_len // S5_CHUNK)

    (xbc_c,) = _dwconv(xbc, p["ssd_conv_w"], p["ssd_conv_b"], SSD_CONV, True, nt_lat, 1, BF16)
    yssd = _ssd_scan(xbc_c, dt_raw, p["ssd_tri"], p["ssd_par"], p["ssd_expand"],
                     seq // SSD_CHUNK, ctx_len // SSD_CHUNK)

    v, x1, x2 = _dwconv(hy, p["hy_conv_w"], p["hy_conv_b"], HY_SHORT, False, nt_lat, 3, BF16)
    hyz = _hyena_lat(v, x1, x2, p["hy_fa"], p["hy_ga"], p["hy_mfwd"], p["hy_minv"],
                     p["hy_kr"], p["hy_ki"], p["hy_bias"], seq)
    if want_ctx:
        hyz = _hyena_ctx(v, x1, x2, *p["hy_ctx_mats"], *p["hy_ctx_taps"], p["hy_bias"], hyz, seq, ctx_len)

    rows = t_all if want_ctx else seq
    x1_all = _mixout(x_lat, x_ctx, ctx_blk, mods, s5y, yssd, xbc_c, z, hyz, p["s5_wglu"], p["s5_bglu"], p["ssd_vec"],
                     p["w_out"], p["ln1"], nt_lat, rows)
    return _ffn(x1_all, mods, p["ffn_w_in"], p["ffn_w_out"], p["ln2"], nt_lat)


def kernel(x, c, ctx, c_ctx, w_mod, b_mod, w_in, s5_lam_re, s5_lam_im, s5_log_step, s5_b_re, s5_b_im, s5_c_re, s5_c_im, s5_d, s5_w_glu, s5_b_glu, ssd_conv_w, ssd_conv_b, ssd_dt_bias, ssd_a_log, ssd_d, ssd_norm_w, hy_conv_w, hy_conv_b, hy_w1, hy_b1, hy_w2, hy_b2, hy_w3, hy_freq, hy_decay, hy_bias, w_out, ln1_g, ln1_b, ffn_w_in, ffn_w_out, ln2_g, ln2_b):
    bsz, seq, _ = x.shape
    ctx_len = ctx.shape[1]
    assert bsz == 8 and seq % ROW_TILE == 0 and ctx_len == ROW_TILE

    fa, ga, mfwd, minv = _hyena_lat_consts(seq)
    ctx_mats = _hyena_ctx_consts(ctx_len)
    tt = np.arange(SSD_CHUNK)
    tri = jnp.asarray(np.stack([tt[None, :] <= tt[:, None], tt[None, :] >= tt[:, None]]), BF16)
    expand = jnp.asarray(np.repeat(np.eye(128, SSD_HEADS), SSD_HEADDIM, axis=1)[:, :SSD_INNER], BF16)
    cvec = jnp.zeros((16, D_MODEL), F32).at[:bsz].set(c.astype(F32)).at[bsz].set(c_ctx.astype(F32))

    x_lat, x_ctx, ctx_blk = x.astype(F32), ctx.astype(F32), 0
    for l in range(DEPTH):
        want_ctx = l < DEPTH - 1
        mod16 = _modulation(cvec, w_mod[l].astype(F32), b_mod[l].astype(F32).reshape(1, -1))
        mods = jnp.stack([mod16[:bsz].reshape(bsz, 6, D_MODEL),
                          jnp.broadcast_to(mod16[bsz].reshape(1, 6, D_MODEL), (bsz, 6, D_MODEL))], axis=1)
        wl = w_in[l]
        w_p = jnp.concatenate([wl[:, 0:256], wl[:, 256:768], wl[:, 768:1792], wl[:, 1800:2568], wl[:, 1792:1800],
                               jnp.zeros((D_MODEL, 120), wl.dtype)], axis=1).astype(BF16)
        ncat, tz, mcat, coef = _s5_weights(s5_lam_re[l], s5_lam_im[l], s5_log_step[l], s5_b_re[l], s5_b_im[l],
                                           s5_c_re[l], s5_c_im[l])
        rep = lambda t: jnp.repeat(t.astype(F32), SSD_HEADDIM, axis=-1)
        par = jnp.zeros((2, 8, 128), F32)
        par = par.at[:, 0, :SSD_HEADS].set(ssd_dt_bias[l].astype(F32))
        par = par.at[:, 1, :SSD_HEADS].set(-jnp.exp(ssd_a_log[l].astype(F32)))
        hy_args = (hy_w1[l], hy_b1[l], hy_w2[l], hy_b2[l], hy_w3[l], hy_freq[l], hy_decay[l])
        kr, ki = _hyena_spectrum(*_hyena_filter_taps(seq, *hy_args), fa, mfwd)
        p = dict(
            w_in=w_p, s5_ncat=ncat, s5_tz=tz, s5_mcat=mcat, s5_coef=coef,
            ssd_conv_w=ssd_conv_w[l].astype(F32), ssd_conv_b=ssd_conv_b[l].astype(F32),
            ssd_tri=tri, ssd_par=par, ssd_expand=expand,
            hy_conv_w=hy_conv_w[l].astype(F32), hy_conv_b=hy_conv_b[l].astype(F32),
            hy_fa=fa, hy_ga=ga, hy_mfwd=mfwd, hy_minv=minv, hy_kr=kr, hy_ki=ki,
            hy_bias=hy_bias[l].astype(F32)[:, None, :],
            s5_wglu=s5_w_glu[l].astype(BF16),
            s5_bglu=s5_b_glu[l].astype(F32).reshape(1, S5_WIDTH),
            s5_dvec=jnp.tile(s5_d[l].astype(F32).reshape(S5_GROUPS, 1, S5_GROUP_CH), (1, 1, S5_CHUNK)),
            ssd_vec=jnp.stack([rep(ssd_d[l]), ssd_norm_w[l].astype(F32)]),
            w_out=w_out[l].astype(BF16), ln1=jnp.stack([ln1_g[l], ln1_b[l]]).astype(F32),
            ffn_w_in=ffn_w_in[l].astype(BF16), ffn_w_out=ffn_w_out[l].astype(BF16),
            ln2=jnp.stack([ln2_g[l], ln2_b[l]]).astype(F32),
        )
        if want_ctx:
            p.update(hy_ctx_mats=ctx_mats, hy_ctx_taps=_hyena_filter_taps(ctx_len, *hy_args))
        x_lat = _layer(x_lat, x_ctx, ctx_blk, mods, p, seq, ctx_len, want_ctx)
        x_ctx, ctx_blk = x_lat, seq // ROW_TILE
    return x_lat.astype(x.dtype)
```

```python
import functools
import math

import numpy as np
import jax
import jax.numpy as jnp
from jax import lax
from jax.experimental import pallas as pl
from jax.experimental.pallas import tpu as pltpu

F32 = jnp.float32
BF16 = jnp.bfloat16
HI = lax.Precision.HIGHEST

D_MODEL = 1024
DEPTH = 2
S5_WIDTH = 256
S5_GROUP_CH = 16
S5_GROUPS = 16
S5_STATE = 64
S5_MAX_RE = -1e-4
S5_CHUNK = 16
SSD_INNER = 512
SSD_HEADDIM = 64
SSD_HEADS = 8
SSD_GROUPS = 2
SSD_STATE = 128
SSD_CONV = 5
SSD_CHUNK = 128
SSD_CHUNKS_PER_STEP = 2
SSD_XBC = 1024
HY_WIDTH = 256
HY_ORDER = 2
HY_SHORT = 3
HY_BANDS = 16
HY_IN = 768
MIX_IN_PAD = 2688
FFN_HIDDEN = 2816
ALPHA = (2 * DEPTH) ** 0.25
LN_EPS = 1e-6

ROW_TILE = 256
FFN_ROW_TILE = 512
CONV_GROUP = 256
HALO = 16
FFT_N2 = 128
HY_A_PITCH = 72
HY_B_PITCH = 136
V7X_VMEM_LIMIT_MB = 56


def _cparams(sem, vmem_mb=None):
    kw = dict(dimension_semantics=sem)
    if vmem_mb is not None:
        kw["vmem_limit_bytes"] = vmem_mb * 2 ** 20
    return pltpu.CompilerParams(**kw)


def _standardise(x):
    mu = jnp.mean(x, axis=-1, keepdims=True)
    xc = x - mu
    var = jnp.mean(xc * xc, axis=-1, keepdims=True)
    return xc * lax.rsqrt(var + LN_EPS)


def _sigmoid(x):
    return 1.0 / (1.0 + jnp.exp(-x))


def _silu(x):
    return x * _sigmoid(x)


def _gelu_tanh(x):
    return 0.5 * x * (1.0 + jnp.tanh(0.7978845608028654 * (x + 0.044715 * (x * x * x))))


def _softplus(x):
    return jnp.maximum(x, 0.0) + jnp.log(1.0 + jnp.exp(-jnp.abs(x)))


def _bdot(a, b):
    return jnp.dot(a.astype(BF16), b.astype(BF16), preferred_element_type=F32)


def _split3(a):
    a1 = a.astype(BF16)
    r1 = a - a1.astype(F32)
    a2 = r1.astype(BF16)
    a3 = (r1 - a2.astype(F32)).astype(BF16)
    return a1, a2, a3


def _dot_sel_lhs(sel, a):
    a1, a2, a3 = _split3(a)
    d = functools.partial(jnp.dot, preferred_element_type=F32)
    return d(sel, a1) + d(sel, a2) + d(sel, a3)


def _dot_sel_rhs(a, sel):
    a1, a2, a3 = _split3(a)
    d = functools.partial(jnp.dot, preferred_element_type=F32)
    return d(a1, sel) + d(a2, sel) + d(a3, sel)


def _mod_kernel(c_ref, w_ref, b_ref, o_ref):
    ca = _silu(c_ref[...])
    c1, c2, c3 = _split3(ca)
    w1, w2, w3 = _split3(w_ref[...])
    d = functools.partial(jnp.dot, preferred_element_type=F32)
    acc = d(c1, w1) + d(c1, w2) + d(c2, w1) + d(c1, w3) + d(c2, w2) + d(c3, w1)
    o_ref[...] = acc + b_ref[...]


def _modulation(cvec, w, b):
    n = w.shape[1]
    tn = 1536
    return pl.pallas_call(
        _mod_kernel,
        grid=(n // tn,),
        in_specs=[pl.BlockSpec((16, D_MODEL), lambda j: (0, 0)),
                  pl.BlockSpec((D_MODEL, tn), lambda j: (0, j)),
                  pl.BlockSpec((1, tn), lambda j: (0, j))],
        out_specs=pl.BlockSpec((16, tn), lambda j: (0, j)),
        out_shape=jax.ShapeDtypeStruct((16, n), F32),
        compiler_params=_cparams(("arbitrary",), 40),
        name="modulation",
    )(cvec, w, b)


def _rows_of(nt_lat, xl_ref, xc_ref):
    return jnp.where(pl.program_id(1) < nt_lat, xl_ref[0], xc_ref[0])


def _row_specs(nt_lat, ctx_blk, width):
    return [pl.BlockSpec((1, ROW_TILE, width), lambda b, i: (b, jnp.minimum(i, nt_lat - 1), 0)),
            pl.BlockSpec((1, ROW_TILE, width), lambda b, i: (b, ctx_blk, 0))]


def _dwconv_rows(ext_ref, w_ref, b_ref, wcol, taps, act, o_ref, ocol):
    pad = taps // 2
    rb = 64
    for c in range(0, CONV_GROUP, 128):
        wk = [w_ref[k:k + 1, wcol + c:wcol + c + 128] for k in range(taps)]
        bias = b_ref[0:1, wcol + c:wcol + c + 128]
        for r0 in range(0, ROW_TILE, rb):
            base = HALO - pad + r0
            acc = bias + wk[0] * ext_ref[base:base + rb, c:c + 128]
            for k in range(1, taps):
                acc = acc + wk[k] * ext_ref[base + k:base + k + rb, c:c + 128]
            if act:
                acc = _silu(acc)
            o_ref[0, r0:r0 + rb, ocol + c:ocol + c + 128] = acc.astype(o_ref.dtype)


def _inproj_kernel(nt_lat, xl_ref, xc_ref, xp_ref, xn_ref, m_ref, w_ref, w5_ref, b5_ref, w3_ref, b3_ref,
                   s5_ref, z_ref, xbc_ref, v_ref, x1_ref, x2_ref, dt_ref, *ext_refs):
    i = pl.program_id(1)
    m = m_ref[0, 0]
    mod = lambda x: (_standardise(x) * (1.0 + m[1:2]) + m[0:1]).astype(BF16)
    d = functools.partial(jnp.dot, preferred_element_type=F32)
    h = mod(_rows_of(nt_lat, xl_ref, xc_ref))
    hp, hn = mod(xp_ref[0]), mod(xn_ref[0])
    has_prev = jnp.logical_and(i > 0, i < nt_lat)
    has_next = i < nt_lat - 1
    n5 = SSD_XBC // CONV_GROUP
    hy_outs = (v_ref, x1_ref, x2_ref)

    def project(k):
        wc = w_ref[:, 768 + k * CONV_GROUP:768 + (k + 1) * CONV_GROUP]
        e = ext_refs[k]
        e[0:HALO, :] = jnp.where(has_prev, d(hp, wc), 0.0)
        e[HALO:HALO + ROW_TILE, :] = d(h, wc)
        e[HALO + ROW_TILE:2 * HALO + ROW_TILE, :] = jnp.where(has_next, d(hn, wc), 0.0)

    def conv(k):
        if k < n5:
            _dwconv_rows(ext_refs[k], w5_ref, b5_ref, k * CONV_GROUP, SSD_CONV, True, xbc_ref, k * CONV_GROUP)
        else:
            _dwconv_rows(ext_refs[k], w3_ref, b3_ref, (k - n5) * CONV_GROUP, HY_SHORT, False, hy_outs[k - n5], 0)

    ngroups = len(ext_refs)
    project(0)
    for k in range(ngroups):
        if k + 1 < ngroups:
            project(k + 1)
        conv(k)
    s5_ref[0] = d(h, w_ref[:, 0:256])
    z_ref[0] = d(h, w_ref[:, 256:768]).astype(z_ref.dtype)
    dt_ref[0] = d(h, w_ref[:, 2560:2688])


def _inproj(x_lat, x_ctx, ctx_blk, mods, w_p, conv5_w, conv5_b, conv3_w, conv3_b, nt_lat):
    bsz = x_lat.shape[0]
    nt = nt_lat + 1
    t_all = nt * ROW_TILE
    rh = ROW_TILE // HALO
    last = nt_lat * rh - 1
    full = lambda shp: pl.BlockSpec(shp, lambda b, i: (0,) * len(shp))
    outs = ((256, F32), (512, BF16), (SSD_XBC, BF16), (HY_WIDTH, BF16), (HY_WIDTH, BF16), (HY_WIDTH, BF16), (128, F32))
    return pl.pallas_call(
        functools.partial(_inproj_kernel, nt_lat),
        grid=(bsz, nt),
        in_specs=_row_specs(nt_lat, ctx_blk, D_MODEL) + [
                  pl.BlockSpec((1, HALO, D_MODEL), lambda b, i: (b, jnp.clip(i * rh - 1, 0, last), 0)),
                  pl.BlockSpec((1, HALO, D_MODEL), lambda b, i: (b, jnp.clip((i + 1) * rh, 0, last), 0)),
                  pl.BlockSpec((1, 1, 6, D_MODEL), lambda b, i: (b, jnp.where(i < nt_lat, 0, 1), 0, 0)),
                  full((D_MODEL, MIX_IN_PAD)), full((SSD_CONV, SSD_XBC)), full((1, SSD_XBC)),
                  full((HY_SHORT, HY_IN)), full((1, HY_IN))],
        out_specs=[pl.BlockSpec((1, ROW_TILE, w), lambda b, i: (b, i, 0)) for w, _ in outs],
        out_shape=[jax.ShapeDtypeStruct((bsz, t_all, w), dt) for w, dt in outs],
        scratch_shapes=[pltpu.VMEM((ROW_TILE + 2 * HALO, CONV_GROUP), F32)] * ((SSD_XBC + HY_IN) // CONV_GROUP),
        compiler_params=_cparams(("parallel", "parallel"), 40),
        name="inproj",
    )(x_lat, x_ctx, x_lat, x_lat, mods, w_p, conv5_w, conv5_b.reshape(1, -1), conv3_w, conv3_b.reshape(1, -1))


def _s5_kernel(n_lat, n_ctx, u0_ref, u1_ref, ncat_ref, tz_ref, mcat_ref, coef_ref, dvec_ref, y_ref,
               ut_ref, s_ref, hp_ref):
    q, gch, ng = S5_CHUNK, S5_GROUP_CH, S5_GROUPS
    nc = n_lat + n_ctx
    dot = functools.partial(jnp.dot, preferred_element_type=F32)
    u_refs = (u0_ref, u1_ref)
    gpl = 128 // gch
    for j in range(2):
        for s in range(q):
            rows = u_refs[j][0, pl.ds(s, nc, stride=q), :]
            for gg in range(gpl):
                ut_ref[gpl * j + gg, :, s * gch:(s + 1) * gch] = rows[:, gg * gch:(gg + 1) * gch]
    for g in range(ng):
        sg = dot(ut_ref[g].astype(BF16), ncat_ref[g])
        for k in range(4):
            s_ref[k, pl.ds(g, nc, stride=ng), :] = sg[:, k * 128:(k + 1) * 128]
    c1f, c2f, c1b, c2b = coef_ref[0], coef_ref[1], coef_ref[2], coef_ref[3]

    def step(cf, cb, carry):
        hf, hsf, hb, hsb = carry
        rf = pl.multiple_of(cf * ng, ng)
        rb = pl.multiple_of(cb * ng, ng)
        hp_ref[0, pl.ds(rf, ng), :] = hf
        hp_ref[1, pl.ds(rb, ng), :] = hb
        sf = s_ref[0, pl.ds(rf, ng), :]
        sb = s_ref[1, pl.ds(rb, ng), :]
        ssf = s_ref[2, pl.ds(rf, ng), :]
        ssb = s_ref[3, pl.ds(rb, ng), :]
        return (c1f * hf + c2f * hsf + sf, c1f * hsf - c2f * hf + ssf,
                c1b * hb + c2b * hsb + sb, c1b * hsb - c2b * hb + ssb)

    z = jnp.zeros((ng, 128), F32)
    carry = lax.fori_loop(0, n_ctx, lambda i, c: step(n_lat + i, n_lat + n_ctx - 1 - i, c), (z, z, z, z))
    lax.fori_loop(0, n_lat, lambda i, c: step(i, n_lat - 1 - i, c), carry)
    for g in range(ng):
        ug = ut_ref[g]
        hp = jnp.concatenate([hp_ref[0, pl.ds(g, nc, stride=ng), :], hp_ref[1, pl.ds(g, nc, stride=ng), :]], axis=1)
        ut_ref[g] = dot(ug.astype(BF16), tz_ref[g]) + dot(hp.astype(BF16), mcat_ref[g]) + ug * dvec_ref[g]
    for j in range(2):
        for s in range(q):
            rows = jnp.concatenate([ut_ref[gpl * j + gg, :, s * gch:(s + 1) * gch] for gg in range(gpl)], axis=1)
            y_ref[0, j, pl.ds(s, nc, stride=q), :] = rows


def _s5_scan(s5u, ncat, tz, mcat, coef, dvec, n_lat, n_ctx):
    bsz, t_all, _ = s5u.shape
    nc = n_lat + n_ctx
    ng = S5_GROUPS
    const = lambda shp: pl.BlockSpec(shp, lambda b: (0,) * len(shp), pipeline_mode=pl.Buffered(1))
    return pl.pallas_call(
        functools.partial(_s5_kernel, n_lat, n_ctx),
        grid=(bsz,),
        in_specs=[pl.BlockSpec((1, t_all, 128), lambda b: (b, 0, 0)),
                  pl.BlockSpec((1, t_all, 128), lambda b: (b, 0, 1)),
                  const((ng, 256, 512)), const((ng, 256, 256)), const((ng, 256, 256)),
                  const((4, ng, 128)), const((ng, 1, 256))],
        out_specs=pl.BlockSpec((1, 2, t_all, 128), lambda b: (b, 0, 0, 0)),
        out_shape=jax.ShapeDtypeStruct((bsz, 2, t_all, 128), F32),
        scratch_shapes=[pltpu.VMEM((ng, nc, 256), F32), pltpu.VMEM((4, nc * ng, 128), F32),
                        pltpu.VMEM((2, nc * ng, 128), F32)],
        compiler_params=_cparams(("parallel",), V7X_VMEM_LIMIT_MB),
        name="s5_scan",
    )(s5u, s5u, ncat, tz, mcat, coef, dvec)


def _s5_weights(lam_re, lam_im, log_step, b_re, b_im, c_re, c_im):
    q, ng, gch, ns = S5_CHUNK, S5_GROUPS, S5_GROUP_CH, S5_STATE
    a_re = jnp.minimum(lam_re.astype(F32), S5_MAX_RE)
    a_im = lam_im.astype(F32)
    step = jnp.exp(log_step.astype(F32))[..., None]
    taus = jnp.arange(q + 1, dtype=F32)[:, None, None, None]
    mag = jnp.exp(taus * (a_re * step))
    pr, pi = mag * jnp.cos(taus * (a_im * step)), mag * jnp.sin(taus * (a_im * step))
    nr, ni, den = pr[1] - 1.0, pi[1], a_re * a_re + a_im * a_im
    fr, fi = ((nr * a_re + ni * a_im) / den)[..., None], ((ni * a_re - nr * a_im) / den)[..., None]
    b_r, b_i = b_re.astype(F32), b_im.astype(F32)
    bb_r, bb_i = fr * b_r - fi * b_i, fr * b_i + fi * b_r
    c_r = jnp.swapaxes(c_re.astype(F32), -1, -2)
    c_i = jnp.swapaxes(c_im.astype(F32), -1, -2)
    cp_r, cp_i = c_r * pr[..., None] - c_i * pi[..., None], c_r * pi[..., None] + c_i * pr[..., None]
    pb_r, pb_i = pr[..., None] * bb_r - pi[..., None] * bb_i, pr[..., None] * bb_i + pi[..., None] * bb_r
    kern = (jnp.einsum('tdgph,dgpk->tdghk', cp_r[:q], bb_r, precision=HI)
            - jnp.einsum('tdgph,dgpk->tdghk', cp_i[:q], bb_i, precision=HI))
    tt = np.arange(q)
    lag = tt[None, :] - tt[:, None]
    sel_f = jnp.asarray((lag[..., None] == tt).astype(np.float32))
    sel_b = jnp.asarray((-lag[..., None] == tt).astype(np.float32))
    tz = (jnp.einsum('stu,ughk->gskth', sel_f, kern[:, 0], precision=HI)
          + jnp.einsum('stu,ughk->gskth', sel_b, kern[:, 1], precision=HI)).reshape(ng, q * gch, q * gch)
    rows_sk = lambda t: jnp.transpose(t, (1, 0, 3, 2)).reshape(ng, q * gch, ns)
    nf_r, nf_i = rows_sk(pb_r[q - 1::-1, 0]), rows_sk(pb_i[q - 1::-1, 0])
    nb_r, nb_i = rows_sk(pb_r[:q, 1]), rows_sk(pb_i[:q, 1])
    ncat = jnp.concatenate([nf_r, nf_i, nb_r, nb_i, nf_i, nf_r, nb_i, nb_r], axis=-1)
    cols_th = lambda t: jnp.transpose(t, (1, 2, 0, 3)).reshape(ng, ns, q * gch)
    mcat = jnp.concatenate([cols_th(cp_r[1:, 0]), -cols_th(cp_i[1:, 0]),
                            cols_th(cp_r[q:0:-1, 1]), -cols_th(cp_i[q:0:-1, 1])], axis=1)
    lr, li = pr[q], pi[q]
    coef = jnp.stack([jnp.concatenate([lr[0], lr[0]], axis=-1), jnp.concatenate([-li[0], li[0]], axis=-1),
                      jnp.concatenate([lr[1], lr[1]], axis=-1), jnp.concatenate([-li[1], li[1]], axis=-1)])
    return ncat.astype(BF16), tz.astype(BF16), mcat.astype(BF16), coef.astype(F32)


def _ssd_kernel(cps, xbc_ref, dt_ref, tri_ref, par_ref, exp_ref, y_ref, st_ref):
    direction = pl.program_id(1)

    @pl.when(pl.program_id(2) == 0)
    def _():
        st_ref[...] = jnp.zeros_like(st_ref)

    tri = tri_ref[0]
    mask = tri > 0
    lane = lax.broadcasted_iota(jnp.int32, (SSD_CHUNK, 128), 1)
    lo = lane < SSD_HEADDIM
    zero_b = jnp.zeros((SSD_CHUNK, 128), BF16)
    expand = lambda t: jnp.dot(t.astype(BF16), exp_ref[...], preferred_element_type=F32)
    state = [st_ref[:, g * 256:(g + 1) * 256] for g in range(SSD_GROUPS)]
    for c in range(cps):
        sub = jnp.where(direction == 0, c, cps - 1 - c)
        rows = pl.ds(pl.multiple_of(sub * SSD_CHUNK, SSD_CHUNK), SSD_CHUNK)
        xs = xbc_ref[0, rows, 0:512].astype(F32)
        dt_c = _softplus(dt_ref[0, rows, :] + par_ref[0, 0:1, :])
        a_c = par_ref[0, 1:2, :] * dt_c
        cs = _dot_sel_lhs(tri, a_c)
        tot = jnp.sum(a_c, axis=0, keepdims=True)
        cs_t = cs.T
        xdt = xs * expand(dt_c)
        xdt_b = xdt.astype(BF16)
        xd_end = (xdt * expand(jnp.exp(tot - cs))).astype(BF16)
        e_cs = expand(jnp.exp(cs))
        e_tot = jnp.exp(_dot_sel_rhs(jnp.broadcast_to(tot, (16, 128)), exp_ref[...]))[0:1]
        for g in range(SSD_GROUPS):
            bm_b = xbc_ref[0, rows, 512 + g * 128:512 + (g + 1) * 128]
            cm_b = xbc_ref[0, rows, 768 + g * 128:768 + (g + 1) * 128]
            cb = lax.dot_general(cm_b, bm_b, (((1,), (1,)), ((), ())), preferred_element_type=F32)
            y_off = jnp.dot(cm_b, state[g].astype(BF16), preferred_element_type=F32) * e_cs[:, g * 256:(g + 1) * 256]
            for j in range(2):
                c0 = g * 256 + j * 128
                x_pair = xdt_b[:, c0:c0 + 128]
                acc = y_off[:, j * 128:(j + 1) * 128]
                for hh in range(2):
                    head = c0 // SSD_HEADDIM + hh
                    decay = jnp.exp(jnp.where(mask, cs[:, head:head + 1] - cs_t[head:head + 1, :], -1e30))
                    gm = (cb * decay).astype(BF16)
                    xh = jnp.where(lo if hh == 0 else jnp.logical_not(lo), x_pair, zero_b)
                    acc = acc + jnp.dot(gm, xh, preferred_element_type=F32)
                y_ref[0, 0, rows, c0:c0 + 128] = acc.astype(y_ref.dtype)
            state[g] = (state[g] * e_tot[:, g * 256:(g + 1) * 256]
                        + jnp.dot(bm_b.astype(F32).T.astype(BF16), xd_end[:, g * 256:(g + 1) * 256],
                                  preferred_element_type=F32))
    for g in range(SSD_GROUPS):
        st_ref[:, g * 256:(g + 1) * 256] = state[g]


def _ssd_scan(xbc_c, dt_raw, tri, par, expand, n_lat, n_ctx):
    bsz, t_all, _ = xbc_c.shape
    cps = SSD_CHUNKS_PER_STEP
    assert n_lat % cps == 0 and n_ctx % cps == 0
    n_lat, n_ctx = n_lat // cps, n_ctx // cps
    nc = n_lat + n_ctx
    rows = cps * SSD_CHUNK

    def blk(d, i):
        fwd = jnp.where(i < n_ctx, n_lat + i, i - n_ctx)
        return jnp.where(d == 0, fwd, nc - 1 - i)

    return pl.pallas_call(
        functools.partial(_ssd_kernel, cps),
        grid=(bsz, 2, nc),
        in_specs=[pl.BlockSpec((1, rows, SSD_XBC), lambda b, d, i: (b, blk(d, i), 0)),
                  pl.BlockSpec((1, rows, 128), lambda b, d, i: (b, blk(d, i), 0)),
                  pl.BlockSpec((1, SSD_CHUNK, SSD_CHUNK), lambda b, d, i: (d, 0, 0)),
                  pl.BlockSpec((1, 8, 128), lambda b, d, i: (d, 0, 0)),
                  pl.BlockSpec((128, SSD_INNER), lambda b, d, i: (0, 0))],
        out_specs=pl.BlockSpec((1, 1, rows, SSD_INNER), lambda b, d, i: (b, d, blk(d, i), 0)),
        out_shape=jax.ShapeDtypeStruct((bsz, 2, t_all, SSD_INNER), BF16),
        scratch_shapes=[pltpu.VMEM((SSD_STATE, SSD_INNER), F32)],
        compiler_params=_cparams(("parallel", "parallel", "arbitrary")),
        name="ssd_scan",
    )(xbc_c, dt_raw, tri, par, expand)


def _hy_lat_kernel(k1n, kgrp, v_ref, x1_ref, x2_ref, fa_ref, ga_ref, mf_ref, mi_ref, kr_ref, ki_ref, bias_ref,
                   o_ref, z_ref, a2_ref, b2_ref):
    order = pl.program_id(1)
    ph = pl.program_id(2)
    n2 = FFT_N2
    kp, h1 = fa_ref.shape
    ks = ga_ref.shape[1]
    seq = h1 * n2
    ngroups = k1n // kgrp
    dot = functools.partial(jnp.dot, preferred_element_type=F32)

    @pl.when(jnp.logical_and(order == 0, ph == 0))
    def _():
        for j in range(2):
            z_ref[j] = v_ref[0, :, j * 128:(j + 1) * 128].astype(F32)
            b2_ref[j, 2 * k1n * HY_B_PITCH:ks * HY_B_PITCH, :] = jnp.zeros(((ks - 2 * k1n) * HY_B_PITCH, 128), F32)

    @pl.when(ph == 0)
    def _():
        fa = fa_ref[...]

        def body(i, carry):
            u = jnp.concatenate([z_ref[j, pl.ds(i, h1, stride=n2), :] for j in range(2)], axis=1)
            r = dot(fa, u.astype(BF16))
            row = pl.multiple_of(i * HY_A_PITCH, 8)
            for j in range(2):
                a2_ref[j, pl.ds(row, kp), :] = r[:, j * 128:(j + 1) * 128]
            return carry

        lax.fori_loop(0, n2, body, 0, unroll=8)

    @pl.when(jnp.logical_and(ph >= 1, ph <= ngroups))
    def _():
        for t in range(kgrp):
            k1 = (ph - 1) * kgrp + t
            are = jnp.concatenate([a2_ref[j, pl.ds(2 * k1, n2, stride=HY_A_PITCH), :] for j in range(2)], axis=1)
            aim = jnp.concatenate([a2_ref[j, pl.ds(2 * k1 + 1, n2, stride=HY_A_PITCH), :] for j in range(2)], axis=1)
            x = dot(mf_ref[t], jnp.concatenate([are, aim], axis=0).astype(BF16))
            xr, xi = x[:n2], x[n2:]
            kr, ki = kr_ref[t], ki_ref[t]
            y = jnp.concatenate([xr * kr - xi * ki, xr * ki + xi * kr], axis=0).astype(BF16)
            bq = dot(mi_ref[t], y)
            row = pl.multiple_of(2 * k1 * HY_B_PITCH, 8)
            for j in range(2):
                b2_ref[j, pl.ds(row, n2), :] = bq[:n2, j * 128:(j + 1) * 128]
                b2_ref[j, pl.ds(row + HY_B_PITCH, n2), :] = bq[n2:, j * 128:(j + 1) * 128]

    @pl.when(ph == ngroups + 1)
    def _():
        ga = ga_ref[...]

        def body(i, carry):
            bs = jnp.concatenate([b2_ref[j, pl.ds(i, ks, stride=HY_B_PITCH), :] for j in range(2)], axis=1)
            y = dot(ga, bs.astype(BF16))
            for j in range(2):
                o_ref[0, j, pl.ds(i, h1, stride=n2), :] = y[:, j * 128:(j + 1) * 128]
            return carry

        lax.fori_loop(0, n2, body, 0, unroll=8)
        rb = min(256, seq)

        def gate(i, carry):
            r0 = pl.multiple_of(i * rb, 8)
            for j in range(2):
                conv = o_ref[0, j, pl.ds(r0, rb), :] + z_ref[j, pl.ds(r0, rb), :] * bias_ref[0, :, j * 128:(j + 1) * 128]
                x1 = x1_ref[0, pl.ds(r0, rb), j * 128:(j + 1) * 128].astype(F32)
                x2 = x2_ref[0, pl.ds(r0, rb), j * 128:(j + 1) * 128].astype(F32)
                z_ref[j, pl.ds(r0, rb), :] = x1 * conv
                o_ref[0, j, pl.ds(r0, rb), :] = x2 * conv
            return carry

        lax.fori_loop(0, seq // rb, gate, 0)


def _hy_spec_kernel(k1n, ts_ref, td_ref, fa_ref, mf_ref, kr_ref, ki_ref, a2_ref):
    n2 = FFT_N2
    kp, h1 = fa_ref.shape
    dot = functools.partial(jnp.dot, preferred_element_type=F32)
    fa = fa_ref[...]

    def body(i, carry):
        u = jnp.concatenate([ts_ref[pl.ds(i, h1, stride=n2), :], td_ref[pl.ds(i, h1, stride=n2), :]], axis=1)
        r = dot(fa, u.astype(BF16))
        row = pl.multiple_of(i * HY_A_PITCH, 8)
        for j in range(2):
            a2_ref[j, pl.ds(row, kp), :] = r[:, j * 128:(j + 1) * 128]
        return carry

    lax.fori_loop(0, n2, body, 0, unroll=8)

    def per_k1(k1, carry):
        are = jnp.concatenate([a2_ref[j, pl.ds(2 * k1, n2, stride=HY_A_PITCH), :] for j in range(2)], axis=1)
        aim = jnp.concatenate([a2_ref[j, pl.ds(2 * k1 + 1, n2, stride=HY_A_PITCH), :] for j in range(2)], axis=1)
        x = dot(mf_ref[k1], jnp.concatenate([are, aim], axis=0).astype(BF16))
        kr_ref[k1] = x[:n2, 0:128]
        ki_ref[k1] = x[n2:, 128:256]
        return carry

    lax.fori_loop(0, k1n, per_k1, 0)


def _hyena_spectrum(tsum, tdiff, fa, mfwd):
    seq, lanes = tsum.shape
    n2 = FFT_N2
    kp, h1 = fa.shape
    k1n = mfwd.shape[0]
    taps = pl.BlockSpec((seq, 128), lambda j: (0, j))
    out = pl.BlockSpec((k1n, n2, 128), lambda j: (0, 0, j))
    return pl.pallas_call(
        functools.partial(_hy_spec_kernel, k1n),
        grid=(lanes // 128,),
        in_specs=[taps, taps, pl.BlockSpec((kp, h1), lambda j: (0, 0)),
                  pl.BlockSpec((k1n, 2 * n2, 2 * n2), lambda j: (0, 0, 0))],
        out_specs=[out, out],
        out_shape=[jax.ShapeDtypeStruct((k1n, n2, lanes), F32)] * 2,
        scratch_shapes=[pltpu.VMEM((2, n2 * HY_A_PITCH, 128), F32)],
        compiler_params=_cparams(("parallel",), 40),
        name="hyena_spectrum",
    )(tsum, tdiff, fa, mfwd)


def _hyena_lat(v, x1, x2, fa, ga, mfwd, minv, kr, ki, bias, seq):
    bsz, t_all, w = v.shape
    n2 = FFT_N2
    kp, h1 = fa.shape
    ks = ga.shape[1]
    k1n = mfwd.shape[0]
    kgrp = 3 if k1n % 3 == 0 else 1
    ngroups = k1n // kgrp
    tok = pl.BlockSpec((1, seq, w), lambda b, o, p: (b, 0, 0))
    full = lambda shp: pl.BlockSpec(shp, lambda b, o, p: (0,) * len(shp))
    grp = lambda p: jnp.clip(p - 1, 0, ngroups - 1)
    mat = pl.BlockSpec((kgrp, 2 * n2, 2 * n2), lambda b, o, p: (grp(p), 0, 0))
    spec = pl.BlockSpec((kgrp, n2, w), lambda b, o, p: (grp(p), 0, o))
    return pl.pallas_call(
        functools.partial(_hy_lat_kernel, k1n, kgrp),
        grid=(bsz, HY_ORDER, ngroups + 2),
        in_specs=[tok, tok, tok, full((kp, h1)), full((h1, ks)), mat, mat, spec, spec,
                  pl.BlockSpec((1, 1, w), lambda b, o, p: (o, 0, 0))],
        out_specs=pl.BlockSpec((1, 2, seq, 128), lambda b, o, p: (b, 0, 0, 0)),
        out_shape=jax.ShapeDtypeStruct((bsz, 2, seq, 128), F32),
        scratch_shapes=[pltpu.VMEM((2, seq, 128), F32),
                        pltpu.VMEM((2, n2 * HY_A_PITCH, 128), F32),
                        pltpu.VMEM((2, ks * HY_B_PITCH, 128), F32)],
        compiler_params=_cparams(("parallel", "arbitrary", "arbitrary"), V7X_VMEM_LIMIT_MB),
        name="hyena_lat",
    )(v, x1, x2, fa, ga, mfwd, minv, kr, ki, bias)


def _hyena_lat_consts(seq):
    n = 2 * seq
    n2 = FFT_N2
    n1 = n // n2
    h1 = n1 // 2
    k1n = n1 // 2 + 1
    kp = -(-2 * k1n // 8) * 8
    ks = -(-2 * k1n // 16) * 16
    assert kp <= HY_A_PITCH and n2 <= HY_B_PITCH
    k1 = np.arange(k1n)
    m1 = np.arange(h1)
    ang = 2.0 * np.pi * np.outer(k1, m1) / n1
    fa = np.zeros((kp, h1))
    fa[0:2 * k1n:2] = np.cos(ang)
    fa[1:2 * k1n:2] = -np.sin(ang)
    wgt = np.where((k1 == 0) | (k1 == n1 // 2), 1.0, 2.0) / n
    ga = np.zeros((h1, ks))
    ga[:, 0:2 * k1n:2] = (np.cos(ang) * wgt[:, None]).T
    ga[:, 1:2 * k1n:2] = (-np.sin(ang) * wgt[:, None]).T
    k2 = np.arange(n2)
    m2 = np.arange(n2)
    kk = k1[:, None, None] + n1 * k2[None, :, None]
    th = 2.0 * np.pi * ((kk * m2[None, None, :]) % n) / n
    mc, ms = np.cos(th), np.sin(th)
    mfwd = np.concatenate([np.concatenate([mc, ms], axis=2), np.concatenate([-ms, mc], axis=2)], axis=1)
    mct, mst = np.transpose(mc, (0, 2, 1)), np.transpose(ms, (0, 2, 1))
    minv = np.concatenate([np.concatenate([mct, -mst], axis=2), np.concatenate([mst, mct], axis=2)], axis=1)
    return tuple(jnp.asarray(t, BF16) for t in (fa, ga, mfwd, minv))


def _hy_ctx_kernel(v_ref, x1_ref, x2_ref, fc_ref, fs_ref, gc_ref, gs_ref, ts_ref, td_ref, bias_ref, z_ref):
    d = functools.partial(jnp.dot, preferred_element_type=F32)
    w = v_ref.shape[2]
    kr_all = d(fc_ref[...], ts_ref[...].astype(BF16))
    ki_all = d(fs_ref[...], td_ref[...].astype(BF16))

    def conv(u, o):
        ub = u.astype(BF16)
        cr, ci = d(fc_ref[...], ub), d(fs_ref[...], ub)
        kr, ki = kr_all[:, o * w:(o + 1) * w], ki_all[:, o * w:(o + 1) * w]
        pr = (cr * kr - ci * ki).astype(BF16)
        pi = (cr * ki + ci * kr).astype(BF16)
        return d(gc_ref[...], pr) + d(gs_ref[...], pi) + u * bias_ref[o]

    z = x1_ref[0].astype(F32) * conv(v_ref[0].astype(F32), 0)
    z = x2_ref[0].astype(F32) * conv(z, 1)
    for j in range(2):
        z_ref[0, j] = z[:, j * 128:(j + 1) * 128]


def _hyena_ctx(v, x1, x2, fc, fs, gc, gs, tsum, tdiff, bias, seq, ctx_len):
    bsz, _, w = v.shape
    blk = seq // ctx_len
    kpad = fc.shape[0]
    tok = pl.BlockSpec((1, ctx_len, w), lambda b: (b, blk, 0))
    full = lambda shp: pl.BlockSpec(shp, lambda b: (0,) * len(shp))
    return pl.pallas_call(
        _hy_ctx_kernel, grid=(bsz,),
        in_specs=[tok, tok, tok, full((kpad, ctx_len)), full((kpad, ctx_len)), full((ctx_len, kpad)),
                  full((ctx_len, kpad)), full((ctx_len, HY_ORDER * w)), full((ctx_len, HY_ORDER * w)), full((2, 1, w))],
        out_specs=pl.BlockSpec((1, 2, ctx_len, 128), lambda b: (b, 0, 0, 0)),
        out_shape=jax.ShapeDtypeStruct((bsz, 2, ctx_len, 128), F32),
        compiler_params=_cparams(("parallel",)),
        name="hyena_ctx",
    )(v, x1, x2, fc, fs, gc, gs, tsum, tdiff, bias)


def _hyena_ctx_consts(ctx_len):
    n = 2 * ctx_len
    nk = ctx_len + 1
    kpad = -(-nk // 128) * 128
    k = np.arange(nk)
    m = np.arange(ctx_len)
    ang = 2.0 * np.pi * np.outer(k, m) / n
    fc = np.zeros((kpad, ctx_len))
    fs = np.zeros((kpad, ctx_len))
    fc[:nk] = np.cos(ang)
    fs[:nk] = -np.sin(ang)
    wgt = np.where((k == 0) | (k == ctx_len), 1.0, 2.0) / n
    gc = np.zeros((ctx_len, kpad))
    gs = np.zeros((ctx_len, kpad))
    gc[:, :nk] = (np.cos(ang) * wgt[:, None]).T
    gs[:, :nk] = (-np.sin(ang) * wgt[:, None]).T
    return tuple(jnp.asarray(t, BF16) for t in (fc, fs, gc, gs))


def _hyena_filter_taps(length, w1, b1, w2, b2, w3, freq, decay):
    pos = jnp.arange(length, dtype=F32)
    t = pos / max(length - 1, 1)
    bands = jnp.linspace(1e-4, HY_BANDS - 1, HY_BANDS, dtype=F32)
    ang = (2.0 * math.pi / length) * pos[:, None] * bands
    feats = jnp.concatenate([t[:, None], jnp.cos(ang), -jnp.sin(ang)], axis=-1)
    freq = freq.astype(F32)
    mm = functools.partial(jnp.matmul, precision=HI)
    hid = jnp.sin(freq * (mm(feats, w1.astype(F32)) + b1.astype(F32)))
    hid = jnp.sin(freq * (mm(hid, w2.astype(F32)) + b2.astype(F32)))
    h = mm(hid, w3.astype(F32)) * jnp.exp(-t[:, None] * jnp.abs(decay.astype(F32)))
    h = h.reshape(length, 2, HY_ORDER, HY_WIDTH)
    h_fwd, h_bwd = h[:, 0], h[:, 1]
    l1 = (jnp.abs(h_fwd[0] + h_bwd[0]) + jnp.sum(jnp.abs(h_fwd[1:]), axis=0) + jnp.sum(jnp.abs(h_bwd[1:]), axis=0))
    flat = lambda t: (t / l1).reshape(length, HY_ORDER * HY_WIDTH)
    return flat(h_fwd + h_bwd), flat(h_fwd - h_bwd)


def _mixout_kernel(nt_lat, xl_ref, xc_ref, m_ref, s5y_ref, yf_ref, yb_ref, xs_ref, z_ref, hyl_ref, hyc_ref,
                   wglu_ref, bglu_ref, vec512_ref, wout_ref, ln_ref, o_ref):
    m = m_ref[0, 0]
    is_lat = pl.program_id(1) < nt_lat
    hy = [jnp.where(is_lat, hyl_ref[0, j], hyc_ref[0, j]).astype(BF16) for j in range(2)]
    d = functools.partial(jnp.dot, preferred_element_type=F32)
    y5 = _gelu_tanh(jnp.concatenate([s5y_ref[0, 0], s5y_ref[0, 1]], axis=1))
    y5 = y5 * _sigmoid(d(y5.astype(BF16), wglu_ref[...]) + bglu_ref[...])
    ys = yf_ref[0, 0].astype(F32) + yb_ref[0, 0].astype(F32) + vec512_ref[0:1, :] * xs_ref[0].astype(F32)
    gsd = ys * _silu(z_ref[0].astype(F32))
    gsd = gsd * lax.rsqrt(jnp.mean(gsd * gsd, axis=-1, keepdims=True) + LN_EPS) * vec512_ref[1:2, :]
    mix = (d(y5.astype(BF16), wout_ref[0:256, :]) + d(gsd.astype(BF16), wout_ref[256:768, :])
           + d(hy[0], wout_ref[768:896, :]) + d(hy[1], wout_ref[896:1024, :]))
    r = ALPHA * _rows_of(nt_lat, xl_ref, xc_ref) + m[2:3] * mix
    o_ref[0] = _standardise(r) * ln_ref[0:1, :] + ln_ref[1:2, :]


def _mixout(x_lat, x_ctx, ctx_blk, mods, s5y, yssd, xbc_c, z, hy_lat, hy_ctx, wglu, bglu, vec512, wout, ln,
            nt_lat, rows):
    bsz = x_lat.shape[0]
    nt = rows // ROW_TILE
    tok = lambda w: pl.BlockSpec((1, ROW_TILE, w), lambda b, i: (b, i, 0))
    halves = pl.BlockSpec((1, 2, ROW_TILE, 128), lambda b, i: (b, 0, i, 0))
    full = lambda shp: pl.BlockSpec(shp, lambda b, i: (0,) * len(shp))
    return pl.pallas_call(
        functools.partial(_mixout_kernel, nt_lat), grid=(bsz, nt),
        in_specs=_row_specs(nt_lat, ctx_blk, D_MODEL) + [
                  pl.BlockSpec((1, 1, 6, D_MODEL), lambda b, i: (b, jnp.where(i < nt_lat, 0, 1), 0, 0)),
                  halves,
                  pl.BlockSpec((1, 1, ROW_TILE, SSD_INNER), lambda b, i: (b, 0, i, 0)),
                  pl.BlockSpec((1, 1, ROW_TILE, SSD_INNER), lambda b, i: (b, 1, i, 0)),
                  tok(SSD_INNER), tok(SSD_INNER),
                  pl.BlockSpec((1, 2, ROW_TILE, 128), lambda b, i: (b, 0, jnp.minimum(i, nt_lat - 1), 0)),
                  pl.BlockSpec((1, 2, ROW_TILE, 128), lambda b, i: (b, 0, 0, 0)),
                  full((256, 256)), full((1, 256)), full((2, 512)), full((D_MODEL, D_MODEL)), full((2, D_MODEL))],
        out_specs=tok(D_MODEL),
        out_shape=jax.ShapeDtypeStruct((bsz, rows, D_MODEL), F32),
        compiler_params=_cparams(("parallel", "parallel"), 40),
        name="mixout",
    )(x_lat, x_ctx, mods, s5y, yssd, yssd, xbc_c, z, hy_lat, hy_ctx, wglu, bglu, vec512, wout, ln)


def _ffn_kernel(x_ref, m_ref, win_ref, wout_ref, ln_ref, o_ref):
    m = m_ref[0, 0]
    x = x_ref[0]
    d = functools.partial(jnp.dot, preferred_element_type=F32)
    h = (_standardise(x) * (1.0 + m[4:5]) + m[3:4]).astype(BF16)
    gate = d(h, win_ref[:, 0:FFN_HIDDEN])
    up = d(h, win_ref[:, FFN_HIDDEN:2 * FFN_HIDDEN])
    act = (_silu(gate) * up).astype(BF16)
    r = ALPHA * x + m[5:6] * d(act, wout_ref[...])
    o_ref[0] = _standardise(r) * ln_ref[0:1, :] + ln_ref[1:2, :]


def _ffn(x_all, mods, win, wout, ln, nt_lat):
    bsz, t_all, _ = x_all.shape
    tile = FFN_ROW_TILE if t_all % FFN_ROW_TILE == 0 else ROW_TILE
    nt = t_all // tile
    lat_tiles = nt_lat * ROW_TILE // tile
    tok = pl.BlockSpec((1, tile, D_MODEL), lambda b, i: (b, i, 0))
    const = lambda shp: pl.BlockSpec(shp, lambda b, i: (0, 0), pipeline_mode=pl.Buffered(1))
    return pl.pallas_call(
        _ffn_kernel, grid=(bsz, nt),
        in_specs=[tok,
                  pl.BlockSpec((1, 1, 6, D_MODEL), lambda b, i: (b, jnp.where(i < lat_tiles, 0, 1), 0, 0)),
                  const((D_MODEL, 2 * FFN_HIDDEN)), const((FFN_HIDDEN, D_MODEL)),
                  pl.BlockSpec((2, D_MODEL), lambda b, i: (0, 0))],
        out_specs=tok,
        out_shape=jax.ShapeDtypeStruct((bsz, t_all, D_MODEL), F32),
        compiler_params=_cparams(("parallel", "parallel"), V7X_VMEM_LIMIT_MB),
        name="ffn",
    )(x_all, mods, win, wout, ln)


def _layer(x_lat, x_ctx, ctx_blk, mods, p, seq, ctx_len, want_ctx):
    nt_lat = seq // ROW_TILE
    t_all = seq + ctx_len
    s5u, z, xbc_c, v, x1, x2, dt_raw = _inproj(x_lat, x_ctx, ctx_blk, mods, p["w_in"], p["ssd_conv_w"],
                                               p["ssd_conv_b"], p["hy_conv_w"], p["hy_conv_b"], nt_lat)
    s5y = _s5_scan(s5u, p["s5_ncat"], p["s5_tz"], p["s5_mcat"], p["s5_coef"], p["s5_dvec"],
                   seq // S5_CHUNK, ctx_len // S5_CHUNK)
    yssd = _ssd_scan(xbc_c, dt_raw, p["ssd_tri"], p["ssd_par"], p["ssd_expand"],
                     seq // SSD_CHUNK, ctx_len // SSD_CHUNK)
    hy_lat = _hyena_lat(v, x1, x2, p["hy_fa"], p["hy_ga"], p["hy_mfwd"], p["hy_minv"],
                        p["hy_kr"], p["hy_ki"], p["hy_bias"], seq)
    if want_ctx:
        hy_ctx = _hyena_ctx(v, x1, x2, *p["hy_ctx_mats"], *p["hy_ctx_taps"], p["hy_bias"], seq, ctx_len)
        rows = t_all
    else:
        hy_ctx, rows = hy_lat, seq
    x1_all = _mixout(x_lat, x_ctx, ctx_blk, mods, s5y, yssd, xbc_c, z, hy_lat, hy_ctx, p["s5_wglu"], p["s5_bglu"],
                     p["ssd_vec"], p["w_out"], p["ln1"], nt_lat, rows)
    return _ffn(x1_all, mods, p["ffn_w_in"], p["ffn_w_out"], p["ln2"], nt_lat)


def kernel(x, c, ctx, c_ctx, w_mod, b_mod, w_in, s5_lam_re, s5_lam_im, s5_log_step, s5_b_re, s5_b_im, s5_c_re, s5_c_im, s5_d, s5_w_glu, s5_b_glu, ssd_conv_w, ssd_conv_b, ssd_dt_bias, ssd_a_log, ssd_d, ssd_norm_w, hy_conv_w, hy_conv_b, hy_w1, hy_b1, hy_w2, hy_b2, hy_w3, hy_freq, hy_decay, hy_bias, w_out, ln1_g, ln1_b, ffn_w_in, ffn_w_out, ln2_g, ln2_b):
    bsz, seq, _ = x.shape
    ctx_len = ctx.shape[1]
    assert bsz == 8 and seq % ROW_TILE == 0 and ctx_len == ROW_TILE

    fa, ga, mfwd, minv = _hyena_lat_consts(seq)
    ctx_mats = _hyena_ctx_consts(ctx_len)
    tt = np.arange(SSD_CHUNK)
    tri = jnp.asarray(np.stack([tt[None, :] <= tt[:, None], tt[None, :] >= tt[:, None]]), BF16)
    expand = jnp.asarray(np.repeat(np.eye(128, SSD_HEADS), SSD_HEADDIM, axis=1)[:, :SSD_INNER], BF16)
    cvec = jnp.zeros((16, D_MODEL), F32).at[:bsz].set(c.astype(F32)).at[bsz].set(c_ctx.astype(F32))

    x_lat, x_ctx, ctx_blk = x.astype(F32), ctx.astype(F32), 0
    for l in range(DEPTH):
        want_ctx = l < DEPTH - 1
        mod16 = _modulation(cvec, w_mod[l].astype(F32), b_mod[l].astype(F32).reshape(1, -1))
        mods = jnp.stack([mod16[:bsz].reshape(bsz, 6, D_MODEL),
                          jnp.broadcast_to(mod16[bsz].reshape(1, 6, D_MODEL), (bsz, 6, D_MODEL))], axis=1)
        wl = w_in[l]
        w_p = jnp.concatenate([wl[:, 0:256], wl[:, 256:768], wl[:, 768:1792], wl[:, 1800:2568], wl[:, 1792:1800],
                               jnp.zeros((D_MODEL, 120), wl.dtype)], axis=1).astype(BF16)
        ncat, tz, mcat, coef = _s5_weights(s5_lam_re[l], s5_lam_im[l], s5_log_step[l], s5_b_re[l], s5_b_im[l],
                                           s5_c_re[l], s5_c_im[l])
        rep = lambda t: jnp.repeat(t.astype(F32), SSD_HEADDIM, axis=-1)
        par = jnp.zeros((2, 8, 128), F32)
        par = par.at[:, 0, :SSD_HEADS].set(ssd_dt_bias[l].astype(F32))
        par = par.at[:, 1, :SSD_HEADS].set(-jnp.exp(ssd_a_log[l].astype(F32)))
        hy_args = (hy_w1[l], hy_b1[l], hy_w2[l], hy_b2[l], hy_w3[l], hy_freq[l], hy_decay[l])
        kr, ki = _hyena_spectrum(*_hyena_filter_taps(seq, *hy_args), fa, mfwd)
        p = dict(
            w_in=w_p, s5_ncat=ncat, s5_tz=tz, s5_mcat=mcat, s5_coef=coef,
            ssd_conv_w=ssd_conv_w[l].astype(F32), ssd_conv_b=ssd_conv_b[l].astype(F32),
            ssd_tri=tri, ssd_par=par, ssd_expand=expand,
            hy_conv_w=hy_conv_w[l].astype(F32), hy_conv_b=hy_conv_b[l].astype(F32),
            hy_fa=fa, hy_ga=ga, hy_mfwd=mfwd, hy_minv=minv, hy_kr=kr, hy_ki=ki,
            hy_bias=hy_bias[l].astype(F32)[:, None, :],
            s5_wglu=s5_w_glu[l].astype(BF16),
            s5_bglu=s5_b_glu[l].astype(F32).reshape(1, S5_WIDTH),
            s5_dvec=jnp.tile(s5_d[l].astype(F32).reshape(S5_GROUPS, 1, S5_GROUP_CH), (1, 1, S5_CHUNK)),
            ssd_vec=jnp.stack([rep(ssd_d[l]), ssd_norm_w[l].astype(F32)]),
            w_out=w_out[l].astype(BF16), ln1=jnp.stack([ln1_g[l], ln1_b[l]]).astype(F32),
            ffn_w_in=ffn_w_in[l].astype(BF16), ffn_w_out=ffn_w_out[l].astype(BF16),
            ln2=jnp.stack([ln2_g[l], ln2_b[l]]).astype(F32),
        )
        if want_ctx:
            p.update(hy_ctx_mats=ctx_mats, hy_ctx_taps=_hyena_filter_taps(ctx_len, *hy_args))
        x_lat = _layer(x_lat, x_ctx, ctx_blk, mods, p, seq, ctx_len, want_ctx)
        x_ctx, ctx_blk = x_lat, seq // ROW_TILE
    return x_lat.astype(x.dtype)
```

```python
import functools
import math

import numpy as np
import jax
import jax.numpy as jnp
from jax import lax
from jax.experimental import pallas as pl
from jax.experimental.pallas import tpu as pltpu

F32 = jnp.float32
BF16 = jnp.bfloat16
HI = lax.Precision.HIGHEST

D_MODEL = 1024
DEPTH = 2
S5_WIDTH = 256
S5_GROUP_CH = 16
S5_GROUPS = 16
S5_STATE = 64
S5_MAX_RE = -1e-4
S5_CHUNK = 16
SSD_INNER = 512
SSD_HEADDIM = 64
SSD_HEADS = 8
SSD_GROUPS = 2
SSD_STATE = 128
SSD_CONV = 5
SSD_CHUNK = 128
SSD_CHUNKS_PER_STEP = 2
SSD_XBC = 1024
HY_WIDTH = 256
HY_ORDER = 2
HY_SHORT = 3
HY_BANDS = 16
HY_IN = 768
MIX_IN_PAD = 2688
FFN_HIDDEN = 2816
ALPHA = (2 * DEPTH) ** 0.25
LN_EPS = 1e-6

ROW_TILE = 256
FFN_ROW_TILE = 512
CONV_GROUP = 256
HALO = 16
FFT_N2 = 128
HY_A_PITCH = 72
HY_K1_PER_STEP = 11
HY_B_PITCH = 136
V7X_VMEM_LIMIT_MB = 56


def _cparams(sem, vmem_mb=None):
    kw = dict(dimension_semantics=sem)
    if vmem_mb is not None:
        kw["vmem_limit_bytes"] = vmem_mb * 2 ** 20
    return pltpu.CompilerParams(**kw)


def _standardise(x):
    mu = jnp.mean(x, axis=-1, keepdims=True)
    xc = x - mu
    var = jnp.mean(xc * xc, axis=-1, keepdims=True)
    return xc * lax.rsqrt(var + LN_EPS)


def _sigmoid(x):
    return 1.0 / (1.0 + jnp.exp(-x))


def _silu(x):
    return x * _sigmoid(x)


def _gelu_tanh(x):
    return 0.5 * x * (1.0 + jnp.tanh(0.7978845608028654 * (x + 0.044715 * (x * x * x))))


def _softplus(x):
    return jnp.maximum(x, 0.0) + jnp.log(1.0 + jnp.exp(-jnp.abs(x)))


def _bdot(a, b):
    return jnp.dot(a.astype(BF16), b.astype(BF16), preferred_element_type=F32)


def _split3(a):
    a1 = a.astype(BF16)
    r1 = a - a1.astype(F32)
    a2 = r1.astype(BF16)
    a3 = (r1 - a2.astype(F32)).astype(BF16)
    return a1, a2, a3


def _dot_sel_lhs(sel, a):
    a1, a2, a3 = _split3(a)
    d = functools.partial(jnp.dot, preferred_element_type=F32)
    return d(sel, a1) + d(sel, a2) + d(sel, a3)


def _dot_sel_rhs(a, sel):
    a1, a2, a3 = _split3(a)
    d = functools.partial(jnp.dot, preferred_element_type=F32)
    return d(a1, sel) + d(a2, sel) + d(a3, sel)


def _mod_kernel(c_ref, w_ref, b_ref, o_ref):
    ca = _silu(c_ref[...])
    c1, c2, c3 = _split3(ca)
    w1, w2, w3 = _split3(w_ref[...])
    d = functools.partial(jnp.dot, preferred_element_type=F32)
    acc = d(c1, w1) + d(c1, w2) + d(c2, w1) + d(c1, w3) + d(c2, w2) + d(c3, w1)
    o_ref[...] = acc + b_ref[...]


def _modulation(cvec, w, b, layer):
    n = w.shape[2]
    tn = 1536
    return pl.pallas_call(
        _mod_kernel,
        grid=(n // tn,),
        in_specs=[pl.BlockSpec((16, D_MODEL), lambda j: (0, 0)),
                  pl.BlockSpec((None, D_MODEL, tn), lambda j: (layer, 0, j)),
                  pl.BlockSpec((1, tn), lambda j: (0, j))],
        out_specs=pl.BlockSpec((16, tn), lambda j: (0, j)),
        out_shape=jax.ShapeDtypeStruct((16, n), F32),
        compiler_params=_cparams(("arbitrary",), 40),
        name="modulation",
    )(cvec, w, b)


def _rows_of(nt_lat, xl_ref, xc_ref):
    return jnp.where(pl.program_id(1) < nt_lat, xl_ref[0], xc_ref[0])


def _row_specs(nt_lat, ctx_blk, width):
    return [pl.BlockSpec((1, ROW_TILE, width), lambda b, i: (b, jnp.minimum(i, nt_lat - 1), 0)),
            pl.BlockSpec((1, ROW_TILE, width), lambda b, i: (b, ctx_blk, 0))]


def _dwconv_rows(ext_ref, w_ref, b_ref, wcol, taps, act, o_ref, ocol):
    pad = taps // 2
    rb = 64
    for c in range(0, CONV_GROUP, 128):
        wk = [w_ref[k:k + 1, wcol + c:wcol + c + 128] for k in range(taps)]
        bias = b_ref[0:1, wcol + c:wcol + c + 128]
        for r0 in range(0, ROW_TILE, rb):
            base = HALO - pad + r0
            acc = bias + wk[0] * ext_ref[base:base + rb, c:c + 128]
            for k in range(1, taps):
                acc = acc + wk[k] * ext_ref[base + k:base + k + rb, c:c + 128]
            if act:
                acc = _silu(acc)
            o_ref[0, r0:r0 + rb, ocol + c:ocol + c + 128] = acc.astype(o_ref.dtype)


def _inproj_kernel(nt_lat, xl_ref, xc_ref, xp_ref, xn_ref, m_ref, w_ref, w5_ref, b5_ref, w3_ref, b3_ref,
                   s5_ref, z_ref, xbc_ref, v_ref, x1_ref, x2_ref, dt_ref, *ext_refs):
    i = pl.program_id(1)
    m = m_ref[0, 0]
    mod = lambda x: (_standardise(x) * (1.0 + m[1:2]) + m[0:1]).astype(BF16)
    d = functools.partial(jnp.dot, preferred_element_type=F32)
    h = mod(_rows_of(nt_lat, xl_ref, xc_ref))
    hp, hn = mod(xp_ref[0]), mod(xn_ref[0])
    has_prev = jnp.logical_and(i > 0, i < nt_lat)
    has_next = i < nt_lat - 1
    n5 = SSD_XBC // CONV_GROUP
    hy_outs = (v_ref, x1_ref, x2_ref)

    def project(k):
        wc = w_ref[:, 768 + k * CONV_GROUP:768 + (k + 1) * CONV_GROUP]
        e = ext_refs[k]
        e[0:HALO, :] = jnp.where(has_prev, d(hp, wc), 0.0)
        e[HALO:HALO + ROW_TILE, :] = d(h, wc)
        e[HALO + ROW_TILE:2 * HALO + ROW_TILE, :] = jnp.where(has_next, d(hn, wc), 0.0)

    def conv(k):
        if k < n5:
            _dwconv_rows(ext_refs[k], w5_ref, b5_ref, k * CONV_GROUP, SSD_CONV, True, xbc_ref, k * CONV_GROUP)
        else:
            _dwconv_rows(ext_refs[k], w3_ref, b3_ref, (k - n5) * CONV_GROUP, HY_SHORT, False, hy_outs[k - n5], 0)

    ngroups = len(ext_refs)
    project(0)
    for k in range(ngroups):
        if k + 1 < ngroups:
            project(k + 1)
        conv(k)
    s5_ref[0] = d(h, w_ref[:, 0:256])
    z_ref[0] = d(h, w_ref[:, 256:768]).astype(z_ref.dtype)
    dt_ref[0] = d(h, w_ref[:, 2560:2688])


def _inproj(x_lat, x_ctx, ctx_blk, mods, w_p, layer, conv5_w, conv5_b, conv3_w, conv3_b, nt_lat):
    bsz = x_lat.shape[0]
    nt = nt_lat + 1
    t_all = nt * ROW_TILE
    rh = ROW_TILE // HALO
    last = nt_lat * rh - 1
    full = lambda shp: pl.BlockSpec(shp, lambda b, i: (0,) * len(shp))
    outs = ((256, F32), (512, BF16), (SSD_XBC, BF16), (HY_WIDTH, BF16), (HY_WIDTH, BF16), (HY_WIDTH, BF16), (128, F32))
    return pl.pallas_call(
        functools.partial(_inproj_kernel, nt_lat),
        grid=(bsz, nt),
        in_specs=_row_specs(nt_lat, ctx_blk, D_MODEL) + [
                  pl.BlockSpec((1, HALO, D_MODEL), lambda b, i: (b, jnp.clip(i * rh - 1, 0, last), 0)),
                  pl.BlockSpec((1, HALO, D_MODEL), lambda b, i: (b, jnp.clip((i + 1) * rh, 0, last), 0)),
                  pl.BlockSpec((1, 1, 6, D_MODEL), lambda b, i: (b, jnp.where(i < nt_lat, 0, 1), 0, 0)),
                  pl.BlockSpec((None, D_MODEL, MIX_IN_PAD), lambda b, i: (layer, 0, 0)), full((SSD_CONV, SSD_XBC)), full((1, SSD_XBC)),
                  full((HY_SHORT, HY_IN)), full((1, HY_IN))],
        out_specs=[pl.BlockSpec((1, ROW_TILE, w), lambda b, i: (b, i, 0)) for w, _ in outs],
        out_shape=[jax.ShapeDtypeStruct((bsz, t_all, w), dt) for w, dt in outs],
        scratch_shapes=[pltpu.VMEM((ROW_TILE + 2 * HALO, CONV_GROUP), F32)] * ((SSD_XBC + HY_IN) // CONV_GROUP),
        compiler_params=_cparams(("parallel", "parallel"), 40),
        name="inproj",
    )(x_lat, x_ctx, x_lat, x_lat, mods, w_p, conv5_w, conv5_b.reshape(1, -1), conv3_w, conv3_b.reshape(1, -1))


def _s5_kernel(n_lat, n_ctx, u0_ref, u1_ref, ncat_ref, tz_ref, mcat_ref, coef_ref, dvec_ref, y_ref,
               ut_ref, s_ref, hp_ref):
    q, gch, ng = S5_CHUNK, S5_GROUP_CH, S5_GROUPS
    nc = n_lat + n_ctx
    dot = functools.partial(jnp.dot, preferred_element_type=F32)
    u_refs = (u0_ref, u1_ref)
    gpl = 128 // gch
    for j in range(2):
        for s in range(q):
            rows = u_refs[j][0, pl.ds(s, nc, stride=q), :]
            for gg in range(gpl):
                ut_ref[gpl * j + gg, :, s * gch:(s + 1) * gch] = rows[:, gg * gch:(gg + 1) * gch]
    for g in range(ng):
        sg = dot(ut_ref[g].astype(BF16), ncat_ref[g])
        for k in range(4):
            s_ref[k, pl.ds(g, nc, stride=ng), :] = sg[:, k * 128:(k + 1) * 128]
    c1f, c2f, c1b, c2b = coef_ref[0], coef_ref[1], coef_ref[2], coef_ref[3]

    def step(cf, cb, carry):
        hf, hsf, hb, hsb = carry
        rf = pl.multiple_of(cf * ng, ng)
        rb = pl.multiple_of(cb * ng, ng)
        hp_ref[0, pl.ds(rf, ng), :] = hf
        hp_ref[1, pl.ds(rb, ng), :] = hb
        sf = s_ref[0, pl.ds(rf, ng), :]
        sb = s_ref[1, pl.ds(rb, ng), :]
        ssf = s_ref[2, pl.ds(rf, ng), :]
        ssb = s_ref[3, pl.ds(rb, ng), :]
        return (c1f * hf + c2f * hsf + sf, c1f * hsf - c2f * hf + ssf,
                c1b * hb + c2b * hsb + sb, c1b * hsb - c2b * hb + ssb)

    z = jnp.zeros((ng, 128), F32)
    carry = lax.fori_loop(0, n_ctx, lambda i, c: step(n_lat + i, n_lat + n_ctx - 1 - i, c), (z, z, z, z))
    lax.fori_loop(0, n_lat, lambda i, c: step(i, n_lat - 1 - i, c), carry)
    for g in range(ng):
        ug = ut_ref[g]
        hp = jnp.concatenate([hp_ref[0, pl.ds(g, nc, stride=ng), :], hp_ref[1, pl.ds(g, nc, stride=ng), :]], axis=1)
        ut_ref[g] = dot(ug.astype(BF16), tz_ref[g]) + dot(hp.astype(BF16), mcat_ref[g]) + ug * dvec_ref[g]
    for j in range(2):
        for s in range(q):
            rows = jnp.concatenate([ut_ref[gpl * j + gg, :, s * gch:(s + 1) * gch] for gg in range(gpl)], axis=1)
            y_ref[0, j, pl.ds(s, nc, stride=q), :] = rows


def _s5_scan(s5u, ncat, tz, mcat, coef, dvec, n_lat, n_ctx):
    bsz, t_all, _ = s5u.shape
    nc = n_lat + n_ctx
    ng = S5_GROUPS
    const = lambda shp: pl.BlockSpec(shp, lambda b: (0,) * len(shp), pipeline_mode=pl.Buffered(1))
    return pl.pallas_call(
        functools.partial(_s5_kernel, n_lat, n_ctx),
        grid=(bsz,),
        in_specs=[pl.BlockSpec((1, t_all, 128), lambda b: (b, 0, 0)),
                  pl.BlockSpec((1, t_all, 128), lambda b: (b, 0, 1)),
                  const((ng, 256, 512)), const((ng, 256, 256)), const((ng, 256, 256)),
                  const((4, ng, 128)), const((ng, 1, 256))],
        out_specs=pl.BlockSpec((1, 2, t_all, 128), lambda b: (b, 0, 0, 0)),
        out_shape=jax.ShapeDtypeStruct((bsz, 2, t_all, 128), F32),
        scratch_shapes=[pltpu.VMEM((ng, nc, 256), F32), pltpu.VMEM((4, nc * ng, 128), F32),
                        pltpu.VMEM((2, nc * ng, 128), F32)],
        compiler_params=_cparams(("parallel",), V7X_VMEM_LIMIT_MB),
        name="s5_scan",
    )(s5u, s5u, ncat, tz, mcat, coef, dvec)


def _s5_weights(lam_re, lam_im, log_step, b_re, b_im, c_re, c_im):
    q, ng, gch, ns = S5_CHUNK, S5_GROUPS, S5_GROUP_CH, S5_STATE
    a_re = jnp.minimum(lam_re.astype(F32), S5_MAX_RE)
    a_im = lam_im.astype(F32)
    step = jnp.exp(log_step.astype(F32))[..., None]
    taus = jnp.arange(q + 1, dtype=F32)[:, None, None, None]
    mag = jnp.exp(taus * (a_re * step))
    pr, pi = mag * jnp.cos(taus * (a_im * step)), mag * jnp.sin(taus * (a_im * step))
    nr, ni, den = pr[1] - 1.0, pi[1], a_re * a_re + a_im * a_im
    fr, fi = ((nr * a_re + ni * a_im) / den)[..., None], ((ni * a_re - nr * a_im) / den)[..., None]
    b_r, b_i = b_re.astype(F32), b_im.astype(F32)
    bb_r, bb_i = fr * b_r - fi * b_i, fr * b_i + fi * b_r
    c_r = jnp.swapaxes(c_re.astype(F32), -1, -2)
    c_i = jnp.swapaxes(c_im.astype(F32), -1, -2)
    cp_r, cp_i = c_r * pr[..., None] - c_i * pi[..., None], c_r * pi[..., None] + c_i * pr[..., None]
    pb_r, pb_i = pr[..., None] * bb_r - pi[..., None] * bb_i, pr[..., None] * bb_i + pi[..., None] * bb_r
    kern = (jnp.einsum('tdgph,dgpk->dgkth', cp_r[:q], bb_r, precision=HI)
            - jnp.einsum('tdgph,dgpk->dgkth', cp_i[:q], bb_i, precision=HI))
    kfull = jnp.concatenate([kern[1, :, :, :0:-1], kern[0, :, :, 0:1] + kern[1, :, :, 0:1], kern[0, :, :, 1:]],
                            axis=2).reshape(ng, gch, (2 * q - 1) * gch)
    tz = jnp.stack([kfull[:, :, (q - 1 - s) * gch:(2 * q - 1 - s) * gch] for s in range(q)],
                   axis=1).reshape(ng, q * gch, q * gch)
    rows_sk = lambda t: jnp.transpose(t, (1, 0, 3, 2)).reshape(ng, q * gch, ns)
    nf_r, nf_i = rows_sk(pb_r[q - 1::-1, 0]), rows_sk(pb_i[q - 1::-1, 0])
    nb_r, nb_i = rows_sk(pb_r[:q, 1]), rows_sk(pb_i[:q, 1])
    ncat = jnp.concatenate([nf_r, nf_i, nb_r, nb_i, nf_i, nf_r, nb_i, nb_r], axis=-1)
    cols_th = lambda t: jnp.transpose(t, (1, 2, 0, 3)).reshape(ng, ns, q * gch)
    mcat = jnp.concatenate([cols_th(cp_r[1:, 0]), -cols_th(cp_i[1:, 0]),
                            cols_th(cp_r[q:0:-1, 1]), -cols_th(cp_i[q:0:-1, 1])], axis=1)
    lr, li = pr[q], pi[q]
    coef = jnp.stack([jnp.concatenate([lr[0], lr[0]], axis=-1), jnp.concatenate([-li[0], li[0]], axis=-1),
                      jnp.concatenate([lr[1], lr[1]], axis=-1), jnp.concatenate([-li[1], li[1]], axis=-1)])
    return ncat.astype(BF16), tz.astype(BF16), mcat.astype(BF16), coef.astype(F32)


def _ssd_kernel(cps, xbc_ref, dt_ref, tri_ref, par_ref, exp_ref, y_ref, st_ref):
    direction = pl.program_id(1)

    @pl.when(pl.program_id(2) == 0)
    def _():
        st_ref[...] = jnp.zeros_like(st_ref)

    tri = tri_ref[0]
    mask = tri > 0
    lane = lax.broadcasted_iota(jnp.int32, (SSD_CHUNK, 128), 1)
    lo = lane < SSD_HEADDIM
    zero_b = jnp.zeros((SSD_CHUNK, 128), BF16)
    head_of_lane = lax.broadcasted_iota(jnp.int32, (1, SSD_INNER), 1) // SSD_HEADDIM
    expand = lambda t: jnp.dot(t.astype(BF16), exp_ref[...], preferred_element_type=F32)
    state = [st_ref[:, g * 256:(g + 1) * 256] for g in range(SSD_GROUPS)]
    for c in range(cps):
        sub = jnp.where(direction == 0, c, cps - 1 - c)
        rows = pl.ds(pl.multiple_of(sub * SSD_CHUNK, SSD_CHUNK), SSD_CHUNK)
        xs = xbc_ref[0, rows, 0:512].astype(F32)
        dt_c = _softplus(dt_ref[0, rows, :] + par_ref[0, 0:1, :])
        a_c = par_ref[0, 1:2, :] * dt_c
        cs = _dot_sel_lhs(tri, a_c)
        tot = jnp.sum(a_c, axis=0, keepdims=True)
        cs_t = cs.T
        xdt = xs * expand(dt_c)
        xdt_b = xdt.astype(BF16)
        xd_end = (xdt * expand(jnp.exp(tot - cs))).astype(BF16)
        e_cs = expand(jnp.exp(cs))
        tot_full = jnp.zeros((1, SSD_INNER), F32)
        for head in range(SSD_HEADS):
            tot_full = jnp.where(head_of_lane == head, tot[:, head:head + 1], tot_full)
        e_tot = jnp.exp(tot_full)
        for g in range(SSD_GROUPS):
            bm_b = xbc_ref[0, rows, 512 + g * 128:512 + (g + 1) * 128]
            cm_b = xbc_ref[0, rows, 768 + g * 128:768 + (g + 1) * 128]
            cb = lax.dot_general(cm_b, bm_b, (((1,), (1,)), ((), ())), preferred_element_type=F32)
            y_off = jnp.dot(cm_b, state[g].astype(BF16), preferred_element_type=F32) * e_cs[:, g * 256:(g + 1) * 256]
            for j in range(2):
                c0 = g * 256 + j * 128
                x_pair = xdt_b[:, c0:c0 + 128]
                acc = y_off[:, j * 128:(j + 1) * 128]
                for hh in range(2):
                    head = c0 // SSD_HEADDIM + hh
                    decay = jnp.exp(jnp.where(mask, cs[:, head:head + 1] - cs_t[head:head + 1, :], -1e30))
                    gm = (cb * decay).astype(BF16)
                    xh = jnp.where(lo if hh == 0 else jnp.logical_not(lo), x_pair, zero_b)
                    acc = acc + jnp.dot(gm, xh, preferred_element_type=F32)
                y_ref[0, 0, rows, c0:c0 + 128] = acc.astype(y_ref.dtype)
            state[g] = (state[g] * e_tot[:, g * 256:(g + 1) * 256]
                        + jnp.dot(bm_b.astype(F32).T.astype(BF16), xd_end[:, g * 256:(g + 1) * 256],
                                  preferred_element_type=F32))
    for g in range(SSD_GROUPS):
        st_ref[:, g * 256:(g + 1) * 256] = state[g]


def _ssd_scan(xbc_c, dt_raw, tri, par, expand, n_lat, n_ctx):
    bsz, t_all, _ = xbc_c.shape
    cps = SSD_CHUNKS_PER_STEP
    assert n_lat % cps == 0 and n_ctx % cps == 0
    n_lat, n_ctx = n_lat // cps, n_ctx // cps
    nc = n_lat + n_ctx
    rows = cps * SSD_CHUNK

    def blk(d, i):
        fwd = jnp.where(i < n_ctx, n_lat + i, i - n_ctx)
        return jnp.where(d == 0, fwd, nc - 1 - i)

    return pl.pallas_call(
        functools.partial(_ssd_kernel, cps),
        grid=(bsz, 2, nc),
        in_specs=[pl.BlockSpec((1, rows, SSD_XBC), lambda b, d, i: (b, blk(d, i), 0)),
                  pl.BlockSpec((1, rows, 128), lambda b, d, i: (b, blk(d, i), 0)),
                  pl.BlockSpec((1, SSD_CHUNK, SSD_CHUNK), lambda b, d, i: (d, 0, 0)),
                  pl.BlockSpec((1, 8, 128), lambda b, d, i: (d, 0, 0)),
                  pl.BlockSpec((128, SSD_INNER), lambda b, d, i: (0, 0))],
        out_specs=pl.BlockSpec((1, 1, rows, SSD_INNER), lambda b, d, i: (b, d, blk(d, i), 0)),
        out_shape=jax.ShapeDtypeStruct((bsz, 2, t_all, SSD_INNER), BF16),
        scratch_shapes=[pltpu.VMEM((SSD_STATE, SSD_INNER), F32)],
        compiler_params=_cparams(("parallel", "parallel", "arbitrary")),
        name="ssd_scan",
    )(xbc_c, dt_raw, tri, par, expand)


def _hy_lat_kernel(k1n, kgrp, v_ref, x1_ref, x2_ref, fa_ref, ga_ref, mf_ref, mi_ref, kr_ref, ki_ref, bias_ref,
                   o_ref, z_ref, a2_ref, b2_ref):
    order = pl.program_id(1)
    ph = pl.program_id(2)
    n2 = FFT_N2
    kp, h1 = fa_ref.shape
    ks = ga_ref.shape[1]
    seq = h1 * n2
    ngroups = k1n // kgrp
    dot = functools.partial(jnp.dot, preferred_element_type=F32)

    @pl.when(jnp.logical_and(order == 0, ph == 0))
    def _():
        for j in range(2):
            z_ref[j] = v_ref[0, :, j * 128:(j + 1) * 128].astype(F32)
            b2_ref[j, 2 * k1n * HY_B_PITCH:ks * HY_B_PITCH, :] = jnp.zeros(((ks - 2 * k1n) * HY_B_PITCH, 128), F32)

    @pl.when(ph == 0)
    def _():
        fa = fa_ref[...]

        def body(i, carry):
            u = jnp.concatenate([z_ref[j, pl.ds(i, h1, stride=n2), :] for j in range(2)], axis=1)
            r = dot(fa, u.astype(BF16))
            row = pl.multiple_of(i * HY_A_PITCH, 8)
            for j in range(2):
                a2_ref[j, pl.ds(row, kp), :] = r[:, j * 128:(j + 1) * 128]
            return carry

        lax.fori_loop(0, n2, body, 0, unroll=8)

    @pl.when(jnp.logical_and(ph >= 1, ph <= ngroups))
    def _():
        for t in range(kgrp):
            k1 = (ph - 1) * kgrp + t
            are = jnp.concatenate([a2_ref[j, pl.ds(2 * k1, n2, stride=HY_A_PITCH), :] for j in range(2)], axis=1)
            aim = jnp.concatenate([a2_ref[j, pl.ds(2 * k1 + 1, n2, stride=HY_A_PITCH), :] for j in range(2)], axis=1)
            x = dot(mf_ref[k1], jnp.concatenate([are, aim], axis=0).astype(BF16))
            xr, xi = x[:n2], x[n2:]
            kr, ki = kr_ref[t].astype(F32), ki_ref[t].astype(F32)
            y = jnp.concatenate([xr * kr - xi * ki, xr * ki + xi * kr], axis=0).astype(BF16)
            bq = dot(mi_ref[k1], y)
            row = pl.multiple_of(2 * k1 * HY_B_PITCH, 8)
            for j in range(2):
                b2_ref[j, pl.ds(row, n2), :] = bq[:n2, j * 128:(j + 1) * 128]
                b2_ref[j, pl.ds(row + HY_B_PITCH, n2), :] = bq[n2:, j * 128:(j + 1) * 128]

    @pl.when(ph == ngroups + 1)
    def _():
        ga = ga_ref[...]

        def body(i, carry):
            bs = jnp.concatenate([b2_ref[j, pl.ds(i, ks, stride=HY_B_PITCH), :] for j in range(2)], axis=1)
            y = dot(ga, bs.astype(BF16))
            for j in range(2):
                o_ref[0, j, pl.ds(i, h1, stride=n2), :] = y[:, j * 128:(j + 1) * 128]
            return carry

        lax.fori_loop(0, n2, body, 0, unroll=8)
        rb = min(256, seq)

        def gate(i, carry):
            r0 = pl.multiple_of(i * rb, 8)
            for j in range(2):
                conv = o_ref[0, j, pl.ds(r0, rb), :] + z_ref[j, pl.ds(r0, rb), :] * bias_ref[0, :, j * 128:(j + 1) * 128]
                x1 = x1_ref[0, pl.ds(r0, rb), j * 128:(j + 1) * 128].astype(F32)
                x2 = x2_ref[0, pl.ds(r0, rb), j * 128:(j + 1) * 128].astype(F32)
                z_ref[j, pl.ds(r0, rb), :] = x1 * conv
                o_ref[0, j, pl.ds(r0, rb), :] = x2 * conv
            return carry

        lax.fori_loop(0, seq // rb, gate, 0)


def _hy_spec_kernel(k1n, ts_ref, td_ref, fa_ref, mf_ref, kr_ref, ki_ref, a2_ref):
    n2 = FFT_N2
    kp, h1 = fa_ref.shape
    dot = functools.partial(jnp.dot, preferred_element_type=F32)
    fa = fa_ref[...]

    def body(i, carry):
        u = jnp.concatenate([ts_ref[pl.ds(i, h1, stride=n2), :], td_ref[pl.ds(i, h1, stride=n2), :]], axis=1)
        r = dot(fa, u.astype(BF16))
        row = pl.multiple_of(i * HY_A_PITCH, 8)
        for j in range(2):
            a2_ref[j, pl.ds(row, kp), :] = r[:, j * 128:(j + 1) * 128]
        return carry

    lax.fori_loop(0, n2, body, 0, unroll=8)

    def per_k1(k1, carry):
        are = jnp.concatenate([a2_ref[j, pl.ds(2 * k1, n2, stride=HY_A_PITCH), :] for j in range(2)], axis=1)
        aim = jnp.concatenate([a2_ref[j, pl.ds(2 * k1 + 1, n2, stride=HY_A_PITCH), :] for j in range(2)], axis=1)
        x = dot(mf_ref[k1], jnp.concatenate([are, aim], axis=0).astype(BF16))
        kr_ref[k1] = x[:n2, 0:128].astype(kr_ref.dtype)
        ki_ref[k1] = x[n2:, 128:256].astype(ki_ref.dtype)
        return carry

    lax.fori_loop(0, k1n, per_k1, 0)


def _hyena_spectrum(tsum, tdiff, fa, mfwd):
    seq, lanes = tsum.shape
    n2 = FFT_N2
    kp, h1 = fa.shape
    k1n = mfwd.shape[0]
    taps = pl.BlockSpec((seq, 128), lambda j: (0, j))
    out = pl.BlockSpec((k1n, n2, 128), lambda j: (0, 0, j))
    return pl.pallas_call(
        functools.partial(_hy_spec_kernel, k1n),
        grid=(lanes // 128,),
        in_specs=[taps, taps, pl.BlockSpec((kp, h1), lambda j: (0, 0)),
                  pl.BlockSpec((k1n, 2 * n2, 2 * n2), lambda j: (0, 0, 0))],
        out_specs=[out, out],
        out_shape=[jax.ShapeDtypeStruct((k1n, n2, lanes), BF16)] * 2,
        scratch_shapes=[pltpu.VMEM((2, n2 * HY_A_PITCH, 128), F32)],
        compiler_params=_cparams(("parallel",), 40),
        name="hyena_spectrum",
    )(tsum, tdiff, fa, mfwd)


def _hyena_lat(v, x1, x2, fa, ga, mfwd, minv, kr, ki, bias, seq):
    bsz, t_all, w = v.shape
    n2 = FFT_N2
    kp, h1 = fa.shape
    ks = ga.shape[1]
    k1n = mfwd.shape[0]
    kgrp = next(g for g in (HY_K1_PER_STEP, 3, 1) if k1n % g == 0)
    ngroups = k1n // kgrp
    tok = pl.BlockSpec((1, seq, w), lambda b, o, p: (b, 0, 0), pipeline_mode=pl.Buffered(1))
    full = lambda shp: pl.BlockSpec(shp, lambda b, o, p: (0,) * len(shp))
    grp = lambda p: jnp.clip(p - 1, 0, ngroups - 1)
    mat = pl.BlockSpec((k1n, 2 * n2, 2 * n2), lambda b, o, p: (0, 0, 0), pipeline_mode=pl.Buffered(1))
    spec = pl.BlockSpec((kgrp, n2, w), lambda b, o, p: (grp(p), 0, o))
    return pl.pallas_call(
        functools.partial(_hy_lat_kernel, k1n, kgrp),
        grid=(bsz, HY_ORDER, ngroups + 2),
        in_specs=[tok, tok, tok, full((kp, h1)), full((h1, ks)), mat, mat, spec, spec,
                  pl.BlockSpec((1, 1, w), lambda b, o, p: (o, 0, 0))],
        out_specs=pl.BlockSpec((1, 2, seq, 128), lambda b, o, p: (b, 0, 0, 0)),
        out_shape=jax.ShapeDtypeStruct((bsz, 2, seq, 128), F32),
        scratch_shapes=[pltpu.VMEM((2, seq, 128), F32),
                        pltpu.VMEM((2, n2 * HY_A_PITCH, 128), F32),
                        pltpu.VMEM((2, ks * HY_B_PITCH, 128), F32)],
        compiler_params=_cparams(("parallel", "arbitrary", "arbitrary"), V7X_VMEM_LIMIT_MB),
        name="hyena_lat",
    )(v, x1, x2, fa, ga, mfwd, minv, kr, ki, bias)


def _hyena_lat_consts(seq):
    n = 2 * seq
    n2 = FFT_N2
    n1 = n // n2
    h1 = n1 // 2
    k1n = n1 // 2 + 1
    kp = -(-2 * k1n // 8) * 8
    ks = -(-2 * k1n // 16) * 16
    assert kp <= HY_A_PITCH and n2 <= HY_B_PITCH
    k1 = np.arange(k1n)
    m1 = np.arange(h1)
    ang = 2.0 * np.pi * np.outer(k1, m1) / n1
    fa = np.zeros((kp, h1))
    fa[0:2 * k1n:2] = np.cos(ang)
    fa[1:2 * k1n:2] = -np.sin(ang)
    wgt = np.where((k1 == 0) | (k1 == n1 // 2), 1.0, 2.0) / n
    ga = np.zeros((h1, ks))
    ga[:, 0:2 * k1n:2] = (np.cos(ang) * wgt[:, None]).T
    ga[:, 1:2 * k1n:2] = (-np.sin(ang) * wgt[:, None]).T
    k2 = np.arange(n2)
    m2 = np.arange(n2)
    kk = k1[:, None, None] + n1 * k2[None, :, None]
    th = 2.0 * np.pi * ((kk * m2[None, None, :]) % n) / n
    mc, ms = np.cos(th), np.sin(th)
    mfwd = np.concatenate([np.concatenate([mc, ms], axis=2), np.concatenate([-ms, mc], axis=2)], axis=1)
    mct, mst = np.transpose(mc, (0, 2, 1)), np.transpose(ms, (0, 2, 1))
    minv = np.concatenate([np.concatenate([mct, -mst], axis=2), np.concatenate([mst, mct], axis=2)], axis=1)
    return tuple(jnp.asarray(t, BF16) for t in (fa, ga, mfwd, minv))


def _hy_ctx_kernel(v_ref, x1_ref, x2_ref, fc_ref, fs_ref, gc_ref, gs_ref, ts_ref, td_ref, bias_ref, z_ref):
    d = functools.partial(jnp.dot, preferred_element_type=F32)
    w = v_ref.shape[2]
    kr_all = d(fc_ref[...], ts_ref[...].astype(BF16))
    ki_all = d(fs_ref[...], td_ref[...].astype(BF16))

    def conv(u, o):
        ub = u.astype(BF16)
        cr, ci = d(fc_ref[...], ub), d(fs_ref[...], ub)
        kr, ki = kr_all[:, o * w:(o + 1) * w], ki_all[:, o * w:(o + 1) * w]
        pr = (cr * kr - ci * ki).astype(BF16)
        pi = (cr * ki + ci * kr).astype(BF16)
        return d(gc_ref[...], pr) + d(gs_ref[...], pi) + u * bias_ref[o]

    z = x1_ref[0].astype(F32) * conv(v_ref[0].astype(F32), 0)
    z = x2_ref[0].astype(F32) * conv(z, 1)
    for j in range(2):
        z_ref[0, j] = z[:, j * 128:(j + 1) * 128]


def _hyena_ctx(v, x1, x2, fc, fs, gc, gs, tsum, tdiff, bias, seq, ctx_len):
    bsz, _, w = v.shape
    blk = seq // ctx_len
    kpad = fc.shape[0]
    tok = pl.BlockSpec((1, ctx_len, w), lambda b: (b, blk, 0))
    full = lambda shp: pl.BlockSpec(shp, lambda b: (0,) * len(shp))
    return pl.pallas_call(
        _hy_ctx_kernel, grid=(bsz,),
        in_specs=[tok, tok, tok, full((kpad, ctx_len)), full((kpad, ctx_len)), full((ctx_len, kpad)),
                  full((ctx_len, kpad)), full((ctx_len, HY_ORDER * w)), full((ctx_len, HY_ORDER * w)), full((2, 1, w))],
        out_specs=pl.BlockSpec((1, 2, ctx_len, 128), lambda b: (b, 0, 0, 0)),
        out_shape=jax.ShapeDtypeStruct((bsz, 2, ctx_len, 128), F32),
        compiler_params=_cparams(("parallel",)),
        name="hyena_ctx",
    )(v, x1, x2, fc, fs, gc, gs, tsum, tdiff, bias)


def _hyena_ctx_consts(ctx_len):
    n = 2 * ctx_len
    nk = ctx_len + 1
    kpad = -(-nk // 128) * 128
    k = np.arange(nk)
    m = np.arange(ctx_len)
    ang = 2.0 * np.pi * np.outer(k, m) / n
    fc = np.zeros((kpad, ctx_len))
    fs = np.zeros((kpad, ctx_len))
    fc[:nk] = np.cos(ang)
    fs[:nk] = -np.sin(ang)
    wgt = np.where((k == 0) | (k == ctx_len), 1.0, 2.0) / n
    gc = np.zeros((ctx_len, kpad))
    gs = np.zeros((ctx_len, kpad))
    gc[:, :nk] = (np.cos(ang) * wgt[:, None]).T
    gs[:, :nk] = (-np.sin(ang) * wgt[:, None]).T
    return tuple(jnp.asarray(t, BF16) for t in (fc, fs, gc, gs))


def _hyena_filter_taps(length, w1, b1, w2, b2, w3, freq, decay):
    pos = jnp.arange(length, dtype=F32)
    t = pos / max(length - 1, 1)
    bands = jnp.linspace(1e-4, HY_BANDS - 1, HY_BANDS, dtype=F32)
    ang = (2.0 * math.pi / length) * pos[:, None] * bands
    feats = jnp.concatenate([t[:, None], jnp.cos(ang), -jnp.sin(ang)], axis=-1)
    freq = freq.astype(F32)
    mm = functools.partial(jnp.matmul, precision=HI)
    hid = jnp.sin(freq * (mm(feats, w1.astype(F32)) + b1.astype(F32)))
    hid = jnp.sin(freq * (mm(hid, w2.astype(F32)) + b2.astype(F32)))
    h = mm(hid, w3.astype(F32)) * jnp.exp(-t[:, None] * jnp.abs(decay.astype(F32)))
    half = HY_ORDER * HY_WIDTH
    h_fwd, h_bwd = h[:, :half], h[:, half:]
    l1 = (jnp.abs(h_fwd[0] + h_bwd[0]) + jnp.sum(jnp.abs(h_fwd[1:]), axis=0) + jnp.sum(jnp.abs(h_bwd[1:]), axis=0))
    return (h_fwd + h_bwd) / l1, (h_fwd - h_bwd) / l1


def _mixout_kernel(nt_lat, xl_ref, xc_ref, m_ref, s5y_ref, yf_ref, yb_ref, xs_ref, z_ref, hyl_ref, hyc_ref,
                   wglu_ref, bglu_ref, vec512_ref, wout_ref, ln_ref, o_ref):
    m = m_ref[0, 0]
    is_lat = pl.program_id(1) < nt_lat
    hy = [jnp.where(is_lat, hyl_ref[0, j], hyc_ref[0, j]).astype(BF16) for j in range(2)]
    d = functools.partial(jnp.dot, preferred_element_type=F32)
    y5 = _gelu_tanh(jnp.concatenate([s5y_ref[0, 0], s5y_ref[0, 1]], axis=1))
    y5 = y5 * _sigmoid(d(y5.astype(BF16), wglu_ref[...]) + bglu_ref[...])
    ys = yf_ref[0, 0].astype(F32) + yb_ref[0, 0].astype(F32) + vec512_ref[0:1, :] * xs_ref[0].astype(F32)
    gsd = ys * _silu(z_ref[0].astype(F32))
    gsd = gsd * lax.rsqrt(jnp.mean(gsd * gsd, axis=-1, keepdims=True) + LN_EPS) * vec512_ref[1:2, :]
    mix = (d(y5.astype(BF16), wout_ref[0:256, :]) + d(gsd.astype(BF16), wout_ref[256:768, :])
           + d(hy[0], wout_ref[768:896, :]) + d(hy[1], wout_ref[896:1024, :]))
    r = ALPHA * _rows_of(nt_lat, xl_ref, xc_ref) + m[2:3] * mix
    o_ref[0] = _standardise(r) * ln_ref[0:1, :] + ln_ref[1:2, :]


def _mixout(x_lat, x_ctx, ctx_blk, mods, s5y, yssd, xbc_c, z, hy_lat, hy_ctx, wglu, bglu, vec512, wout, layer, ln,
            nt_lat, rows):
    bsz = x_lat.shape[0]
    nt = rows // ROW_TILE
    tok = lambda w: pl.BlockSpec((1, ROW_TILE, w), lambda b, i: (b, i, 0))
    halves = pl.BlockSpec((1, 2, ROW_TILE, 128), lambda b, i: (b, 0, i, 0))
    full = lambda shp: pl.BlockSpec(shp, lambda b, i: (0,) * len(shp))
    return pl.pallas_call(
        functools.partial(_mixout_kernel, nt_lat), grid=(bsz, nt),
        in_specs=_row_specs(nt_lat, ctx_blk, D_MODEL) + [
                  pl.BlockSpec((1, 1, 6, D_MODEL), lambda b, i: (b, jnp.where(i < nt_lat, 0, 1), 0, 0)),
                  halves,
                  pl.BlockSpec((1, 1, ROW_TILE, SSD_INNER), lambda b, i: (b, 0, i, 0)),
                  pl.BlockSpec((1, 1, ROW_TILE, SSD_INNER), lambda b, i: (b, 1, i, 0)),
                  tok(SSD_INNER), tok(SSD_INNER),
                  pl.BlockSpec((1, 2, ROW_TILE, 128), lambda b, i: (b, 0, jnp.minimum(i, nt_lat - 1), 0)),
                  pl.BlockSpec((1, 2, ROW_TILE, 128), lambda b, i: (b, 0, 0, 0)),
                  full((256, 256)), full((1, 256)), full((2, 512)),
                  pl.BlockSpec((None, D_MODEL, D_MODEL), lambda b, i: (layer, 0, 0)), full((2, D_MODEL))],
        out_specs=tok(D_MODEL),
        out_shape=jax.ShapeDtypeStruct((bsz, rows, D_MODEL), F32),
        compiler_params=_cparams(("parallel", "parallel"), 40),
        name="mixout",
    )(x_lat, x_ctx, mods, s5y, yssd, yssd, xbc_c, z, hy_lat, hy_ctx, wglu, bglu, vec512, wout, ln)


def _ffn_kernel(x_ref, m_ref, win_ref, wout_ref, ln_ref, o_ref):
    m = m_ref[0, 0]
    x = x_ref[0]
    d = functools.partial(jnp.dot, preferred_element_type=F32)
    h = (_standardise(x) * (1.0 + m[4:5]) + m[3:4]).astype(BF16)
    gate = d(h, win_ref[:, 0:FFN_HIDDEN])
    up = d(h, win_ref[:, FFN_HIDDEN:2 * FFN_HIDDEN])
    act = (_silu(gate) * up).astype(BF16)
    r = ALPHA * x + m[5:6] * d(act, wout_ref[...])
    o_ref[0] = _standardise(r) * ln_ref[0:1, :] + ln_ref[1:2, :]


def _ffn(x_all, mods, win, wout, layer, ln, nt_lat):
    bsz, t_all, _ = x_all.shape
    tile = FFN_ROW_TILE if t_all % FFN_ROW_TILE == 0 else ROW_TILE
    nt = t_all // tile
    lat_tiles = nt_lat * ROW_TILE // tile
    tok = pl.BlockSpec((1, tile, D_MODEL), lambda b, i: (b, i, 0))
    const = lambda shp: pl.BlockSpec((None,) + shp, lambda b, i: (layer, 0, 0), pipeline_mode=pl.Buffered(1))
    return pl.pallas_call(
        _ffn_kernel, grid=(bsz, nt),
        in_specs=[tok,
                  pl.BlockSpec((1, 1, 6, D_MODEL), lambda b, i: (b, jnp.where(i < lat_tiles, 0, 1), 0, 0)),
                  const((D_MODEL, 2 * FFN_HIDDEN)), const((FFN_HIDDEN, D_MODEL)),
                  pl.BlockSpec((2, D_MODEL), lambda b, i: (0, 0))],
        out_specs=tok,
        out_shape=jax.ShapeDtypeStruct((bsz, t_all, D_MODEL), F32),
        compiler_params=_cparams(("parallel", "parallel"), V7X_VMEM_LIMIT_MB),
        name="ffn",
    )(x_all, mods, win, wout, ln)


def _layer(x_lat, x_ctx, ctx_blk, mods, p, seq, ctx_len, want_ctx):
    nt_lat = seq // ROW_TILE
    t_all = seq + ctx_len
    layer = p["layer"]
    s5u, z, xbc_c, v, x1, x2, dt_raw = _inproj(x_lat, x_ctx, ctx_blk, mods, p["w_in"], layer, p["ssd_conv_w"],
                                               p["ssd_conv_b"], p["hy_conv_w"], p["hy_conv_b"], nt_lat)
    s5y = _s5_scan(s5u, p["s5_ncat"], p["s5_tz"], p["s5_mcat"], p["s5_coef"], p["s5_dvec"],
                   seq // S5_CHUNK, ctx_len // S5_CHUNK)
    yssd = _ssd_scan(xbc_c, dt_raw, p["ssd_tri"], p["ssd_par"], p["ssd_expand"],
                     seq // SSD_CHUNK, ctx_len // SSD_CHUNK)
    hy_lat = _hyena_lat(v, x1, x2, p["hy_fa"], p["hy_ga"], p["hy_mfwd"], p["hy_minv"],
                        p["hy_kr"], p["hy_ki"], p["hy_bias"], seq)
    if want_ctx:
        hy_ctx = _hyena_ctx(v, x1, x2, *p["hy_ctx_mats"], *p["hy_ctx_taps"], p["hy_bias"], seq, ctx_len)
        rows = t_all
    else:
        hy_ctx, rows = hy_lat, seq
    x1_all = _mixout(x_lat, x_ctx, ctx_blk, mods, s5y, yssd, xbc_c, z, hy_lat, hy_ctx, p["s5_wglu"], p["s5_bglu"],
                     p["ssd_vec"], p["w_out"], layer, p["ln1"], nt_lat, rows)
    return _ffn(x1_all, mods, p["ffn_w_in"], p["ffn_w_out"], layer, p["ln2"], nt_lat)


def kernel(x, c, ctx, c_ctx, w_mod, b_mod, w_in, s5_lam_re, s5_lam_im, s5_log_step, s5_b_re, s5_b_im, s5_c_re, s5_c_im, s5_d, s5_w_glu, s5_b_glu, ssd_conv_w, ssd_conv_b, ssd_dt_bias, ssd_a_log, ssd_d, ssd_norm_w, hy_conv_w, hy_conv_b, hy_w1, hy_b1, hy_w2, hy_b2, hy_w3, hy_freq, hy_decay, hy_bias, w_out, ln1_g, ln1_b, ffn_w_in, ffn_w_out, ln2_g, ln2_b):
    bsz, seq, _ = x.shape
    ctx_len = ctx.shape[1]
    assert bsz == 8 and seq % ROW_TILE == 0 and ctx_len == ROW_TILE

    fa, ga, mfwd, minv = _hyena_lat_consts(seq)
    ctx_mats = _hyena_ctx_consts(ctx_len)
    tt = np.arange(SSD_CHUNK)
    tri = jnp.asarray(np.stack([tt[None, :] <= tt[:, None], tt[None, :] >= tt[:, None]]), BF16)
    expand = jnp.asarray(np.repeat(np.eye(128, SSD_HEADS), SSD_HEADDIM, axis=1)[:, :SSD_INNER], BF16)
    cvec = jnp.zeros((16, D_MODEL), F32).at[:bsz].set(c.astype(F32)).at[bsz].set(c_ctx.astype(F32))

    w_p = jnp.concatenate([w_in[..., 0:1792], w_in[..., 1800:2568], w_in[..., 1792:1800],
                           jnp.zeros((DEPTH, D_MODEL, MIX_IN_PAD - 2568), w_in.dtype)], axis=-1).astype(BF16)
    w_out_b, ffn_in_b, ffn_out_b = w_out.astype(BF16), ffn_w_in.astype(BF16), ffn_w_out.astype(BF16)

    x_lat, x_ctx, ctx_blk = x.astype(F32), ctx.astype(F32), 0
    for l in range(DEPTH):
        want_ctx = l < DEPTH - 1
        mod16 = _modulation(cvec, w_mod.astype(F32), b_mod[l].astype(F32).reshape(1, -1), l)
        mods = jnp.stack([mod16[:bsz].reshape(bsz, 6, D_MODEL),
                          jnp.broadcast_to(mod16[bsz].reshape(1, 6, D_MODEL), (bsz, 6, D_MODEL))], axis=1)
        ncat, tz, mcat, coef = _s5_weights(s5_lam_re[l], s5_lam_im[l], s5_log_step[l], s5_b_re[l], s5_b_im[l],
                                           s5_c_re[l], s5_c_im[l])
        rep = lambda t: jnp.repeat(t.astype(F32), SSD_HEADDIM, axis=-1)
        par = jnp.zeros((2, 8, 128), F32)
        par = par.at[:, 0, :SSD_HEADS].set(ssd_dt_bias[l].astype(F32))
        par = par.at[:, 1, :SSD_HEADS].set(-jnp.exp(ssd_a_log[l].astype(F32)))
        hy_args = (hy_w1[l], hy_b1[l], hy_w2[l], hy_b2[l], hy_w3[l], hy_freq[l], hy_decay[l])
        kr, ki = _hyena_spectrum(*_hyena_filter_taps(seq, *hy_args), fa, mfwd)
        p = dict(
            layer=l, w_in=w_p, s5_ncat=ncat, s5_tz=tz, s5_mcat=mcat, s5_coef=coef,
            ssd_conv_w=ssd_conv_w[l].astype(F32), ssd_conv_b=ssd_conv_b[l].astype(F32),
            ssd_tri=tri, ssd_par=par, ssd_expand=expand,
            hy_conv_w=hy_conv_w[l].astype(F32), hy_conv_b=hy_conv_b[l].astype(F32),
            hy_fa=fa, hy_ga=ga, hy_mfwd=mfwd, hy_minv=minv, hy_kr=kr, hy_ki=ki,
            hy_bias=hy_bias[l].astype(F32)[:, None, :],
            s5_wglu=s5_w_glu[l].astype(BF16),
            s5_bglu=s5_b_glu[l].astype(F32).reshape(1, S5_WIDTH),
            s5_dvec=jnp.tile(s5_d[l].astype(F32).reshape(S5_GROUPS, 1, S5_GROUP_CH), (1, 1, S5_CHUNK)),
            ssd_vec=jnp.stack([rep(ssd_d[l]), ssd_norm_w[l].astype(F32)]),
            w_out=w_out_b, ln1=jnp.stack([ln1_g[l], ln1_b[l]]).astype(F32),
            ffn_w_in=ffn_in_b, ffn_w_out=ffn_out_b,
            ln2=jnp.stack([ln2_g[l], ln2_b[l]]).astype(F32),
        )
        if want_ctx:
            p.update(hy_ctx_mats=ctx_mats, hy_ctx_taps=_hyena_filter_taps(ctx_len, *hy_args))
        x_lat = _layer(x_lat, x_ctx, ctx_blk, mods, p, seq, ctx_len, want_ctx)
        x_ctx, ctx_blk = x_lat, seq // ROW_TILE
    return x_lat.astype(x.dtype)
```

```python
import functools
import math

import numpy as np
import jax
import jax.numpy as jnp
from jax import lax
from jax.experimental import pallas as pl
from jax.experimental.pallas import tpu as pltpu

F32 = jnp.float32
BF16 = jnp.bfloat16
HI = lax.Precision.HIGHEST

D_MODEL = 1024
DEPTH = 2
S5_WIDTH = 256
S5_GROUP_CH = 16
S5_GROUPS = 16
S5_STATE = 64
S5_MAX_RE = -1e-4
S5_CHUNK = 16
SSD_INNER = 512
SSD_HEADDIM = 64
SSD_HEADS = 8
SSD_GROUPS = 2
SSD_STATE = 128
SSD_CONV = 5
SSD_CHUNK = 128
SSD_CHUNKS_PER_STEP = 2
SSD_XBC = 1024
HY_WIDTH = 256
HY_ORDER = 2
HY_SHORT = 3
HY_BANDS = 16
HY_IN = 768
MIX_IN_PAD = 2688
FFN_HIDDEN = 2816
ALPHA = (2 * DEPTH) ** 0.25
LN_EPS = 1e-6

ROW_TILE = 256
CONV_GROUP = 256
HALO = 16
FFT_N2 = 128
HY_A_PITCH = 72
HY_K1_PER_STEP = 11
HY_B_PITCH = 136
V7X_VMEM_LIMIT_MB = 56


def _cparams(sem, vmem_mb=None):
    kw = dict(dimension_semantics=sem)
    if vmem_mb is not None:
        kw["vmem_limit_bytes"] = vmem_mb * 2 ** 20
    return pltpu.CompilerParams(**kw)


def _standardise(x):
    mu = jnp.mean(x, axis=-1, keepdims=True)
    xc = x - mu
    var = jnp.mean(xc * xc, axis=-1, keepdims=True)
    return xc * lax.rsqrt(var + LN_EPS)


def _sigmoid(x):
    return 1.0 / (1.0 + jnp.exp(-x))


def _silu(x):
    return x * _sigmoid(x)


def _gelu_tanh(x):
    return 0.5 * x * (1.0 + jnp.tanh(0.7978845608028654 * (x + 0.044715 * (x * x * x))))


def _softplus(x):
    return jnp.maximum(x, 0.0) + jnp.log(1.0 + jnp.exp(-jnp.abs(x)))


def _bdot(a, b):
    return jnp.dot(a.astype(BF16), b.astype(BF16), preferred_element_type=F32)


def _split3(a):
    a1 = a.astype(BF16)
    r1 = a - a1.astype(F32)
    a2 = r1.astype(BF16)
    a3 = (r1 - a2.astype(F32)).astype(BF16)
    return a1, a2, a3


def _dot_sel_lhs(sel, a):
    a1, a2, a3 = _split3(a)
    d = functools.partial(jnp.dot, preferred_element_type=F32)
    return d(sel, a1) + d(sel, a2) + d(sel, a3)


def _dot_sel_rhs(a, sel):
    a1, a2, a3 = _split3(a)
    d = functools.partial(jnp.dot, preferred_element_type=F32)
    return d(a1, sel) + d(a2, sel) + d(a3, sel)


def _mod_kernel(c_ref, w_ref, b_ref, o_ref):
    ca = _silu(c_ref[...])
    c1, c2, c3 = _split3(ca)
    w1, w2, w3 = _split3(w_ref[...])
    d = functools.partial(jnp.dot, preferred_element_type=F32)
    acc = d(c1, w1) + d(c1, w2) + d(c2, w1) + d(c1, w3) + d(c2, w2) + d(c3, w1)
    o_ref[...] = acc + b_ref[...]


def _modulation(cvec, w, b, layer):
    n = w.shape[2]
    tn = 1536
    return pl.pallas_call(
        _mod_kernel,
        grid=(n // tn,),
        in_specs=[pl.BlockSpec((16, D_MODEL), lambda j: (0, 0)),
                  pl.BlockSpec((None, D_MODEL, tn), lambda j: (layer, 0, j)),
                  pl.BlockSpec((1, tn), lambda j: (0, j))],
        out_specs=pl.BlockSpec((16, tn), lambda j: (0, j)),
        out_shape=jax.ShapeDtypeStruct((16, n), F32),
        compiler_params=_cparams(("arbitrary",), 40),
        name="modulation",
    )(cvec, w, b)


def _rows_of(nt_lat, xl_ref, xc_ref):
    return jnp.where(pl.program_id(1) < nt_lat, xl_ref[0], xc_ref[0])


def _row_specs(nt_lat, ctx_blk, width):
    return [pl.BlockSpec((1, ROW_TILE, width), lambda b, i: (b, jnp.minimum(i, nt_lat - 1), 0)),
            pl.BlockSpec((1, ROW_TILE, width), lambda b, i: (b, ctx_blk, 0))]


def _dwconv_rows(ext_ref, w_ref, b_ref, wcol, taps, act, o_ref, ocol):
    pad = taps // 2
    rb = 64
    for c in range(0, CONV_GROUP, 128):
        wk = [w_ref[k:k + 1, wcol + c:wcol + c + 128] for k in range(taps)]
        bias = b_ref[0:1, wcol + c:wcol + c + 128]
        for r0 in range(0, ROW_TILE, rb):
            base = HALO - pad + r0
            acc = bias + wk[0] * ext_ref[base:base + rb, c:c + 128]
            for k in range(1, taps):
                acc = acc + wk[k] * ext_ref[base + k:base + k + rb, c:c + 128]
            if act:
                acc = _silu(acc)
            o_ref[0, r0:r0 + rb, ocol + c:ocol + c + 128] = acc.astype(o_ref.dtype)


def _inproj_kernel(nt_lat, xl_ref, xc_ref, xp_ref, xn_ref, m_ref, w_ref, w5_ref, b5_ref, w3_ref, b3_ref,
                   s5_ref, z_ref, xbc_ref, v_ref, x1_ref, x2_ref, dt_ref, *ext_refs):
    i = pl.program_id(1)
    m = m_ref[0, 0]
    mod = lambda x: (_standardise(x) * (1.0 + m[1:2]) + m[0:1]).astype(BF16)
    d = functools.partial(jnp.dot, preferred_element_type=F32)
    h = mod(_rows_of(nt_lat, xl_ref, xc_ref))
    hp, hn = mod(xp_ref[0]), mod(xn_ref[0])
    has_prev = jnp.logical_and(i > 0, i < nt_lat)
    has_next = i < nt_lat - 1
    n5 = SSD_XBC // CONV_GROUP
    hy_outs = (v_ref, x1_ref, x2_ref)

    def project(k):
        wc = w_ref[:, 768 + k * CONV_GROUP:768 + (k + 1) * CONV_GROUP]
        e = ext_refs[k]
        e[0:HALO, :] = jnp.where(has_prev, d(hp, wc), 0.0)
        e[HALO:HALO + ROW_TILE, :] = d(h, wc)
        e[HALO + ROW_TILE:2 * HALO + ROW_TILE, :] = jnp.where(has_next, d(hn, wc), 0.0)

    def conv(k):
        if k < n5:
            _dwconv_rows(ext_refs[k], w5_ref, b5_ref, k * CONV_GROUP, SSD_CONV, True, xbc_ref, k * CONV_GROUP)
        else:
            _dwconv_rows(ext_refs[k], w3_ref, b3_ref, (k - n5) * CONV_GROUP, HY_SHORT, False, hy_outs[k - n5], 0)

    ngroups = len(ext_refs)
    project(0)
    for k in range(ngroups):
        if k + 1 < ngroups:
            project(k + 1)
        conv(k)
    s5_ref[0] = d(h, w_ref[:, 0:256])
    z_ref[0] = d(h, w_ref[:, 256:768]).astype(z_ref.dtype)
    dt_ref[0] = d(h, w_ref[:, 2560:2688])


def _inproj(x_lat, x_ctx, ctx_blk, mods, w_p, layer, conv5_w, conv5_b, conv3_w, conv3_b, nt_lat):
    bsz = x_lat.shape[0]
    nt = nt_lat + 1
    t_all = nt * ROW_TILE
    rh = ROW_TILE // HALO
    last = nt_lat * rh - 1
    full = lambda shp: pl.BlockSpec(shp, lambda b, i: (0,) * len(shp))
    outs = ((256, F32), (512, BF16), (SSD_XBC, BF16), (HY_WIDTH, BF16), (HY_WIDTH, BF16), (HY_WIDTH, BF16), (128, F32))
    return pl.pallas_call(
        functools.partial(_inproj_kernel, nt_lat),
        grid=(bsz, nt),
        in_specs=_row_specs(nt_lat, ctx_blk, D_MODEL) + [
                  pl.BlockSpec((1, HALO, D_MODEL), lambda b, i: (b, jnp.clip(i * rh - 1, 0, last), 0)),
                  pl.BlockSpec((1, HALO, D_MODEL), lambda b, i: (b, jnp.clip((i + 1) * rh, 0, last), 0)),
                  pl.BlockSpec((1, 1, 6, D_MODEL), lambda b, i: (b, jnp.where(i < nt_lat, 0, 1), 0, 0)),
                  pl.BlockSpec((None, D_MODEL, MIX_IN_PAD), lambda b, i: (layer, 0, 0)), full((SSD_CONV, SSD_XBC)), full((1, SSD_XBC)),
                  full((HY_SHORT, HY_IN)), full((1, HY_IN))],
        out_specs=[pl.BlockSpec((1, ROW_TILE, w), lambda b, i: (b, i, 0)) for w, _ in outs],
        out_shape=[jax.ShapeDtypeStruct((bsz, t_all, w), dt) for w, dt in outs],
        scratch_shapes=[pltpu.VMEM((ROW_TILE + 2 * HALO, CONV_GROUP), F32)] * ((SSD_XBC + HY_IN) // CONV_GROUP),
        compiler_params=_cparams(("parallel", "parallel"), 40),
        name="inproj",
    )(x_lat, x_ctx, x_lat, x_lat, mods, w_p, conv5_w, conv5_b.reshape(1, -1), conv3_w, conv3_b.reshape(1, -1))


def _s5_kernel(n_lat, n_ctx, u0_ref, u1_ref, ncat_ref, tz_ref, mcat_ref, coef_ref, dvec_ref, y_ref,
               ut_ref, s_ref, hp_ref):
    q, gch, ng = S5_CHUNK, S5_GROUP_CH, S5_GROUPS
    nc = n_lat + n_ctx
    dot = functools.partial(jnp.dot, preferred_element_type=F32)
    u_refs = (u0_ref, u1_ref)
    gpl = 128 // gch
    for j in range(2):
        for s in range(q):
            rows = u_refs[j][0, pl.ds(s, nc, stride=q), :]
            for gg in range(gpl):
                ut_ref[gpl * j + gg, :, s * gch:(s + 1) * gch] = rows[:, gg * gch:(gg + 1) * gch]
    for g in range(ng):
        sg = dot(ut_ref[g].astype(BF16), ncat_ref[g])
        for k in range(4):
            s_ref[k, pl.ds(g, nc, stride=ng), :] = sg[:, k * 128:(k + 1) * 128]
    c1f, c2f, c1b, c2b = coef_ref[0], coef_ref[1], coef_ref[2], coef_ref[3]

    def step(cf, cb, carry):
        hf, hsf, hb, hsb = carry
        rf = pl.multiple_of(cf * ng, ng)
        rb = pl.multiple_of(cb * ng, ng)
        hp_ref[0, pl.ds(rf, ng), :] = hf
        hp_ref[1, pl.ds(rb, ng), :] = hb
        sf = s_ref[0, pl.ds(rf, ng), :]
        sb = s_ref[1, pl.ds(rb, ng), :]
        ssf = s_ref[2, pl.ds(rf, ng), :]
        ssb = s_ref[3, pl.ds(rb, ng), :]
        return (c1f * hf + c2f * hsf + sf, c1f * hsf - c2f * hf + ssf,
                c1b * hb + c2b * hsb + sb, c1b * hsb - c2b * hb + ssb)

    z = jnp.zeros((ng, 128), F32)
    carry = lax.fori_loop(0, n_ctx, lambda i, c: step(n_lat + i, n_lat + n_ctx - 1 - i, c), (z, z, z, z))
    lax.fori_loop(0, n_lat, lambda i, c: step(i, n_lat - 1 - i, c), carry)
    for g in range(ng):
        ug = ut_ref[g]
        hp = jnp.concatenate([hp_ref[0, pl.ds(g, nc, stride=ng), :], hp_ref[1, pl.ds(g, nc, stride=ng), :]], axis=1)
        ut_ref[g] = dot(ug.astype(BF16), tz_ref[g]) + dot(hp.astype(BF16), mcat_ref[g]) + ug * dvec_ref[g]
    for j in range(2):
        for s in range(q):
            rows = jnp.concatenate([ut_ref[gpl * j + gg, :, s * gch:(s + 1) * gch] for gg in range(gpl)], axis=1)
            y_ref[0, j, pl.ds(s, nc, stride=q), :] = rows


def _s5_scan(s5u, ncat, tz, mcat, coef, dvec, n_lat, n_ctx):
    bsz, t_all, _ = s5u.shape
    nc = n_lat + n_ctx
    ng = S5_GROUPS
    const = lambda shp: pl.BlockSpec(shp, lambda b: (0,) * len(shp), pipeline_mode=pl.Buffered(1))
    return pl.pallas_call(
        functools.partial(_s5_kernel, n_lat, n_ctx),
        grid=(bsz,),
        in_specs=[pl.BlockSpec((1, t_all, 128), lambda b: (b, 0, 0)),
                  pl.BlockSpec((1, t_all, 128), lambda b: (b, 0, 1)),
                  const((ng, 256, 512)), const((ng, 256, 256)), const((ng, 256, 256)),
                  const((4, ng, 128)), const((ng, 1, 256))],
        out_specs=pl.BlockSpec((1, 2, t_all, 128), lambda b: (b, 0, 0, 0)),
        out_shape=jax.ShapeDtypeStruct((bsz, 2, t_all, 128), F32),
        scratch_shapes=[pltpu.VMEM((ng, nc, 256), F32), pltpu.VMEM((4, nc * ng, 128), F32),
                        pltpu.VMEM((2, nc * ng, 128), F32)],
        compiler_params=_cparams(("parallel",), V7X_VMEM_LIMIT_MB),
        name="s5_scan",
    )(s5u, s5u, ncat, tz, mcat, coef, dvec)


def _s5_weights(lam_re, lam_im, log_step, b_re, b_im, c_re, c_im):
    q, ng, gch = S5_CHUNK, S5_GROUPS, S5_GROUP_CH
    a_re = jnp.minimum(lam_re.astype(F32), S5_MAX_RE)
    a_im = lam_im.astype(F32)
    step = jnp.exp(log_step.astype(F32))[..., None]
    taus = jnp.arange(q + 1, dtype=F32)[:, None, None, None]
    mag = jnp.exp(taus * (a_re * step))
    pr, pi = mag * jnp.cos(taus * (a_im * step)), mag * jnp.sin(taus * (a_im * step))
    nr, ni, den = pr[1] - 1.0, pi[1], a_re * a_re + a_im * a_im
    fr, fi = ((nr * a_re + ni * a_im) / den)[..., None], ((ni * a_re - nr * a_im) / den)[..., None]
    b_r, b_i = b_re.astype(F32), b_im.astype(F32)
    bb_r, bb_i = fr * b_r - fi * b_i, fr * b_i + fi * b_r
    c_r = jnp.swapaxes(c_re.astype(F32), -1, -2)
    c_i = jnp.swapaxes(c_im.astype(F32), -1, -2)
    pw_r, pw_i = jnp.transpose(pr, (1, 2, 3, 0)), jnp.transpose(pi, (1, 2, 3, 0))

    def lagged(v_r, v_i, d, lags):
        e_r, e_i = jnp.repeat(pw_r[d][..., lags], gch, axis=-1), jnp.repeat(pw_i[d][..., lags], gch, axis=-1)
        n = e_r.shape[-1] // gch
        t_r, t_i = jnp.tile(v_r[d], (1, 1, n)), jnp.tile(v_i[d], (1, 1, n))
        return t_r * e_r - t_i * e_i, t_r * e_i + t_i * e_r

    def lag_kernels(d, lags):
        cp_r, cp_i = lagged(c_r, c_i, d, lags)
        return (jnp.einsum('gpk,gpn->gkn', bb_r[d], cp_r, precision=HI)
                - jnp.einsum('gpk,gpn->gkn', bb_i[d], cp_i, precision=HI))

    k_fwd = lag_kernels(0, slice(0, q))
    k_bwd = lag_kernels(1, slice(q - 1, None, -1))
    edge = (q - 1) * gch
    kfull = jnp.concatenate([k_bwd[..., :edge], k_bwd[..., edge:] + k_fwd[..., :gch], k_fwd[..., gch:]], axis=-1)
    tz = jnp.stack([kfull[:, :, (q - 1 - s) * gch:(2 * q - 1 - s) * gch] for s in range(q)],
                   axis=1).reshape(ng, q * gch, q * gch)
    nf_r, nf_i = (jnp.swapaxes(t, -1, -2) for t in lagged(bb_r, bb_i, 0, slice(q - 1, None, -1)))
    nb_r, nb_i = (jnp.swapaxes(t, -1, -2) for t in lagged(bb_r, bb_i, 1, slice(0, q)))
    ncat = jnp.concatenate([nf_r, nf_i, nb_r, nb_i, nf_i, nf_r, nb_i, nb_r], axis=-1)
    mf_r, mf_i = lagged(c_r, c_i, 0, slice(1, None))
    mb_r, mb_i = lagged(c_r, c_i, 1, slice(q, 0, -1))
    mcat = jnp.concatenate([mf_r, -mf_i, mb_r, -mb_i], axis=1)
    lr, li = pr[q], pi[q]
    coef = jnp.stack([jnp.concatenate([lr[0], lr[0]], axis=-1), jnp.concatenate([-li[0], li[0]], axis=-1),
                      jnp.concatenate([lr[1], lr[1]], axis=-1), jnp.concatenate([-li[1], li[1]], axis=-1)])
    return ncat.astype(BF16), tz.astype(BF16), mcat.astype(BF16), coef.astype(F32)


def _ssd_kernel(cps, xbc_ref, dt_ref, tri_ref, par_ref, exp_ref, y_ref, st_ref):
    direction = pl.program_id(1)

    @pl.when(pl.program_id(2) == 0)
    def _():
        st_ref[...] = jnp.zeros_like(st_ref)

    tri = tri_ref[0]
    mask = tri > 0
    lane = lax.broadcasted_iota(jnp.int32, (SSD_CHUNK, 128), 1)
    lo = lane < SSD_HEADDIM
    zero_b = jnp.zeros((SSD_CHUNK, 128), BF16)
    head_of_lane = lax.broadcasted_iota(jnp.int32, (1, SSD_INNER), 1) // SSD_HEADDIM
    expand = lambda t: jnp.dot(t.astype(BF16), exp_ref[...], preferred_element_type=F32)
    state = [st_ref[:, g * 256:(g + 1) * 256] for g in range(SSD_GROUPS)]
    for c in range(cps):
        sub = jnp.where(direction == 0, c, cps - 1 - c)
        rows = pl.ds(pl.multiple_of(sub * SSD_CHUNK, SSD_CHUNK), SSD_CHUNK)
        xs = xbc_ref[0, rows, 0:512].astype(F32)
        dt_c = _softplus(dt_ref[0, rows, :] + par_ref[0, 0:1, :])
        a_c = par_ref[0, 1:2, :] * dt_c
        cs = _dot_sel_lhs(tri, a_c)
        tot = jnp.sum(a_c, axis=0, keepdims=True)
        cs_t = cs.T
        xdt = xs * expand(dt_c)
        xdt_b = xdt.astype(BF16)
        xd_end = (xdt * expand(jnp.exp(tot - cs))).astype(BF16)
        e_cs = expand(jnp.exp(cs))
        tot_full = jnp.zeros((1, SSD_INNER), F32)
        for head in range(SSD_HEADS):
            tot_full = jnp.where(head_of_lane == head, tot[:, head:head + 1], tot_full)
        e_tot = jnp.exp(tot_full)
        for g in range(SSD_GROUPS):
            bm_b = xbc_ref[0, rows, 512 + g * 128:512 + (g + 1) * 128]
            cm_b = xbc_ref[0, rows, 768 + g * 128:768 + (g + 1) * 128]
            cb = lax.dot_general(cm_b, bm_b, (((1,), (1,)), ((), ())), preferred_element_type=F32)
            y_off = jnp.dot(cm_b, state[g].astype(BF16), preferred_element_type=F32) * e_cs[:, g * 256:(g + 1) * 256]
            for j in range(2):
                c0 = g * 256 + j * 128
                x_pair = xdt_b[:, c0:c0 + 128]
                acc = y_off[:, j * 128:(j + 1) * 128]
                for hh in range(2):
                    head = c0 // SSD_HEADDIM + hh
                    decay = jnp.exp(jnp.where(mask, cs[:, head:head + 1] - cs_t[head:head + 1, :], -1e30))
                    gm = (cb * decay).astype(BF16)
                    xh = jnp.where(lo if hh == 0 else jnp.logical_not(lo), x_pair, zero_b)
                    acc = acc + jnp.dot(gm, xh, preferred_element_type=F32)
                y_ref[0, 0, rows, c0:c0 + 128] = acc.astype(y_ref.dtype)
            state[g] = (state[g] * e_tot[:, g * 256:(g + 1) * 256]
                        + jnp.dot(bm_b.astype(F32).T.astype(BF16), xd_end[:, g * 256:(g + 1) * 256],
                                  preferred_element_type=F32))
    for g in range(SSD_GROUPS):
        st_ref[:, g * 256:(g + 1) * 256] = state[g]


def _ssd_scan(xbc_c, dt_raw, tri, par, expand, n_lat, n_ctx):
    bsz, t_all, _ = xbc_c.shape
    cps = SSD_CHUNKS_PER_STEP
    assert n_lat % cps == 0 and n_ctx % cps == 0
    n_lat, n_ctx = n_lat // cps, n_ctx // cps
    nc = n_lat + n_ctx
    rows = cps * SSD_CHUNK

    def blk(d, i):
        fwd = jnp.where(i < n_ctx, n_lat + i, i - n_ctx)
        return jnp.where(d == 0, fwd, nc - 1 - i)

    return pl.pallas_call(
        functools.partial(_ssd_kernel, cps),
        grid=(bsz, 2, nc),
        in_specs=[pl.BlockSpec((1, rows, SSD_XBC), lambda b, d, i: (b, blk(d, i), 0)),
                  pl.BlockSpec((1, rows, 128), lambda b, d, i: (b, blk(d, i), 0)),
                  pl.BlockSpec((1, SSD_CHUNK, SSD_CHUNK), lambda b, d, i: (d, 0, 0)),
                  pl.BlockSpec((1, 8, 128), lambda b, d, i: (d, 0, 0)),
                  pl.BlockSpec((128, SSD_INNER), lambda b, d, i: (0, 0))],
        out_specs=pl.BlockSpec((1, 1, rows, SSD_INNER), lambda b, d, i: (b, d, blk(d, i), 0)),
        out_shape=jax.ShapeDtypeStruct((bsz, 2, t_all, SSD_INNER), BF16),
        scratch_shapes=[pltpu.VMEM((SSD_STATE, SSD_INNER), F32)],
        compiler_params=_cparams(("parallel", "parallel", "arbitrary")),
        name="ssd_scan",
    )(xbc_c, dt_raw, tri, par, expand)


def _hy_lat_kernel(k1n, kgrp, v_ref, x1_ref, x2_ref, fa_ref, ga_ref, mf_ref, mi_ref, kr_ref, ki_ref, bias_ref,
                   o_ref, z_ref, a2_ref, b2_ref):
    order = pl.program_id(1)
    ph = pl.program_id(2)
    n2 = FFT_N2
    kp, h1 = fa_ref.shape
    ks = ga_ref.shape[1]
    seq = h1 * n2
    ngroups = k1n // kgrp
    dot = functools.partial(jnp.dot, preferred_element_type=F32)

    @pl.when(jnp.logical_and(order == 0, ph == 0))
    def _():
        for j in range(2):
            z_ref[j] = v_ref[0, :, j * 128:(j + 1) * 128].astype(F32)
            b2_ref[j, 2 * k1n * HY_B_PITCH:ks * HY_B_PITCH, :] = jnp.zeros(((ks - 2 * k1n) * HY_B_PITCH, 128), F32)

    @pl.when(ph == 0)
    def _():
        fa = fa_ref[...]

        def body(i, carry):
            u = jnp.concatenate([z_ref[j, pl.ds(i, h1, stride=n2), :] for j in range(2)], axis=1)
            r = dot(fa, u.astype(BF16))
            row = pl.multiple_of(i * HY_A_PITCH, 8)
            for j in range(2):
                a2_ref[j, pl.ds(row, kp), :] = r[:, j * 128:(j + 1) * 128]
            return carry

        lax.fori_loop(0, n2, body, 0, unroll=8)

    @pl.when(jnp.logical_and(ph >= 1, ph <= ngroups))
    def _():
        for t in range(kgrp):
            k1 = (ph - 1) * kgrp + t
            are = jnp.concatenate([a2_ref[j, pl.ds(2 * k1, n2, stride=HY_A_PITCH), :] for j in range(2)], axis=1)
            aim = jnp.concatenate([a2_ref[j, pl.ds(2 * k1 + 1, n2, stride=HY_A_PITCH), :] for j in range(2)], axis=1)
            x = dot(mf_ref[k1], jnp.concatenate([are, aim], axis=0).astype(BF16))
            xr, xi = x[:n2], x[n2:]
            kr, ki = kr_ref[t].astype(F32), ki_ref[t].astype(F32)
            y = jnp.concatenate([xr * kr - xi * ki, xr * ki + xi * kr], axis=0).astype(BF16)
            bq = dot(mi_ref[k1], y)
            row = pl.multiple_of(2 * k1 * HY_B_PITCH, 8)
            for j in range(2):
                b2_ref[j, pl.ds(row, n2), :] = bq[:n2, j * 128:(j + 1) * 128]
                b2_ref[j, pl.ds(row + HY_B_PITCH, n2), :] = bq[n2:, j * 128:(j + 1) * 128]

    @pl.when(ph == ngroups + 1)
    def _():
        ga = ga_ref[...]

        def body(i, carry):
            bs = jnp.concatenate([b2_ref[j, pl.ds(i, ks, stride=HY_B_PITCH), :] for j in range(2)], axis=1)
            y = dot(ga, bs.astype(BF16))
            for j in range(2):
                o_ref[0, j, pl.ds(i, h1, stride=n2), :] = y[:, j * 128:(j + 1) * 128]
            return carry

        lax.fori_loop(0, n2, body, 0, unroll=8)
        rb = min(256, seq)

        def gate(i, carry):
            r0 = pl.multiple_of(i * rb, 8)
            for j in range(2):
                conv = o_ref[0, j, pl.ds(r0, rb), :] + z_ref[j, pl.ds(r0, rb), :] * bias_ref[0, :, j * 128:(j + 1) * 128]
                x1 = x1_ref[0, pl.ds(r0, rb), j * 128:(j + 1) * 128].astype(F32)
                x2 = x2_ref[0, pl.ds(r0, rb), j * 128:(j + 1) * 128].astype(F32)
                z_ref[j, pl.ds(r0, rb), :] = x1 * conv
                o_ref[0, j, pl.ds(r0, rb), :] = x2 * conv
            return carry

        lax.fori_loop(0, seq // rb, gate, 0)


def _hy_spec_kernel(k1n, ts_ref, td_ref, fa_ref, mf_ref, kr_ref, ki_ref, a2_ref):
    n2 = FFT_N2
    kp, h1 = fa_ref.shape
    dot = functools.partial(jnp.dot, preferred_element_type=F32)
    fa = fa_ref[...]

    def body(i, carry):
        u = jnp.concatenate([ts_ref[pl.ds(i, h1, stride=n2), :], td_ref[pl.ds(i, h1, stride=n2), :]], axis=1)
        r = dot(fa, u.astype(BF16))
        row = pl.multiple_of(i * HY_A_PITCH, 8)
        for j in range(2):
            a2_ref[j, pl.ds(row, kp), :] = r[:, j * 128:(j + 1) * 128]
        return carry

    lax.fori_loop(0, n2, body, 0, unroll=8)

    def per_k1(k1, carry):
        are = jnp.concatenate([a2_ref[j, pl.ds(2 * k1, n2, stride=HY_A_PITCH), :] for j in range(2)], axis=1)
        aim = jnp.concatenate([a2_ref[j, pl.ds(2 * k1 + 1, n2, stride=HY_A_PITCH), :] for j in range(2)], axis=1)
        x = dot(mf_ref[k1], jnp.concatenate([are, aim], axis=0).astype(BF16))
        kr_ref[k1] = x[:n2, 0:128].astype(kr_ref.dtype)
        ki_ref[k1] = x[n2:, 128:256].astype(ki_ref.dtype)
        return carry

    lax.fori_loop(0, k1n, per_k1, 0)


def _hyena_spectrum(tsum, tdiff, fa, mfwd):
    seq, lanes = tsum.shape
    n2 = FFT_N2
    kp, h1 = fa.shape
    k1n = mfwd.shape[0]
    taps = pl.BlockSpec((seq, 128), lambda j: (0, j))
    out = pl.BlockSpec((k1n, n2, 128), lambda j: (0, 0, j))
    return pl.pallas_call(
        functools.partial(_hy_spec_kernel, k1n),
        grid=(lanes // 128,),
        in_specs=[taps, taps, pl.BlockSpec((kp, h1), lambda j: (0, 0)),
                  pl.BlockSpec((k1n, 2 * n2, 2 * n2), lambda j: (0, 0, 0))],
        out_specs=[out, out],
        out_shape=[jax.ShapeDtypeStruct((k1n, n2, lanes), BF16)] * 2,
        scratch_shapes=[pltpu.VMEM((2, n2 * HY_A_PITCH, 128), F32)],
        compiler_params=_cparams(("parallel",), 40),
        name="hyena_spectrum",
    )(tsum, tdiff, fa, mfwd)


def _hyena_lat(v, x1, x2, fa, ga, mfwd, minv, kr, ki, bias, seq):
    bsz, t_all, w = v.shape
    n2 = FFT_N2
    kp, h1 = fa.shape
    ks = ga.shape[1]
    k1n = mfwd.shape[0]
    kgrp = next(g for g in (HY_K1_PER_STEP, 3, 1) if k1n % g == 0)
    ngroups = k1n // kgrp
    tok = pl.BlockSpec((1, seq, w), lambda b, o, p: (b, 0, 0), pipeline_mode=pl.Buffered(1))
    full = lambda shp: pl.BlockSpec(shp, lambda b, o, p: (0,) * len(shp))
    grp = lambda p: jnp.clip(p - 1, 0, ngroups - 1)
    mat = pl.BlockSpec((k1n, 2 * n2, 2 * n2), lambda b, o, p: (0, 0, 0), pipeline_mode=pl.Buffered(1))
    spec = pl.BlockSpec((kgrp, n2, w), lambda b, o, p: (grp(p), 0, o))
    return pl.pallas_call(
        functools.partial(_hy_lat_kernel, k1n, kgrp),
        grid=(bsz, HY_ORDER, ngroups + 2),
        in_specs=[tok, tok, tok, full((kp, h1)), full((h1, ks)), mat, mat, spec, spec,
                  pl.BlockSpec((1, 1, w), lambda b, o, p: (o, 0, 0))],
        out_specs=pl.BlockSpec((1, 2, seq, 128), lambda b, o, p: (b, 0, 0, 0)),
        out_shape=jax.ShapeDtypeStruct((bsz, 2, seq, 128), F32),
        scratch_shapes=[pltpu.VMEM((2, seq, 128), F32),
                        pltpu.VMEM((2, n2 * HY_A_PITCH, 128), F32),
                        pltpu.VMEM((2, ks * HY_B_PITCH, 128), F32)],
        compiler_params=_cparams(("parallel", "arbitrary", "arbitrary"), V7X_VMEM_LIMIT_MB),
        name="hyena_lat",
    )(v, x1, x2, fa, ga, mfwd, minv, kr, ki, bias)


def _hyena_lat_consts(seq):
    n = 2 * seq
    n2 = FFT_N2
    n1 = n // n2
    h1 = n1 // 2
    k1n = n1 // 2 + 1
    kp = -(-2 * k1n // 8) * 8
    ks = -(-2 * k1n // 16) * 16
    assert kp <= HY_A_PITCH and n2 <= HY_B_PITCH
    k1 = np.arange(k1n)
    m1 = np.arange(h1)
    ang = 2.0 * np.pi * np.outer(k1, m1) / n1
    fa = np.zeros((kp, h1))
    fa[0:2 * k1n:2] = np.cos(ang)
    fa[1:2 * k1n:2] = -np.sin(ang)
    wgt = np.where((k1 == 0) | (k1 == n1 // 2), 1.0, 2.0) / n
    ga = np.zeros((h1, ks))
    ga[:, 0:2 * k1n:2] = (np.cos(ang) * wgt[:, None]).T
    ga[:, 1:2 * k1n:2] = (-np.sin(ang) * wgt[:, None]).T
    k2 = np.arange(n2)
    m2 = np.arange(n2)
    kk = k1[:, None, None] + n1 * k2[None, :, None]
    th = 2.0 * np.pi * ((kk * m2[None, None, :]) % n) / n
    mc, ms = np.cos(th), np.sin(th)
    mfwd = np.concatenate([np.concatenate([mc, ms], axis=2), np.concatenate([-ms, mc], axis=2)], axis=1)
    mct, mst = np.transpose(mc, (0, 2, 1)), np.transpose(ms, (0, 2, 1))
    minv = np.concatenate([np.concatenate([mct, -mst], axis=2), np.concatenate([mst, mct], axis=2)], axis=1)
    return tuple(jnp.asarray(t, BF16) for t in (fa, ga, mfwd, minv))


def _hy_ctx_kernel(v_ref, x1_ref, x2_ref, fc_ref, fs_ref, gc_ref, gs_ref, ts_ref, td_ref, bias_ref, z_ref):
    d = functools.partial(jnp.dot, preferred_element_type=F32)
    w = v_ref.shape[2]
    kr_all = d(fc_ref[...], ts_ref[...].astype(BF16))
    ki_all = d(fs_ref[...], td_ref[...].astype(BF16))

    def conv(u, o):
        ub = u.astype(BF16)
        cr, ci = d(fc_ref[...], ub), d(fs_ref[...], ub)
        kr, ki = kr_all[:, o * w:(o + 1) * w], ki_all[:, o * w:(o + 1) * w]
        pr = (cr * kr - ci * ki).astype(BF16)
        pi = (cr * ki + ci * kr).astype(BF16)
        return d(gc_ref[...], pr) + d(gs_ref[...], pi) + u * bias_ref[o]

    z = x1_ref[0].astype(F32) * conv(v_ref[0].astype(F32), 0)
    z = x2_ref[0].astype(F32) * conv(z, 1)
    for j in range(2):
        z_ref[0, j] = z[:, j * 128:(j + 1) * 128]


def _hyena_ctx(v, x1, x2, fc, fs, gc, gs, tsum, tdiff, bias, seq, ctx_len):
    bsz, _, w = v.shape
    blk = seq // ctx_len
    kpad = fc.shape[0]
    tok = pl.BlockSpec((1, ctx_len, w), lambda b: (b, blk, 0))
    full = lambda shp: pl.BlockSpec(shp, lambda b: (0,) * len(shp))
    return pl.pallas_call(
        _hy_ctx_kernel, grid=(bsz,),
        in_specs=[tok, tok, tok, full((kpad, ctx_len)), full((kpad, ctx_len)), full((ctx_len, kpad)),
                  full((ctx_len, kpad)), full((ctx_len, HY_ORDER * w)), full((ctx_len, HY_ORDER * w)), full((2, 1, w))],
        out_specs=pl.BlockSpec((1, 2, ctx_len, 128), lambda b: (b, 0, 0, 0)),
        out_shape=jax.ShapeDtypeStruct((bsz, 2, ctx_len, 128), F32),
        compiler_params=_cparams(("parallel",)),
        name="hyena_ctx",
    )(v, x1, x2, fc, fs, gc, gs, tsum, tdiff, bias)


def _hyena_ctx_consts(ctx_len):
    n = 2 * ctx_len
    nk = ctx_len + 1
    kpad = -(-nk // 128) * 128
    k = np.arange(nk)
    m = np.arange(ctx_len)
    ang = 2.0 * np.pi * np.outer(k, m) / n
    fc = np.zeros((kpad, ctx_len))
    fs = np.zeros((kpad, ctx_len))
    fc[:nk] = np.cos(ang)
    fs[:nk] = -np.sin(ang)
    wgt = np.where((k == 0) | (k == ctx_len), 1.0, 2.0) / n
    gc = np.zeros((ctx_len, kpad))
    gs = np.zeros((ctx_len, kpad))
    gc[:, :nk] = (np.cos(ang) * wgt[:, None]).T
    gs[:, :nk] = (-np.sin(ang) * wgt[:, None]).T
    return tuple(jnp.asarray(t, BF16) for t in (fc, fs, gc, gs))


def _hyena_filter_taps(length, w1, b1, w2, b2, w3, freq, decay):
    pos = jnp.arange(length, dtype=F32)
    t = pos / max(length - 1, 1)
    bands = jnp.linspace(1e-4, HY_BANDS - 1, HY_BANDS, dtype=F32)
    ang = (2.0 * math.pi / length) * pos[:, None] * bands
    feats = jnp.concatenate([t[:, None], jnp.cos(ang), -jnp.sin(ang)], axis=-1)
    freq = freq.astype(F32)
    mm = functools.partial(jnp.matmul, precision=HI)
    hid = jnp.sin(freq * (mm(feats, w1.astype(F32)) + b1.astype(F32)))
    hid = jnp.sin(freq * (mm(hid, w2.astype(F32)) + b2.astype(F32)))
    h = mm(hid, w3.astype(F32)) * jnp.exp(-t[:, None] * jnp.abs(decay.astype(F32)))
    half = HY_ORDER * HY_WIDTH
    h_fwd, h_bwd = h[:, :half], h[:, half:]
    l1 = (jnp.abs(h_fwd[0] + h_bwd[0]) + jnp.sum(jnp.abs(h_fwd[1:]), axis=0) + jnp.sum(jnp.abs(h_bwd[1:]), axis=0))
    return (h_fwd + h_bwd) / l1, (h_fwd - h_bwd) / l1


def _mixout_kernel(nt_lat, xl_ref, xc_ref, m_ref, s5y_ref, yf_ref, yb_ref, xs_ref, z_ref, hyl_ref, hyc_ref,
                   wglu_ref, bglu_ref, vec512_ref, wout_ref, win_ref, wout2_ref, ln_ref, o_ref):
    m = m_ref[0, 0]
    is_lat = pl.program_id(1) < nt_lat
    hy = [jnp.where(is_lat, hyl_ref[0, j], hyc_ref[0, j]).astype(BF16) for j in range(2)]
    d = functools.partial(jnp.dot, preferred_element_type=F32)
    y5 = _gelu_tanh(jnp.concatenate([s5y_ref[0, 0], s5y_ref[0, 1]], axis=1))
    y5 = y5 * _sigmoid(d(y5.astype(BF16), wglu_ref[...]) + bglu_ref[...])
    ys = yf_ref[0, 0].astype(F32) + yb_ref[0, 0].astype(F32) + vec512_ref[0:1, :] * xs_ref[0].astype(F32)
    gsd = ys * _silu(z_ref[0].astype(F32))
    gsd = gsd * lax.rsqrt(jnp.mean(gsd * gsd, axis=-1, keepdims=True) + LN_EPS) * vec512_ref[1:2, :]
    mix = (d(y5.astype(BF16), wout_ref[0:256, :]) + d(gsd.astype(BF16), wout_ref[256:768, :])
           + d(hy[0], wout_ref[768:896, :]) + d(hy[1], wout_ref[896:1024, :]))
    r = ALPHA * _rows_of(nt_lat, xl_ref, xc_ref) + m[2:3] * mix
    x = _standardise(r) * ln_ref[0:1, :] + ln_ref[1:2, :]
    h = (_standardise(x) * (1.0 + m[4:5]) + m[3:4]).astype(BF16)
    gate = d(h, win_ref[:, 0:FFN_HIDDEN])
    up = d(h, win_ref[:, FFN_HIDDEN:2 * FFN_HIDDEN])
    act = (_silu(gate) * up).astype(BF16)
    r = ALPHA * x + m[5:6] * d(act, wout2_ref[...])
    o_ref[0] = _standardise(r) * ln_ref[2:3, :] + ln_ref[3:4, :]


def _mixout_ffn(x_lat, x_ctx, ctx_blk, mods, s5y, yssd, xbc_c, z, hy_lat, hy_ctx, wglu, bglu, vec512, wout, win,
                wout2, layer, ln, nt_lat, rows):
    bsz = x_lat.shape[0]
    nt = rows // ROW_TILE
    tok = lambda w: pl.BlockSpec((1, ROW_TILE, w), lambda b, i: (b, i, 0))
    halves = pl.BlockSpec((1, 2, ROW_TILE, 128), lambda b, i: (b, 0, i, 0))
    full = lambda shp: pl.BlockSpec(shp, lambda b, i: (0,) * len(shp))
    const = lambda shp: pl.BlockSpec((None,) + shp, lambda b, i: (layer, 0, 0), pipeline_mode=pl.Buffered(1))
    return pl.pallas_call(
        functools.partial(_mixout_kernel, nt_lat), grid=(bsz, nt),
        in_specs=_row_specs(nt_lat, ctx_blk, D_MODEL) + [
                  pl.BlockSpec((1, 1, 6, D_MODEL), lambda b, i: (b, jnp.where(i < nt_lat, 0, 1), 0, 0)),
                  halves,
                  pl.BlockSpec((1, 1, ROW_TILE, SSD_INNER), lambda b, i: (b, 0, i, 0)),
                  pl.BlockSpec((1, 1, ROW_TILE, SSD_INNER), lambda b, i: (b, 1, i, 0)),
                  tok(SSD_INNER), tok(SSD_INNER),
                  pl.BlockSpec((1, 2, ROW_TILE, 128), lambda b, i: (b, 0, jnp.minimum(i, nt_lat - 1), 0)),
                  pl.BlockSpec((1, 2, ROW_TILE, 128), lambda b, i: (b, 0, 0, 0)),
                  full((256, 256)), full((1, 256)), full((2, 512)),
                  const((D_MODEL, D_MODEL)), const((D_MODEL, 2 * FFN_HIDDEN)), const((FFN_HIDDEN, D_MODEL)),
                  full((4, D_MODEL))],
        out_specs=tok(D_MODEL),
        out_shape=jax.ShapeDtypeStruct((bsz, rows, D_MODEL), F32),
        compiler_params=_cparams(("parallel", "parallel"), V7X_VMEM_LIMIT_MB),
        name="mixout_ffn",
    )(x_lat, x_ctx, mods, s5y, yssd, yssd, xbc_c, z, hy_lat, hy_ctx, wglu, bglu, vec512, wout, win, wout2, ln)


def _layer(x_lat, x_ctx, ctx_blk, mods, p, seq, ctx_len, want_ctx):
    nt_lat = seq // ROW_TILE
    t_all = seq + ctx_len
    layer = p["layer"]
    s5u, z, xbc_c, v, x1, x2, dt_raw = _inproj(x_lat, x_ctx, ctx_blk, mods, p["w_in"], layer, p["ssd_conv_w"],
                                               p["ssd_conv_b"], p["hy_conv_w"], p["hy_conv_b"], nt_lat)
    s5y = _s5_scan(s5u, p["s5_ncat"], p["s5_tz"], p["s5_mcat"], p["s5_coef"], p["s5_dvec"],
                   seq // S5_CHUNK, ctx_len // S5_CHUNK)
    yssd = _ssd_scan(xbc_c, dt_raw, p["ssd_tri"], p["ssd_par"], p["ssd_expand"],
                     seq // SSD_CHUNK, ctx_len // SSD_CHUNK)
    hy_lat = _hyena_lat(v, x1, x2, p["hy_fa"], p["hy_ga"], p["hy_mfwd"], p["hy_minv"],
                        p["hy_kr"], p["hy_ki"], p["hy_bias"], seq)
    if want_ctx:
        hy_ctx = _hyena_ctx(v, x1, x2, *p["hy_ctx_mats"], *p["hy_ctx_taps"], p["hy_bias"], seq, ctx_len)
        rows = t_all
    else:
        hy_ctx, rows = hy_lat, seq
    return _mixout_ffn(x_lat, x_ctx, ctx_blk, mods, s5y, yssd, xbc_c, z, hy_lat, hy_ctx, p["s5_wglu"], p["s5_bglu"],
                       p["ssd_vec"], p["w_out"], p["ffn_w_in"], p["ffn_w_out"], layer, p["ln"], nt_lat, rows)


def kernel(x, c, ctx, c_ctx, w_mod, b_mod, w_in, s5_lam_re, s5_lam_im, s5_log_step, s5_b_re, s5_b_im, s5_c_re, s5_c_im, s5_d, s5_w_glu, s5_b_glu, ssd_conv_w, ssd_conv_b, ssd_dt_bias, ssd_a_log, ssd_d, ssd_norm_w, hy_conv_w, hy_conv_b, hy_w1, hy_b1, hy_w2, hy_b2, hy_w3, hy_freq, hy_decay, hy_bias, w_out, ln1_g, ln1_b, ffn_w_in, ffn_w_out, ln2_g, ln2_b):
    bsz, seq, _ = x.shape
    ctx_len = ctx.shape[1]
    assert bsz == 8 and seq % ROW_TILE == 0 and ctx_len == ROW_TILE

    fa, ga, mfwd, minv = _hyena_lat_consts(seq)
    ctx_mats = _hyena_ctx_consts(ctx_len)
    tt = np.arange(SSD_CHUNK)
    tri = jnp.asarray(np.stack([tt[None, :] <= tt[:, None], tt[None, :] >= tt[:, None]]), BF16)
    expand = jnp.asarray(np.repeat(np.eye(128, SSD_HEADS), SSD_HEADDIM, axis=1)[:, :SSD_INNER], BF16)
    cvec = jnp.zeros((16, D_MODEL), F32).at[:bsz].set(c.astype(F32)).at[bsz].set(c_ctx.astype(F32))

    w_p = jnp.concatenate([w_in[..., 0:1792], w_in[..., 1800:2568], w_in[..., 1792:1800],
                           jnp.zeros((DEPTH, D_MODEL, MIX_IN_PAD - 2568), w_in.dtype)], axis=-1).astype(BF16)
    w_out_b, ffn_in_b, ffn_out_b = w_out.astype(BF16), ffn_w_in.astype(BF16), ffn_w_out.astype(BF16)

    x_lat, x_ctx, ctx_blk = x.astype(F32), ctx.astype(F32), 0
    for l in range(DEPTH):
        want_ctx = l < DEPTH - 1
        mod16 = _modulation(cvec, w_mod.astype(F32), b_mod[l].astype(F32).reshape(1, -1), l)
        mods = jnp.stack([mod16[:bsz].reshape(bsz, 6, D_MODEL),
                          jnp.broadcast_to(mod16[bsz].reshape(1, 6, D_MODEL), (bsz, 6, D_MODEL))], axis=1)
        ncat, tz, mcat, coef = _s5_weights(s5_lam_re[l], s5_lam_im[l], s5_log_step[l], s5_b_re[l], s5_b_im[l],
                                           s5_c_re[l], s5_c_im[l])
        rep = lambda t: jnp.repeat(t.astype(F32), SSD_HEADDIM, axis=-1)
        par = jnp.zeros((2, 8, 128), F32)
        par = par.at[:, 0, :SSD_HEADS].set(ssd_dt_bias[l].astype(F32))
        par = par.at[:, 1, :SSD_HEADS].set(-jnp.exp(ssd_a_log[l].astype(F32)))
        hy_args = (hy_w1[l], hy_b1[l], hy_w2[l], hy_b2[l], hy_w3[l], hy_freq[l], hy_decay[l])
        kr, ki = _hyena_spectrum(*_hyena_filter_taps(seq, *hy_args), fa, mfwd)
        p = dict(
            layer=l, w_in=w_p, s5_ncat=ncat, s5_tz=tz, s5_mcat=mcat, s5_coef=coef,
            ssd_conv_w=ssd_conv_w[l].astype(F32), ssd_conv_b=ssd_conv_b[l].astype(F32),
            ssd_tri=tri, ssd_par=par, ssd_expand=expand,
            hy_conv_w=hy_conv_w[l].astype(F32), hy_conv_b=hy_conv_b[l].astype(F32),
            hy_fa=fa, hy_ga=ga, hy_mfwd=mfwd, hy_minv=minv, hy_kr=kr, hy_ki=ki,
            hy_bias=hy_bias[l].astype(F32)[:, None, :],
            s5_wglu=s5_w_glu[l].astype(BF16),
            s5_bglu=s5_b_glu[l].astype(F32).reshape(1, S5_WIDTH),
            s5_dvec=jnp.tile(s5_d[l].astype(F32).reshape(S5_GROUPS, 1, S5_GROUP_CH), (1, 1, S5_CHUNK)),
            ssd_vec=jnp.stack([rep(ssd_d[l]), ssd_norm_w[l].astype(F32)]),
            w_out=w_out_b, ln=jnp.stack([ln1_g[l], ln1_b[l], ln2_g[l], ln2_b[l]]).astype(F32),
            ffn_w_in=ffn_in_b, ffn_w_out=ffn_out_b,
        )
        if want_ctx:
            p.update(hy_ctx_mats=ctx_mats, hy_ctx_taps=_hyena_filter_taps(ctx_len, *hy_args))
        x_lat = _layer(x_lat, x_ctx, ctx_blk, mods, p, seq, ctx_len, want_ctx)
        x_ctx, ctx_blk = x_lat, seq // ROW_TILE
    return x_lat.astype(x.dtype)
```

```python
import functools
import math

import numpy as np
import jax
import jax.numpy as jnp
from jax import lax
from jax.experimental import pallas as pl
from jax.experimental.pallas import tpu as pltpu

F32 = jnp.float32
BF16 = jnp.bfloat16
HI = lax.Precision.HIGHEST

D_MODEL = 1024
DEPTH = 2
S5_WIDTH = 256
S5_GROUP_CH = 16
S5_GROUPS = 16
S5_STATE = 64
S5_MAX_RE = -1e-4
S5_CHUNK = 16
SSD_INNER = 512
SSD_HEADDIM = 64
SSD_HEADS = 8
SSD_GROUPS = 2
SSD_STATE = 128
SSD_CONV = 5
SSD_CHUNK = 128
SSD_CHUNKS_PER_STEP = 2
SSD_XBC = 1024
HY_WIDTH = 256
HY_ORDER = 2
HY_SHORT = 3
HY_BANDS = 16
HY_IN = 768
MIX_IN_PAD = 2688
FFN_HIDDEN = 2816
ALPHA = (2 * DEPTH) ** 0.25
LN_EPS = 1e-6

ROW_TILE = 256
MIX_SUB_TILES = 2
CONV_GROUP = 256
HALO = 16
FFT_N2 = 128
HY_A_PITCH = 72
HY_K1_PER_STEP = 11
HY_B_PITCH = 136
V7X_VMEM_LIMIT_MB = 56


def _cparams(sem, vmem_mb=None):
    kw = dict(dimension_semantics=sem)
    if vmem_mb is not None:
        kw["vmem_limit_bytes"] = vmem_mb * 2 ** 20
    return pltpu.CompilerParams(**kw)


def _standardise(x):
    mu = jnp.mean(x, axis=-1, keepdims=True)
    xc = x - mu
    var = jnp.mean(xc * xc, axis=-1, keepdims=True)
    return xc * lax.rsqrt(var + LN_EPS)


def _sigmoid(x):
    return 1.0 / (1.0 + jnp.exp(-x))


def _silu(x):
    return x * _sigmoid(x)


def _gelu_tanh(x):
    return 0.5 * x * (1.0 + jnp.tanh(0.7978845608028654 * (x + 0.044715 * (x * x * x))))


def _softplus(x):
    return jnp.maximum(x, 0.0) + jnp.log(1.0 + jnp.exp(-jnp.abs(x)))


def _bdot(a, b):
    return jnp.dot(a.astype(BF16), b.astype(BF16), preferred_element_type=F32)


def _split3(a):
    a1 = a.astype(BF16)
    r1 = a - a1.astype(F32)
    a2 = r1.astype(BF16)
    a3 = (r1 - a2.astype(F32)).astype(BF16)
    return a1, a2, a3


def _dot_sel_lhs(sel, a):
    a1, a2, a3 = _split3(a)
    d = functools.partial(jnp.dot, preferred_element_type=F32)
    return d(sel, a1) + d(sel, a2) + d(sel, a3)


def _dot_sel_rhs(a, sel):
    a1, a2, a3 = _split3(a)
    d = functools.partial(jnp.dot, preferred_element_type=F32)
    return d(a1, sel) + d(a2, sel) + d(a3, sel)


def _mod_kernel(c_ref, w_ref, b_ref, o_ref):
    ca = _silu(c_ref[...])
    c1, c2, c3 = _split3(ca)
    w1, w2, w3 = _split3(w_ref[...])
    d = functools.partial(jnp.dot, preferred_element_type=F32)
    acc = d(c1, w1) + d(c1, w2) + d(c2, w1) + d(c1, w3) + d(c2, w2) + d(c3, w1)
    o_ref[...] = acc + b_ref[...]


def _modulation(cvec, w, b, layer):
    n = w.shape[2]
    tn = 1536
    return pl.pallas_call(
        _mod_kernel,
        grid=(n // tn,),
        in_specs=[pl.BlockSpec((16, D_MODEL), lambda j: (0, 0)),
                  pl.BlockSpec((None, D_MODEL, tn), lambda j: (layer, 0, j)),
                  pl.BlockSpec((1, tn), lambda j: (0, j))],
        out_specs=pl.BlockSpec((16, tn), lambda j: (0, j)),
        out_shape=jax.ShapeDtypeStruct((16, n), F32),
        compiler_params=_cparams(("arbitrary",), 40),
        name="modulation",
    )(cvec, w, b)


def _rows_of(nt_lat, xl_ref, xc_ref):
    return jnp.where(pl.program_id(1) < nt_lat, xl_ref[0], xc_ref[0])


def _row_specs(nt_lat, ctx_blk, width):
    return [pl.BlockSpec((1, ROW_TILE, width), lambda b, i: (b, jnp.minimum(i, nt_lat - 1), 0)),
            pl.BlockSpec((1, ROW_TILE, width), lambda b, i: (b, ctx_blk, 0))]


def _dwconv_rows(ext_ref, w_ref, b_ref, wcol, taps, act, o_ref, ocol):
    pad = taps // 2
    rb = 64
    for c in range(0, CONV_GROUP, 128):
        wk = [w_ref[k:k + 1, wcol + c:wcol + c + 128] for k in range(taps)]
        bias = b_ref[0:1, wcol + c:wcol + c + 128]
        for r0 in range(0, ROW_TILE, rb):
            base = HALO - pad + r0
            acc = bias + wk[0] * ext_ref[base:base + rb, c:c + 128]
            for k in range(1, taps):
                acc = acc + wk[k] * ext_ref[base + k:base + k + rb, c:c + 128]
            if act:
                acc = _silu(acc)
            o_ref[0, r0:r0 + rb, ocol + c:ocol + c + 128] = acc.astype(o_ref.dtype)


def _inproj_kernel(nt_lat, xl_ref, xc_ref, xp_ref, xn_ref, m_ref, w_ref, w5_ref, b5_ref, w3_ref, b3_ref,
                   s5_ref, z_ref, xbc_ref, v_ref, x1_ref, x2_ref, dt_ref, *ext_refs):
    i = pl.program_id(1)
    m = m_ref[0, 0]
    mod = lambda x: (_standardise(x) * (1.0 + m[1:2]) + m[0:1]).astype(BF16)
    d = functools.partial(jnp.dot, preferred_element_type=F32)
    h = mod(_rows_of(nt_lat, xl_ref, xc_ref))
    hp, hn = mod(xp_ref[0]), mod(xn_ref[0])
    has_prev = jnp.logical_and(i > 0, i < nt_lat)
    has_next = i < nt_lat - 1
    n5 = SSD_XBC // CONV_GROUP
    hy_outs = (v_ref, x1_ref, x2_ref)

    def project(k):
        wc = w_ref[:, 768 + k * CONV_GROUP:768 + (k + 1) * CONV_GROUP]
        e = ext_refs[k]
        e[0:HALO, :] = jnp.where(has_prev, d(hp, wc), 0.0)
        e[HALO:HALO + ROW_TILE, :] = d(h, wc)
        e[HALO + ROW_TILE:2 * HALO + ROW_TILE, :] = jnp.where(has_next, d(hn, wc), 0.0)

    def conv(k):
        if k < n5:
            _dwconv_rows(ext_refs[k], w5_ref, b5_ref, k * CONV_GROUP, SSD_CONV, True, xbc_ref, k * CONV_GROUP)
        else:
            _dwconv_rows(ext_refs[k], w3_ref, b3_ref, (k - n5) * CONV_GROUP, HY_SHORT, False, hy_outs[k - n5], 0)

    ngroups = len(ext_refs)
    project(0)
    for k in range(ngroups):
        if k + 1 < ngroups:
            project(k + 1)
        conv(k)
    s5_ref[0] = d(h, w_ref[:, 0:256])
    z_ref[0] = d(h, w_ref[:, 256:768]).astype(z_ref.dtype)
    dt_ref[0] = d(h, w_ref[:, 2560:2688])


def _inproj(x_lat, x_ctx, ctx_blk, mods, w_p, layer, conv5_w, conv5_b, conv3_w, conv3_b, nt_lat):
    bsz = x_lat.shape[0]
    nt = nt_lat + 1
    t_all = nt * ROW_TILE
    rh = ROW_TILE // HALO
    last = nt_lat * rh - 1
    full = lambda shp: pl.BlockSpec(shp, lambda b, i: (0,) * len(shp))
    outs = ((256, F32), (512, BF16), (SSD_XBC, BF16), (HY_WIDTH, BF16), (HY_WIDTH, BF16), (HY_WIDTH, BF16), (128, F32))
    return pl.pallas_call(
        functools.partial(_inproj_kernel, nt_lat),
        grid=(bsz, nt),
        in_specs=_row_specs(nt_lat, ctx_blk, D_MODEL) + [
                  pl.BlockSpec((1, HALO, D_MODEL), lambda b, i: (b, jnp.clip(i * rh - 1, 0, last), 0)),
                  pl.BlockSpec((1, HALO, D_MODEL), lambda b, i: (b, jnp.clip((i + 1) * rh, 0, last), 0)),
                  pl.BlockSpec((1, 1, 6, D_MODEL), lambda b, i: (b, jnp.where(i < nt_lat, 0, 1), 0, 0)),
                  pl.BlockSpec((None, D_MODEL, MIX_IN_PAD), lambda b, i: (layer, 0, 0)), full((SSD_CONV, SSD_XBC)), full((1, SSD_XBC)),
                  full((HY_SHORT, HY_IN)), full((1, HY_IN))],
        out_specs=[pl.BlockSpec((1, ROW_TILE, w), lambda b, i: (b, i, 0)) for w, _ in outs],
        out_shape=[jax.ShapeDtypeStruct((bsz, t_all, w), dt) for w, dt in outs],
        scratch_shapes=[pltpu.VMEM((ROW_TILE + 2 * HALO, CONV_GROUP), F32)] * ((SSD_XBC + HY_IN) // CONV_GROUP),
        compiler_params=_cparams(("parallel", "parallel"), 40),
        name="inproj",
    )(x_lat, x_ctx, x_lat, x_lat, mods, w_p, conv5_w, conv5_b.reshape(1, -1), conv3_w, conv3_b.reshape(1, -1))


def _s5_kernel(n_lat, n_ctx, u0_ref, u1_ref, ncat_ref, tz_ref, mcat_ref, coef_ref, dvec_ref, y_ref,
               ut_ref, s_ref, hp_ref):
    q, gch, ng = S5_CHUNK, S5_GROUP_CH, S5_GROUPS
    nc = n_lat + n_ctx
    dot = functools.partial(jnp.dot, preferred_element_type=F32)
    u_refs = (u0_ref, u1_ref)
    gpl = 128 // gch
    for j in range(2):
        for s in range(q):
            rows = u_refs[j][0, pl.ds(s, nc, stride=q), :]
            for gg in range(gpl):
                ut_ref[gpl * j + gg, :, s * gch:(s + 1) * gch] = rows[:, gg * gch:(gg + 1) * gch]
    for g in range(ng):
        sg = dot(ut_ref[g].astype(BF16), ncat_ref[g])
        for k in range(4):
            s_ref[k, pl.ds(g, nc, stride=ng), :] = sg[:, k * 128:(k + 1) * 128]
    c1f, c2f, c1b, c2b = coef_ref[0], coef_ref[1], coef_ref[2], coef_ref[3]

    def step(cf, cb, carry):
        hf, hsf, hb, hsb = carry
        rf = pl.multiple_of(cf * ng, ng)
        rb = pl.multiple_of(cb * ng, ng)
        hp_ref[0, pl.ds(rf, ng), :] = hf
        hp_ref[1, pl.ds(rb, ng), :] = hb
        sf = s_ref[0, pl.ds(rf, ng), :]
        sb = s_ref[1, pl.ds(rb, ng), :]
        ssf = s_ref[2, pl.ds(rf, ng), :]
        ssb = s_ref[3, pl.ds(rb, ng), :]
        return (c1f * hf + c2f * hsf + sf, c1f * hsf - c2f * hf + ssf,
                c1b * hb + c2b * hsb + sb, c1b * hsb - c2b * hb + ssb)

    z = jnp.zeros((ng, 128), F32)
    carry = lax.fori_loop(0, n_ctx, lambda i, c: step(n_lat + i, n_lat + n_ctx - 1 - i, c), (z, z, z, z))
    lax.fori_loop(0, n_lat, lambda i, c: step(i, n_lat - 1 - i, c), carry)
    for g in range(ng):
        ug = ut_ref[g]
        hp = jnp.concatenate([hp_ref[0, pl.ds(g, nc, stride=ng), :], hp_ref[1, pl.ds(g, nc, stride=ng), :]], axis=1)
        ut_ref[g] = dot(ug.astype(BF16), tz_ref[g]) + dot(hp.astype(BF16), mcat_ref[g]) + ug * dvec_ref[g]
    for j in range(2):
        for s in range(q):
            rows = jnp.concatenate([ut_ref[gpl * j + gg, :, s * gch:(s + 1) * gch] for gg in range(gpl)], axis=1)
            y_ref[0, j, pl.ds(s, nc, stride=q), :] = rows


def _s5_scan(s5u, ncat, tz, mcat, coef, dvec, n_lat, n_ctx):
    bsz, t_all, _ = s5u.shape
    nc = n_lat + n_ctx
    ng = S5_GROUPS
    const = lambda shp: pl.BlockSpec(shp, lambda b: (0,) * len(shp), pipeline_mode=pl.Buffered(1))
    return pl.pallas_call(
        functools.partial(_s5_kernel, n_lat, n_ctx),
        grid=(bsz,),
        in_specs=[pl.BlockSpec((1, t_all, 128), lambda b: (b, 0, 0)),
                  pl.BlockSpec((1, t_all, 128), lambda b: (b, 0, 1)),
                  const((ng, 256, 512)), const((ng, 256, 256)), const((ng, 256, 256)),
                  const((4, ng, 128)), const((ng, 1, 256))],
        out_specs=pl.BlockSpec((1, 2, t_all, 128), lambda b: (b, 0, 0, 0)),
        out_shape=jax.ShapeDtypeStruct((bsz, 2, t_all, 128), F32),
        scratch_shapes=[pltpu.VMEM((ng, nc, 256), F32), pltpu.VMEM((4, nc * ng, 128), F32),
                        pltpu.VMEM((2, nc * ng, 128), F32)],
        compiler_params=_cparams(("parallel",), V7X_VMEM_LIMIT_MB),
        name="s5_scan",
    )(s5u, s5u, ncat, tz, mcat, coef, dvec)


def _s5_weights(lam_re, lam_im, log_step, b_re, b_im, c_re, c_im):
    q, ng, gch = S5_CHUNK, S5_GROUPS, S5_GROUP_CH
    a_re = jnp.minimum(lam_re.astype(F32), S5_MAX_RE)
    a_im = lam_im.astype(F32)
    step = jnp.exp(log_step.astype(F32))[..., None]
    taus = jnp.arange(q + 1, dtype=F32)[:, None, None, None]
    mag = jnp.exp(taus * (a_re * step))
    pr, pi = mag * jnp.cos(taus * (a_im * step)), mag * jnp.sin(taus * (a_im * step))
    nr, ni, den = pr[1] - 1.0, pi[1], a_re * a_re + a_im * a_im
    fr, fi = ((nr * a_re + ni * a_im) / den)[..., None], ((ni * a_re - nr * a_im) / den)[..., None]
    b_r, b_i = b_re.astype(F32), b_im.astype(F32)
    bb_r, bb_i = fr * b_r - fi * b_i, fr * b_i + fi * b_r
    c_r = jnp.swapaxes(c_re.astype(F32), -1, -2)
    c_i = jnp.swapaxes(c_im.astype(F32), -1, -2)
    pw_r, pw_i = jnp.transpose(pr, (1, 2, 3, 0)), jnp.transpose(pi, (1, 2, 3, 0))

    def lagged(v_r, v_i, d, lags):
        e_r, e_i = jnp.repeat(pw_r[d][..., lags], gch, axis=-1), jnp.repeat(pw_i[d][..., lags], gch, axis=-1)
        n = e_r.shape[-1] // gch
        t_r, t_i = jnp.tile(v_r[d], (1, 1, n)), jnp.tile(v_i[d], (1, 1, n))
        return t_r * e_r - t_i * e_i, t_r * e_i + t_i * e_r

    def lag_kernels(d, lags):
        cp_r, cp_i = lagged(c_r, c_i, d, lags)
        return (jnp.einsum('gpk,gpn->gkn', bb_r[d], cp_r, precision=HI)
                - jnp.einsum('gpk,gpn->gkn', bb_i[d], cp_i, precision=HI))

    k_fwd = lag_kernels(0, slice(0, q))
    k_bwd = lag_kernels(1, slice(q - 1, None, -1))
    edge = (q - 1) * gch
    kfull = jnp.concatenate([k_bwd[..., :edge], k_bwd[..., edge:] + k_fwd[..., :gch], k_fwd[..., gch:]], axis=-1)
    tz = jnp.stack([kfull[:, :, (q - 1 - s) * gch:(2 * q - 1 - s) * gch] for s in range(q)],
                   axis=1).reshape(ng, q * gch, q * gch)
    nf_r, nf_i = (jnp.swapaxes(t, -1, -2) for t in lagged(bb_r, bb_i, 0, slice(q - 1, None, -1)))
    nb_r, nb_i = (jnp.swapaxes(t, -1, -2) for t in lagged(bb_r, bb_i, 1, slice(0, q)))
    ncat = jnp.concatenate([nf_r, nf_i, nb_r, nb_i, nf_i, nf_r, nb_i, nb_r], axis=-1)
    mf_r, mf_i = lagged(c_r, c_i, 0, slice(1, None))
    mb_r, mb_i = lagged(c_r, c_i, 1, slice(q, 0, -1))
    mcat = jnp.concatenate([mf_r, -mf_i, mb_r, -mb_i], axis=1)
    lr, li = pr[q], pi[q]
    coef = jnp.stack([jnp.concatenate([lr[0], lr[0]], axis=-1), jnp.concatenate([-li[0], li[0]], axis=-1),
                      jnp.concatenate([lr[1], lr[1]], axis=-1), jnp.concatenate([-li[1], li[1]], axis=-1)])
    return ncat.astype(BF16), tz.astype(BF16), mcat.astype(BF16), coef.astype(F32)


def _ssd_kernel(cps, xbc_ref, dt_ref, tri_ref, par_ref, exp_ref, y_ref, st_ref):
    direction = pl.program_id(1)

    @pl.when(pl.program_id(2) == 0)
    def _():
        st_ref[...] = jnp.zeros_like(st_ref)

    tri = tri_ref[0]
    mask = tri > 0
    lane = lax.broadcasted_iota(jnp.int32, (SSD_CHUNK, 128), 1)
    lo = lane < SSD_HEADDIM
    zero_b = jnp.zeros((SSD_CHUNK, 128), BF16)
    head_of_lane = lax.broadcasted_iota(jnp.int32, (1, SSD_INNER), 1) // SSD_HEADDIM
    expand = lambda t: jnp.dot(t.astype(BF16), exp_ref[...], preferred_element_type=F32)
    state = [st_ref[:, g * 256:(g + 1) * 256] for g in range(SSD_GROUPS)]
    for c in range(cps):
        sub = jnp.where(direction == 0, c, cps - 1 - c)
        rows = pl.ds(pl.multiple_of(sub * SSD_CHUNK, SSD_CHUNK), SSD_CHUNK)
        xs = xbc_ref[0, rows, 0:512].astype(F32)
        dt_c = _softplus(dt_ref[0, rows, :] + par_ref[0, 0:1, :])
        a_c = par_ref[0, 1:2, :] * dt_c
        cs = _dot_sel_lhs(tri, a_c)
        tot = jnp.sum(a_c, axis=0, keepdims=True)
        cs_t = cs.T
        xdt = xs * expand(dt_c)
        xdt_b = xdt.astype(BF16)
        xd_end = (xdt * expand(jnp.exp(tot - cs))).astype(BF16)
        e_cs = expand(jnp.exp(cs))
        tot_full = jnp.zeros((1, SSD_INNER), F32)
        for head in range(SSD_HEADS):
            tot_full = jnp.where(head_of_lane == head, tot[:, head:head + 1], tot_full)
        e_tot = jnp.exp(tot_full)
        for g in range(SSD_GROUPS):
            bm_b = xbc_ref[0, rows, 512 + g * 128:512 + (g + 1) * 128]
            cm_b = xbc_ref[0, rows, 768 + g * 128:768 + (g + 1) * 128]
            cb = lax.dot_general(cm_b, bm_b, (((1,), (1,)), ((), ())), preferred_element_type=F32)
            y_off = jnp.dot(cm_b, state[g].astype(BF16), preferred_element_type=F32) * e_cs[:, g * 256:(g + 1) * 256]
            for j in range(2):
                c0 = g * 256 + j * 128
                x_pair = xdt_b[:, c0:c0 + 128]
                acc = y_off[:, j * 128:(j + 1) * 128]
                for hh in range(2):
                    head = c0 // SSD_HEADDIM + hh
                    decay = jnp.exp(jnp.where(mask, cs[:, head:head + 1] - cs_t[head:head + 1, :], -1e30))
                    gm = (cb * decay).astype(BF16)
                    xh = jnp.where(lo if hh == 0 else jnp.logical_not(lo), x_pair, zero_b)
                    acc = acc + jnp.dot(gm, xh, preferred_element_type=F32)
                y_ref[0, 0, rows, c0:c0 + 128] = acc.astype(y_ref.dtype)
            state[g] = (state[g] * e_tot[:, g * 256:(g + 1) * 256]
                        + jnp.dot(bm_b.astype(F32).T.astype(BF16), xd_end[:, g * 256:(g + 1) * 256],
                                  preferred_element_type=F32))
    for g in range(SSD_GROUPS):
        st_ref[:, g * 256:(g + 1) * 256] = state[g]


def _ssd_scan(xbc_c, dt_raw, tri, par, expand, n_lat, n_ctx):
    bsz, t_all, _ = xbc_c.shape
    cps = SSD_CHUNKS_PER_STEP
    assert n_lat % cps == 0 and n_ctx % cps == 0
    n_lat, n_ctx = n_lat // cps, n_ctx // cps
    nc = n_lat + n_ctx
    rows = cps * SSD_CHUNK

    def blk(d, i):
        fwd = jnp.where(i < n_ctx, n_lat + i, i - n_ctx)
        return jnp.where(d == 0, fwd, nc - 1 - i)

    return pl.pallas_call(
        functools.partial(_ssd_kernel, cps),
        grid=(bsz, 2, nc),
        in_specs=[pl.BlockSpec((1, rows, SSD_XBC), lambda b, d, i: (b, blk(d, i), 0)),
                  pl.BlockSpec((1, rows, 128), lambda b, d, i: (b, blk(d, i), 0)),
                  pl.BlockSpec((1, SSD_CHUNK, SSD_CHUNK), lambda b, d, i: (d, 0, 0)),
                  pl.BlockSpec((1, 8, 128), lambda b, d, i: (d, 0, 0)),
                  pl.BlockSpec((128, SSD_INNER), lambda b, d, i: (0, 0))],
        out_specs=pl.BlockSpec((1, 1, rows, SSD_INNER), lambda b, d, i: (b, d, blk(d, i), 0)),
        out_shape=jax.ShapeDtypeStruct((bsz, 2, t_all, SSD_INNER), BF16),
        scratch_shapes=[pltpu.VMEM((SSD_STATE, SSD_INNER), F32)],
        compiler_params=_cparams(("parallel", "parallel", "arbitrary")),
        name="ssd_scan",
    )(xbc_c, dt_raw, tri, par, expand)


def _hy_outer_forward(tile, fa_ref, a2_ref, h1, kp):
    dot = functools.partial(jnp.dot, preferred_element_type=F32)
    fa = fa_ref[...]

    def body(i8, carry):
        r0 = pl.multiple_of(i8 * 8, 8)
        x = jnp.concatenate([tile(n1, r0) for n1 in range(h1)], axis=0)
        out = dot(fa, x.astype(BF16))
        for j in range(8):
            row = pl.multiple_of((i8 * 8 + j) * HY_A_PITCH, 8)
            for s in range(2):
                a2_ref[s, pl.ds(row, kp), :] = out[j * kp:(j + 1) * kp, s * 128:(s + 1) * 128]
        return carry

    lax.fori_loop(0, FFT_N2 // 8, body, 0, unroll=2)


def _hy_lat_kernel(k1n, kgrp, v_ref, x1_ref, x2_ref, fa_ref, ga_ref, mf_ref, mi_ref, kr_ref, ki_ref, bias_ref,
                   o_ref, z_ref, a2_ref, b2_ref):
    order = pl.program_id(1)
    ph = pl.program_id(2)
    n2 = FFT_N2
    kp, h1 = fa_ref.shape[0] // 8, fa_ref.shape[1] // 8
    ks = ga_ref.shape[1] // 8
    seq = h1 * n2
    ngroups = k1n // kgrp
    dot = functools.partial(jnp.dot, preferred_element_type=F32)

    @pl.when(jnp.logical_and(order == 0, ph == 0))
    def _():
        for j in range(2):
            z_ref[j] = v_ref[0, :, j * 128:(j + 1) * 128].astype(F32)
            b2_ref[j, 2 * k1n * HY_B_PITCH:ks * HY_B_PITCH, :] = jnp.zeros(((ks - 2 * k1n) * HY_B_PITCH, 128), F32)

    @pl.when(ph == 0)
    def _():
        tile = lambda n1, r0: jnp.concatenate([z_ref[j, pl.ds(n1 * n2 + r0, 8), :] for j in range(2)], axis=1)
        _hy_outer_forward(tile, fa_ref, a2_ref, h1, kp)

    @pl.when(jnp.logical_and(ph >= 1, ph <= ngroups))
    def _():
        for t in range(kgrp):
            k1 = (ph - 1) * kgrp + t
            are = jnp.concatenate([a2_ref[j, pl.ds(2 * k1, n2, stride=HY_A_PITCH), :] for j in range(2)], axis=1)
            aim = jnp.concatenate([a2_ref[j, pl.ds(2 * k1 + 1, n2, stride=HY_A_PITCH), :] for j in range(2)], axis=1)
            x = dot(mf_ref[k1], jnp.concatenate([are, aim], axis=0).astype(BF16))
            xr, xi = x[:n2], x[n2:]
            kr, ki = kr_ref[t].astype(F32), ki_ref[t].astype(F32)
            y = jnp.concatenate([xr * kr - xi * ki, xr * ki + xi * kr], axis=0).astype(BF16)
            bq = dot(mi_ref[k1], y)
            row = pl.multiple_of(2 * k1 * HY_B_PITCH, 8)
            for j in range(2):
                b2_ref[j, pl.ds(row, n2), :] = bq[:n2, j * 128:(j + 1) * 128]
                b2_ref[j, pl.ds(row + HY_B_PITCH, n2), :] = bq[n2:, j * 128:(j + 1) * 128]

    @pl.when(ph == ngroups + 1)
    def _():
        ga = ga_ref[...]

        def body(i8, carry):
            r0 = pl.multiple_of(i8 * 8, 8)
            rows = jnp.concatenate(
                [jnp.concatenate([b2_ref[j, pl.ds(k * HY_B_PITCH + r0, 8), :] for j in range(2)], axis=1)
                 for k in range(ks)], axis=0)
            y = dot(ga, rows.astype(BF16))
            for n1 in range(h1):
                for j in range(2):
                    o_ref[0, j, pl.ds(n1 * n2 + r0, 8), :] = y[n1 * 8:(n1 + 1) * 8, j * 128:(j + 1) * 128]
            return carry

        lax.fori_loop(0, n2 // 8, body, 0, unroll=2)
        rb = min(256, seq)

        def gate(i, carry):
            r0 = pl.multiple_of(i * rb, 8)
            for j in range(2):
                conv = o_ref[0, j, pl.ds(r0, rb), :] + z_ref[j, pl.ds(r0, rb), :] * bias_ref[0, :, j * 128:(j + 1) * 128]
                x1 = x1_ref[0, pl.ds(r0, rb), j * 128:(j + 1) * 128].astype(F32)
                x2 = x2_ref[0, pl.ds(r0, rb), j * 128:(j + 1) * 128].astype(F32)
                z_ref[j, pl.ds(r0, rb), :] = x1 * conv
                o_ref[0, j, pl.ds(r0, rb), :] = x2 * conv
            return carry

        lax.fori_loop(0, seq // rb, gate, 0)


def _hy_spec_kernel(k1n, ts_ref, td_ref, fa_ref, mf_ref, kr_ref, ki_ref, a2_ref):
    n2 = FFT_N2
    kp, h1 = fa_ref.shape[0] // 8, fa_ref.shape[1] // 8
    dot = functools.partial(jnp.dot, preferred_element_type=F32)
    tile = lambda n1, r0: jnp.concatenate([ts_ref[pl.ds(n1 * n2 + r0, 8), :], td_ref[pl.ds(n1 * n2 + r0, 8), :]], axis=1)
    _hy_outer_forward(tile, fa_ref, a2_ref, h1, kp)

    def per_k1(k1, carry):
        are = jnp.concatenate([a2_ref[j, pl.ds(2 * k1, n2, stride=HY_A_PITCH), :] for j in range(2)], axis=1)
        aim = jnp.concatenate([a2_ref[j, pl.ds(2 * k1 + 1, n2, stride=HY_A_PITCH), :] for j in range(2)], axis=1)
        x = dot(mf_ref[k1], jnp.concatenate([are, aim], axis=0).astype(BF16))
        kr_ref[k1] = x[:n2, 0:128].astype(kr_ref.dtype)
        ki_ref[k1] = x[n2:, 128:256].astype(ki_ref.dtype)
        return carry

    lax.fori_loop(0, k1n, per_k1, 0)


def _hyena_spectrum(tsum, tdiff, fa, mfwd):
    seq, lanes = tsum.shape
    n2 = FFT_N2
    k1n = mfwd.shape[0]
    taps = pl.BlockSpec((seq, 128), lambda j: (0, j))
    out = pl.BlockSpec((k1n, n2, 128), lambda j: (0, 0, j))
    return pl.pallas_call(
        functools.partial(_hy_spec_kernel, k1n),
        grid=(lanes // 128,),
        in_specs=[taps, taps, pl.BlockSpec(fa.shape, lambda j: (0, 0)),
                  pl.BlockSpec((k1n, 2 * n2, 2 * n2), lambda j: (0, 0, 0))],
        out_specs=[out, out],
        out_shape=[jax.ShapeDtypeStruct((k1n, n2, lanes), BF16)] * 2,
        scratch_shapes=[pltpu.VMEM((2, n2 * HY_A_PITCH, 128), F32)],
        compiler_params=_cparams(("parallel",), 40),
        name="hyena_spectrum",
    )(tsum, tdiff, fa, mfwd)


def _hyena_lat(v, x1, x2, fa, ga, mfwd, minv, kr, ki, bias, seq):
    bsz, t_all, w = v.shape
    n2 = FFT_N2
    ks = ga.shape[1] // 8
    k1n = mfwd.shape[0]
    kgrp = next(g for g in (HY_K1_PER_STEP, 3, 1) if k1n % g == 0)
    ngroups = k1n // kgrp
    tok = pl.BlockSpec((1, seq, w), lambda b, o, p: (b, 0, 0), pipeline_mode=pl.Buffered(1))
    full = lambda shp: pl.BlockSpec(shp, lambda b, o, p: (0,) * len(shp))
    grp = lambda p: jnp.clip(p - 1, 0, ngroups - 1)
    mat = pl.BlockSpec((k1n, 2 * n2, 2 * n2), lambda b, o, p: (0, 0, 0), pipeline_mode=pl.Buffered(1))
    spec = pl.BlockSpec((kgrp, n2, w), lambda b, o, p: (grp(p), 0, o))
    return pl.pallas_call(
        functools.partial(_hy_lat_kernel, k1n, kgrp),
        grid=(bsz, HY_ORDER, ngroups + 2),
        in_specs=[tok, tok, tok, full(fa.shape), full(ga.shape), mat, mat, spec, spec,
                  pl.BlockSpec((1, 1, w), lambda b, o, p: (o, 0, 0))],
        out_specs=pl.BlockSpec((1, 2, seq, 128), lambda b, o, p: (b, 0, 0, 0)),
        out_shape=jax.ShapeDtypeStruct((bsz, 2, seq, 128), F32),
        scratch_shapes=[pltpu.VMEM((2, seq, 128), F32),
                        pltpu.VMEM((2, n2 * HY_A_PITCH, 128), F32),
                        pltpu.VMEM((2, ks * HY_B_PITCH, 128), F32)],
        compiler_params=_cparams(("parallel", "arbitrary", "arbitrary"), V7X_VMEM_LIMIT_MB),
        name="hyena_lat",
    )(v, x1, x2, fa, ga, mfwd, minv, kr, ki, bias)


def _hyena_lat_consts(seq):
    n = 2 * seq
    n2 = FFT_N2
    n1 = n // n2
    h1 = n1 // 2
    k1n = n1 // 2 + 1
    kp = -(-2 * k1n // 8) * 8
    ks = -(-2 * k1n // 16) * 16
    assert kp <= HY_A_PITCH and n2 <= HY_B_PITCH
    k1 = np.arange(k1n)
    m1 = np.arange(h1)
    ang = 2.0 * np.pi * np.outer(k1, m1) / n1
    fa = np.zeros((kp, h1))
    fa[0:2 * k1n:2] = np.cos(ang)
    fa[1:2 * k1n:2] = -np.sin(ang)
    wgt = np.where((k1 == 0) | (k1 == n1 // 2), 1.0, 2.0) / n
    ga = np.zeros((h1, ks))
    ga[:, 0:2 * k1n:2] = (np.cos(ang) * wgt[:, None]).T
    ga[:, 1:2 * k1n:2] = (-np.sin(ang) * wgt[:, None]).T
    k2 = np.arange(n2)
    m2 = np.arange(n2)
    kk = k1[:, None, None] + n1 * k2[None, :, None]
    th = 2.0 * np.pi * ((kk * m2[None, None, :]) % n) / n
    mc, ms = np.cos(th), np.sin(th)
    mfwd = np.concatenate([np.concatenate([mc, ms], axis=2), np.concatenate([-ms, mc], axis=2)], axis=1)
    mct, mst = np.transpose(mc, (0, 2, 1)), np.transpose(ms, (0, 2, 1))
    minv = np.concatenate([np.concatenate([mct, -mst], axis=2), np.concatenate([mst, mct], axis=2)], axis=1)
    eye = np.eye(8)
    fa8 = np.einsum('kn,jJ->jknJ', fa, eye).reshape(8 * kp, 8 * h1)
    ga8 = np.einsum('nk,jJ->njkJ', ga, eye).reshape(8 * h1, 8 * ks)
    return tuple(jnp.asarray(t, BF16) for t in (fa8, ga8, mfwd, minv))


def _hy_ctx_kernel(v_ref, x1_ref, x2_ref, fc_ref, fs_ref, gc_ref, gs_ref, ts_ref, td_ref, bias_ref, z_ref):
    d = functools.partial(jnp.dot, preferred_element_type=F32)
    w = v_ref.shape[2]
    kr_all = d(fc_ref[...], ts_ref[...].astype(BF16))
    ki_all = d(fs_ref[...], td_ref[...].astype(BF16))

    def conv(u, o):
        ub = u.astype(BF16)
        cr, ci = d(fc_ref[...], ub), d(fs_ref[...], ub)
        kr, ki = kr_all[:, o * w:(o + 1) * w], ki_all[:, o * w:(o + 1) * w]
        pr = (cr * kr - ci * ki).astype(BF16)
        pi = (cr * ki + ci * kr).astype(BF16)
        return d(gc_ref[...], pr) + d(gs_ref[...], pi) + u * bias_ref[o]

    z = x1_ref[0].astype(F32) * conv(v_ref[0].astype(F32), 0)
    z = x2_ref[0].astype(F32) * conv(z, 1)
    for j in range(2):
        z_ref[0, j] = z[:, j * 128:(j + 1) * 128]


def _hyena_ctx(v, x1, x2, fc, fs, gc, gs, tsum, tdiff, bias, seq, ctx_len):
    bsz, _, w = v.shape
    blk = seq // ctx_len
    kpad = fc.shape[0]
    tok = pl.BlockSpec((1, ctx_len, w), lambda b: (b, blk, 0))
    full = lambda shp: pl.BlockSpec(shp, lambda b: (0,) * len(shp))
    return pl.pallas_call(
        _hy_ctx_kernel, grid=(bsz,),
        in_specs=[tok, tok, tok, full((kpad, ctx_len)), full((kpad, ctx_len)), full((ctx_len, kpad)),
                  full((ctx_len, kpad)), full((ctx_len, HY_ORDER * w)), full((ctx_len, HY_ORDER * w)), full((2, 1, w))],
        out_specs=pl.BlockSpec((1, 2, ctx_len, 128), lambda b: (b, 0, 0, 0)),
        out_shape=jax.ShapeDtypeStruct((bsz, 2, ctx_len, 128), F32),
        compiler_params=_cparams(("parallel",)),
        name="hyena_ctx",
    )(v, x1, x2, fc, fs, gc, gs, tsum, tdiff, bias)


def _hyena_ctx_consts(ctx_len):
    n = 2 * ctx_len
    nk = ctx_len + 1
    kpad = -(-nk // 128) * 128
    k = np.arange(nk)
    m = np.arange(ctx_len)
    ang = 2.0 * np.pi * np.outer(k, m) / n
    fc = np.zeros((kpad, ctx_len))
    fs = np.zeros((kpad, ctx_len))
    fc[:nk] = np.cos(ang)
    fs[:nk] = -np.sin(ang)
    wgt = np.where((k == 0) | (k == ctx_len), 1.0, 2.0) / n
    gc = np.zeros((ctx_len, kpad))
    gs = np.zeros((ctx_len, kpad))
    gc[:, :nk] = (np.cos(ang) * wgt[:, None]).T
    gs[:, :nk] = (-np.sin(ang) * wgt[:, None]).T
    return tuple(jnp.asarray(t, BF16) for t in (fc, fs, gc, gs))


def _hyena_filter_taps(length, w1, b1, w2, b2, w3, freq, decay):
    pos = jnp.arange(length, dtype=F32)
    t = pos / max(length - 1, 1)
    bands = jnp.linspace(1e-4, HY_BANDS - 1, HY_BANDS, dtype=F32)
    ang = (2.0 * math.pi / length) * pos[:, None] * bands
    feats = jnp.concatenate([t[:, None], jnp.cos(ang), -jnp.sin(ang)], axis=-1)
    freq = freq.astype(F32)
    mm = functools.partial(jnp.matmul, precision=HI)
    hid = jnp.sin(freq * (mm(feats, w1.astype(F32)) + b1.astype(F32)))
    hid = jnp.sin(freq * (mm(hid, w2.astype(F32)) + b2.astype(F32)))
    h = mm(hid, w3.astype(F32)) * jnp.exp(-t[:, None] * jnp.abs(decay.astype(F32)))
    half = HY_ORDER * HY_WIDTH
    h_fwd, h_bwd = h[:, :half], h[:, half:]
    l1 = (jnp.abs(h_fwd[0] + h_bwd[0]) + jnp.sum(jnp.abs(h_fwd[1:]), axis=0) + jnp.sum(jnp.abs(h_bwd[1:]), axis=0))
    return (h_fwd + h_bwd) / l1, (h_fwd - h_bwd) / l1


def _mixout_kernel(nt_lat, n_sub, lat_only, *refs):
    if lat_only:
        (xl_ref, m_ref, s5y_ref, yf_ref, yb_ref, xs_ref, z_ref, hyl_ref,
         wglu_ref, bglu_ref, vec512_ref, wout_ref, win_ref, wout2_ref, ln_ref, o_ref) = refs
    else:
        (xl_ref, xc_ref, m_ref, s5y_ref, yf_ref, yb_ref, xs_ref, z_ref, hyl_ref, hyc_ref,
         wglu_ref, bglu_ref, vec512_ref, wout_ref, win_ref, wout2_ref, ln_ref, o_ref) = refs
        is_lat = pl.program_id(1) < nt_lat
    m = m_ref[0, 0]
    d = functools.partial(jnp.dot, preferred_element_type=F32)
    for sub in range(n_sub):
        r = slice(sub * ROW_TILE, (sub + 1) * ROW_TILE)
        if lat_only:
            x_in = xl_ref[0, r, :]
            hy = [hyl_ref[0, j, r, :].astype(BF16) for j in range(2)]
        else:
            x_in = jnp.where(is_lat, xl_ref[0, r, :], xc_ref[0, r, :])
            hy = [jnp.where(is_lat, hyl_ref[0, j, r, :], hyc_ref[0, j, r, :]).astype(BF16) for j in range(2)]
        y5 = _gelu_tanh(jnp.concatenate([s5y_ref[0, 0, r, :], s5y_ref[0, 1, r, :]], axis=1))
        y5 = y5 * _sigmoid(d(y5.astype(BF16), wglu_ref[...]) + bglu_ref[...])
        ys = (yf_ref[0, 0, r, :].astype(F32) + yb_ref[0, 0, r, :].astype(F32)
              + vec512_ref[0:1, :] * xs_ref[0, r, :].astype(F32))
        gsd = ys * _silu(z_ref[0, r, :].astype(F32))
        gsd = gsd * lax.rsqrt(jnp.mean(gsd * gsd, axis=-1, keepdims=True) + LN_EPS) * vec512_ref[1:2, :]
        mix = (d(y5.astype(BF16), wout_ref[0:256, :]) + d(gsd.astype(BF16), wout_ref[256:768, :])
               + d(hy[0], wout_ref[768:896, :]) + d(hy[1], wout_ref[896:1024, :]))
        x = _standardise(ALPHA * x_in + m[2:3] * mix) * ln_ref[0:1, :] + ln_ref[1:2, :]
        h = (_standardise(x) * (1.0 + m[4:5]) + m[3:4]).astype(BF16)
        gate = d(h, win_ref[:, 0:FFN_HIDDEN])
        up = d(h, win_ref[:, FFN_HIDDEN:2 * FFN_HIDDEN])
        act = (_silu(gate) * up).astype(BF16)
        x = _standardise(ALPHA * x + m[5:6] * d(act, wout2_ref[...]))
        o_ref[0, r, :] = x * ln_ref[2:3, :] + ln_ref[3:4, :]


def _mixout_ffn(x_lat, x_ctx, ctx_blk, mods, s5y, yssd, xbc_c, z, hy_lat, hy_ctx, wglu, bglu, vec512, wout, win,
                wout2, layer, ln, nt_lat, rows):
    bsz = x_lat.shape[0]
    lat_only = rows == nt_lat * ROW_TILE
    n_sub = MIX_SUB_TILES if lat_only and rows % (MIX_SUB_TILES * ROW_TILE) == 0 else 1
    tile = n_sub * ROW_TILE
    tok = lambda w: pl.BlockSpec((1, tile, w), lambda b, i: (b, i, 0))
    halves = pl.BlockSpec((1, 2, tile, 128), lambda b, i: (b, 0, i, 0))
    full = lambda shp: pl.BlockSpec(shp, lambda b, i: (0,) * len(shp))
    const = lambda shp: pl.BlockSpec((None,) + shp, lambda b, i: (layer, 0, 0), pipeline_mode=pl.Buffered(1))
    ssd_dir = lambda dirn: pl.BlockSpec((1, 1, tile, SSD_INNER), lambda b, i: (b, dirn, i, 0))
    if lat_only:
        x_specs, x_args = [tok(D_MODEL)], (x_lat,)
        hy_specs, hy_args = [halves], (hy_lat,)
        mod_spec = pl.BlockSpec((1, 1, 6, D_MODEL), lambda b, i: (b, 0, 0, 0))
    else:
        x_specs, x_args = _row_specs(nt_lat, ctx_blk, D_MODEL), (x_lat, x_ctx)
        hy_specs = [pl.BlockSpec((1, 2, tile, 128), lambda b, i: (b, 0, jnp.minimum(i, nt_lat - 1), 0)),
                    pl.BlockSpec((1, 2, tile, 128), lambda b, i: (b, 0, 0, 0))]
        hy_args = (hy_lat, hy_ctx)
        mod_spec = pl.BlockSpec((1, 1, 6, D_MODEL), lambda b, i: (b, jnp.where(i < nt_lat, 0, 1), 0, 0))
    return pl.pallas_call(
        functools.partial(_mixout_kernel, nt_lat, n_sub, lat_only), grid=(bsz, rows // tile),
        in_specs=x_specs + [mod_spec, halves, ssd_dir(0), ssd_dir(1), tok(SSD_INNER), tok(SSD_INNER)] + hy_specs + [
                  full((256, 256)), full((1, 256)), full((2, 512)),
                  const((D_MODEL, D_MODEL)), const((D_MODEL, 2 * FFN_HIDDEN)), const((FFN_HIDDEN, D_MODEL)),
                  full((4, D_MODEL))],
        out_specs=tok(D_MODEL),
        out_shape=jax.ShapeDtypeStruct((bsz, rows, D_MODEL), F32),
        compiler_params=_cparams(("parallel", "parallel"), V7X_VMEM_LIMIT_MB),
        name="mixout_ffn",
    )(*x_args, mods, s5y, yssd, yssd, xbc_c, z, *hy_args, wglu, bglu, vec512, wout, win, wout2, ln)


def _layer(x_lat, x_ctx, ctx_blk, mods, p, seq, ctx_len, want_ctx):
    nt_lat = seq // ROW_TILE
    t_all = seq + ctx_len
    layer = p["layer"]
    s5u, z, xbc_c, v, x1, x2, dt_raw = _inproj(x_lat, x_ctx, ctx_blk, mods, p["w_in"], layer, p["ssd_conv_w"],
                                               p["ssd_conv_b"], p["hy_conv_w"], p["hy_conv_b"], nt_lat)
    s5y = _s5_scan(s5u, p["s5_ncat"], p["s5_tz"], p["s5_mcat"], p["s5_coef"], p["s5_dvec"],
                   seq // S5_CHUNK, ctx_len // S5_CHUNK)
    yssd = _ssd_scan(xbc_c, dt_raw, p["ssd_tri"], p["ssd_par"], p["ssd_expand"],
                     seq // SSD_CHUNK, ctx_len // SSD_CHUNK)
    hy_lat = _hyena_lat(v, x1, x2, p["hy_fa"], p["hy_ga"], p["hy_mfwd"], p["hy_minv"],
                        p["hy_kr"], p["hy_ki"], p["hy_bias"], seq)
    if want_ctx:
        hy_ctx = _hyena_ctx(v, x1, x2, *p["hy_ctx_mats"], *p["hy_ctx_taps"], p["hy_bias"], seq, ctx_len)
        rows = t_all
    else:
        hy_ctx, rows = hy_lat, seq
    return _mixout_ffn(x_lat, x_ctx, ctx_blk, mods, s5y, yssd, xbc_c, z, hy_lat, hy_ctx, p["s5_wglu"], p["s5_bglu"],
                       p["ssd_vec"], p["w_out"], p["ffn_w_in"], p["ffn_w_out"], layer, p["ln"], nt_lat, rows)


def kernel(x, c, ctx, c_ctx, w_mod, b_mod, w_in, s5_lam_re, s5_lam_im, s5_log_step, s5_b_re, s5_b_im, s5_c_re, s5_c_im, s5_d, s5_w_glu, s5_b_glu, ssd_conv_w, ssd_conv_b, ssd_dt_bias, ssd_a_log, ssd_d, ssd_norm_w, hy_conv_w, hy_conv_b, hy_w1, hy_b1, hy_w2, hy_b2, hy_w3, hy_freq, hy_decay, hy_bias, w_out, ln1_g, ln1_b, ffn_w_in, ffn_w_out, ln2_g, ln2_b):
    bsz, seq, _ = x.shape
    ctx_len = ctx.shape[1]
    assert bsz == 8 and seq % ROW_TILE == 0 and ctx_len == ROW_TILE

    fa, ga, mfwd, minv = _hyena_lat_consts(seq)
    ctx_mats = _hyena_ctx_consts(ctx_len)
    tt = np.arange(SSD_CHUNK)
    tri = jnp.asarray(np.stack([tt[None, :] <= tt[:, None], tt[None, :] >= tt[:, None]]), BF16)
    expand = jnp.asarray(np.repeat(np.eye(128, SSD_HEADS), SSD_HEADDIM, axis=1)[:, :SSD_INNER], BF16)
    cvec = jnp.zeros((16, D_MODEL), F32).at[:bsz].set(c.astype(F32)).at[bsz].set(c_ctx.astype(F32))

    w_p = jnp.concatenate([w_in[..., 0:1792], w_in[..., 1800:2568], w_in[..., 1792:1800],
                           jnp.zeros((DEPTH, D_MODEL, MIX_IN_PAD - 2568), w_in.dtype)], axis=-1).astype(BF16)
    w_out_b, ffn_in_b, ffn_out_b = w_out.astype(BF16), ffn_w_in.astype(BF16), ffn_w_out.astype(BF16)

    x_lat, x_ctx, ctx_blk = x.astype(F32), ctx.astype(F32), 0
    for l in range(DEPTH):
        want_ctx = l < DEPTH - 1
        mod16 = _modulation(cvec, w_mod.astype(F32), b_mod[l].astype(F32).reshape(1, -1), l)
        mods = jnp.stack([mod16[:bsz].reshape(bsz, 6, D_MODEL),
                          jnp.broadcast_to(mod16[bsz].reshape(1, 6, D_MODEL), (bsz, 6, D_MODEL))], axis=1)
        ncat, tz, mcat, coef = _s5_weights(s5_lam_re[l], s5_lam_im[l], s5_log_step[l], s5_b_re[l], s5_b_im[l],
                                           s5_c_re[l], s5_c_im[l])
        rep = lambda t: jnp.repeat(t.astype(F32), SSD_HEADDIM, axis=-1)
        par = jnp.zeros((2, 8, 128), F32)
        par = par.at[:, 0, :SSD_HEADS].set(ssd_dt_bias[l].astype(F32))
        par = par.at[:, 1, :SSD_HEADS].set(-jnp.exp(ssd_a_log[l].astype(F32)))
        hy_args = (hy_w1[l], hy_b1[l], hy_w2[l], hy_b2[l], hy_w3[l], hy_freq[l], hy_decay[l])
        kr, ki = _hyena_spectrum(*_hyena_filter_taps(seq, *hy_args), fa, mfwd)
        p = dict(
            layer=l, w_in=w_p, s5_ncat=ncat, s5_tz=tz, s5_mcat=mcat, s5_coef=coef,
            ssd_conv_w=ssd_conv_w[l].astype(F32), ssd_conv_b=ssd_conv_b[l].astype(F32),
            ssd_tri=tri, ssd_par=par, ssd_expand=expand,
            hy_conv_w=hy_conv_w[l].astype(F32), hy_conv_b=hy_conv_b[l].astype(F32),
            hy_fa=fa, hy_ga=ga, hy_mfwd=mfwd, hy_minv=minv, hy_kr=kr, hy_ki=ki,
            hy_bias=hy_bias[l].astype(F32)[:, None, :],
            s5_wglu=s5_w_glu[l].astype(BF16),
            s5_bglu=s5_b_glu[l].astype(F32).reshape(1, S5_WIDTH),
            s5_dvec=jnp.tile(s5_d[l].astype(F32).reshape(S5_GROUPS, 1, S5_GROUP_CH), (1, 1, S5_CHUNK)),
            ssd_vec=jnp.stack([rep(ssd_d[l]), ssd_norm_w[l].astype(F32)]),
            w_out=w_out_b, ln=jnp.stack([ln1_g[l], ln1_b[l], ln2_g[l], ln2_b[l]]).astype(F32),
            ffn_w_in=ffn_in_b, ffn_w_out=ffn_out_b,
        )
        if want_ctx:
            p.update(hy_ctx_mats=ctx_mats, hy_ctx_taps=_hyena_filter_taps(ctx_len, *hy_args))
        x_lat = _layer(x_lat, x_ctx, ctx_blk, mods, p, seq, ctx_len, want_ctx)
        x_ctx, ctx_blk = x_lat, seq // ROW_TILE
    return x_lat.astype(x.dtype)
```

```python
import functools
import math

import numpy as np
import jax
import jax.numpy as jnp
from jax import lax
from jax.experimental import pallas as pl
from jax.experimental.pallas import tpu as pltpu

F32 = jnp.float32
BF16 = jnp.bfloat16
HI = lax.Precision.HIGHEST

D_MODEL = 1024
DEPTH = 2
S5_WIDTH = 256
S5_GROUP_CH = 16
S5_GROUPS = 16
S5_MAX_RE = -1e-4
S5_CHUNK = 16
SSD_INNER = 512
SSD_HEADDIM = 64
SSD_HEADS = 8
SSD_GROUPS = 2
SSD_STATE = 128
SSD_CONV = 5
SSD_CHUNK = 128
SSD_CHUNKS_PER_STEP = 2
SSD_XBC = 1024
HY_WIDTH = 256
HY_ORDER = 2
HY_SHORT = 3
HY_BANDS = 16
HY_IN = 768
COL_Z = S5_WIDTH
COL_XBC = COL_Z + SSD_INNER
COL_HY = COL_XBC + SSD_XBC
COL_DT = COL_HY + HY_IN
MIX_IN_PAD = COL_DT + 128
REF_COL_DT = COL_HY
REF_COL_HY = REF_COL_DT + SSD_HEADS
MIX_IN = REF_COL_HY + HY_IN
FFN_HIDDEN = 2816
ALPHA = (2 * DEPTH) ** 0.25
LN_EPS = 1e-6

ROW_TILE = 256
MIX_SUB_TILES = 2
CONV_GROUP = 256
HALO = 16
FFT_N2 = 128
HY_A_PITCH = 72
HY_K1_PER_STEP = 11
HY_B_PITCH = 136
V7X_VMEM_LIMIT_MB = 56


def _cparams(sem, vmem_mb=None):
    kw = dict(dimension_semantics=sem)
    if vmem_mb is not None:
        kw["vmem_limit_bytes"] = vmem_mb * 2 ** 20
    return pltpu.CompilerParams(**kw)


def _standardise(x):
    mu = jnp.mean(x, axis=-1, keepdims=True)
    xc = x - mu
    var = jnp.mean(xc * xc, axis=-1, keepdims=True)
    return xc * lax.rsqrt(var + LN_EPS)


def _sigmoid(x):
    return 1.0 / (1.0 + jnp.exp(-x))


def _silu(x):
    return x * _sigmoid(x)


def _gelu_tanh(x):
    return 0.5 * x * (1.0 + jnp.tanh(0.7978845608028654 * (x + 0.044715 * (x * x * x))))


def _softplus(x):
    return jnp.maximum(x, 0.0) + jnp.log(1.0 + jnp.exp(-jnp.abs(x)))


def _split3(a):
    a1 = a.astype(BF16)
    r1 = a - a1.astype(F32)
    a2 = r1.astype(BF16)
    a3 = (r1 - a2.astype(F32)).astype(BF16)
    return a1, a2, a3


def _dot_sel_lhs(sel, a):
    a1, a2, a3 = _split3(a)
    d = functools.partial(jnp.dot, preferred_element_type=F32)
    return d(sel, a1) + d(sel, a2) + d(sel, a3)


def _mod_kernel(c_ref, w_ref, b_ref, o_ref):
    ca = _silu(c_ref[...])
    c1, c2, c3 = _split3(ca)
    w1, w2, w3 = _split3(w_ref[...])
    d = functools.partial(jnp.dot, preferred_element_type=F32)
    acc = d(c1, w1) + d(c1, w2) + d(c2, w1) + d(c1, w3) + d(c2, w2) + d(c3, w1)
    o_ref[...] = acc + b_ref[...]


def _modulation(cvec, w, b, layer):
    n = w.shape[2]
    tn = 1536
    return pl.pallas_call(
        _mod_kernel,
        grid=(n // tn,),
        in_specs=[pl.BlockSpec((16, D_MODEL), lambda j: (0, 0)),
                  pl.BlockSpec((None, D_MODEL, tn), lambda j: (layer, 0, j)),
                  pl.BlockSpec((1, tn), lambda j: (0, j))],
        out_specs=pl.BlockSpec((16, tn), lambda j: (0, j)),
        out_shape=jax.ShapeDtypeStruct((16, n), F32),
        compiler_params=_cparams(("arbitrary",), 40),
        name="modulation",
    )(cvec, w, b)


def _rows_of(nt_lat, xl_ref, xc_ref):
    return jnp.where(pl.program_id(1) < nt_lat, xl_ref[0], xc_ref[0])


def _row_specs(nt_lat, ctx_blk, width):
    return [pl.BlockSpec((1, ROW_TILE, width), lambda b, i: (b, jnp.minimum(i, nt_lat - 1), 0)),
            pl.BlockSpec((1, ROW_TILE, width), lambda b, i: (b, ctx_blk, 0))]


def _dwconv_rows(ext_ref, w_ref, b_ref, wcol, taps, act, o_ref, ocol):
    pad = taps // 2
    rb = 64
    for c in range(0, CONV_GROUP, 128):
        wk = [w_ref[k:k + 1, wcol + c:wcol + c + 128] for k in range(taps)]
        bias = b_ref[0:1, wcol + c:wcol + c + 128]
        for r0 in range(0, ROW_TILE, rb):
            base = HALO - pad + r0
            acc = bias + wk[0] * ext_ref[base:base + rb, c:c + 128]
            for k in range(1, taps):
                acc = acc + wk[k] * ext_ref[base + k:base + k + rb, c:c + 128]
            if act:
                acc = _silu(acc)
            o_ref[0, r0:r0 + rb, ocol + c:ocol + c + 128] = acc.astype(o_ref.dtype)


def _inproj_kernel(nt_lat, xl_ref, xc_ref, xp_ref, xn_ref, m_ref, w_ref, w5_ref, b5_ref, w3_ref, b3_ref,
                   s5_ref, z_ref, xbc_ref, v_ref, x1_ref, x2_ref, dt_ref, *ext_refs):
    i = pl.program_id(1)
    m = m_ref[0, 0]
    mod = lambda x: (_standardise(x) * (1.0 + m[1:2]) + m[0:1]).astype(BF16)
    d = functools.partial(jnp.dot, preferred_element_type=F32)
    h = mod(_rows_of(nt_lat, xl_ref, xc_ref))
    hp, hn = mod(xp_ref[0]), mod(xn_ref[0])
    has_prev = jnp.logical_and(i > 0, i < nt_lat)
    has_next = i < nt_lat - 1
    n5 = SSD_XBC // CONV_GROUP
    hy_outs = (v_ref, x1_ref, x2_ref)

    def project(k):
        wc = w_ref[:, COL_XBC + k * CONV_GROUP:COL_XBC + (k + 1) * CONV_GROUP]
        e = ext_refs[k]
        e[0:HALO, :] = jnp.where(has_prev, d(hp, wc), 0.0)
        e[HALO:HALO + ROW_TILE, :] = d(h, wc)
        e[HALO + ROW_TILE:2 * HALO + ROW_TILE, :] = jnp.where(has_next, d(hn, wc), 0.0)

    def conv(k):
        if k < n5:
            _dwconv_rows(ext_refs[k], w5_ref, b5_ref, k * CONV_GROUP, SSD_CONV, True, xbc_ref, k * CONV_GROUP)
        else:
            _dwconv_rows(ext_refs[k], w3_ref, b3_ref, (k - n5) * CONV_GROUP, HY_SHORT, False, hy_outs[k - n5], 0)

    ngroups = len(ext_refs)
    project(0)
    for k in range(ngroups):
        if k + 1 < ngroups:
            project(k + 1)
        conv(k)
    s5_ref[0] = d(h, w_ref[:, 0:COL_Z])
    z_ref[0] = d(h, w_ref[:, COL_Z:COL_XBC]).astype(z_ref.dtype)
    dt_ref[0] = d(h, w_ref[:, COL_DT:MIX_IN_PAD])


def _inproj(x_lat, x_ctx, ctx_blk, mods, w_p, layer, conv5_w, conv5_b, conv3_w, conv3_b, nt_lat):
    bsz = x_lat.shape[0]
    nt = nt_lat + 1
    t_all = nt * ROW_TILE
    rh = ROW_TILE // HALO
    last = nt_lat * rh - 1
    full = lambda shp: pl.BlockSpec(shp, lambda b, i: (0,) * len(shp))
    outs = ((256, F32), (512, BF16), (SSD_XBC, BF16), (HY_WIDTH, BF16), (HY_WIDTH, BF16), (HY_WIDTH, BF16), (128, F32))
    return pl.pallas_call(
        functools.partial(_inproj_kernel, nt_lat),
        grid=(bsz, nt),
        in_specs=_row_specs(nt_lat, ctx_blk, D_MODEL) + [
                  pl.BlockSpec((1, HALO, D_MODEL), lambda b, i: (b, jnp.clip(i * rh - 1, 0, last), 0)),
                  pl.BlockSpec((1, HALO, D_MODEL), lambda b, i: (b, jnp.clip((i + 1) * rh, 0, last), 0)),
                  pl.BlockSpec((1, 1, 6, D_MODEL), lambda b, i: (b, jnp.where(i < nt_lat, 0, 1), 0, 0)),
                  pl.BlockSpec((None, D_MODEL, MIX_IN_PAD), lambda b, i: (layer, 0, 0)), full((SSD_CONV, SSD_XBC)), full((1, SSD_XBC)),
                  full((HY_SHORT, HY_IN)), full((1, HY_IN))],
        out_specs=[pl.BlockSpec((1, ROW_TILE, w), lambda b, i: (b, i, 0)) for w, _ in outs],
        out_shape=[jax.ShapeDtypeStruct((bsz, t_all, w), dt) for w, dt in outs],
        scratch_shapes=[pltpu.VMEM((ROW_TILE + 2 * HALO, CONV_GROUP), F32)] * ((SSD_XBC + HY_IN) // CONV_GROUP),
        compiler_params=_cparams(("parallel", "parallel"), 40),
        name="inproj",
    )(x_lat, x_ctx, x_lat, x_lat, mods, w_p, conv5_w, conv5_b.reshape(1, -1), conv3_w, conv3_b.reshape(1, -1))


def _s5_kernel(n_lat, n_ctx, u0_ref, u1_ref, ncat_ref, tz_ref, mcat_ref, coef_ref, dvec_ref, y_ref,
               ut_ref, s_ref, hp_ref):
    q, gch, ng = S5_CHUNK, S5_GROUP_CH, S5_GROUPS
    nc = n_lat + n_ctx
    dot = functools.partial(jnp.dot, preferred_element_type=F32)
    u_refs = (u0_ref, u1_ref)
    gpl = 128 // gch
    for j in range(2):
        for s in range(q):
            rows = u_refs[j][0, pl.ds(s, nc, stride=q), :]
            for gg in range(gpl):
                ut_ref[gpl * j + gg, :, s * gch:(s + 1) * gch] = rows[:, gg * gch:(gg + 1) * gch]
    for g in range(ng):
        sg = dot(ut_ref[g].astype(BF16), ncat_ref[g])
        for k in range(4):
            s_ref[k, pl.ds(g, nc, stride=ng), :] = sg[:, k * 128:(k + 1) * 128]
    c1f, c2f, c1b, c2b = coef_ref[0], coef_ref[1], coef_ref[2], coef_ref[3]

    def step(cf, cb, carry):
        hf, hsf, hb, hsb = carry
        rf = pl.multiple_of(cf * ng, ng)
        rb = pl.multiple_of(cb * ng, ng)
        hp_ref[0, pl.ds(rf, ng), :] = hf
        hp_ref[1, pl.ds(rb, ng), :] = hb
        sf = s_ref[0, pl.ds(rf, ng), :]
        sb = s_ref[1, pl.ds(rb, ng), :]
        ssf = s_ref[2, pl.ds(rf, ng), :]
        ssb = s_ref[3, pl.ds(rb, ng), :]
        return (c1f * hf + c2f * hsf + sf, c1f * hsf - c2f * hf + ssf,
                c1b * hb + c2b * hsb + sb, c1b * hsb - c2b * hb + ssb)

    z = jnp.zeros((ng, 128), F32)
    carry = lax.fori_loop(0, n_ctx, lambda i, c: step(n_lat + i, n_lat + n_ctx - 1 - i, c), (z, z, z, z))
    lax.fori_loop(0, n_lat, lambda i, c: step(i, n_lat - 1 - i, c), carry)
    for g in range(ng):
        ug = ut_ref[g]
        hp = jnp.concatenate([hp_ref[0, pl.ds(g, nc, stride=ng), :], hp_ref[1, pl.ds(g, nc, stride=ng), :]], axis=1)
        ut_ref[g] = dot(ug.astype(BF16), tz_ref[g]) + dot(hp.astype(BF16), mcat_ref[g]) + ug * dvec_ref[g]
    for j in range(2):
        for s in range(q):
            rows = jnp.concatenate([ut_ref[gpl * j + gg, :, s * gch:(s + 1) * gch] for gg in range(gpl)], axis=1)
            y_ref[0, j, pl.ds(s, nc, stride=q), :] = rows


def _s5_scan(s5u, ncat, tz, mcat, coef, dvec, n_lat, n_ctx):
    bsz, t_all, _ = s5u.shape
    nc = n_lat + n_ctx
    ng = S5_GROUPS
    const = lambda shp: pl.BlockSpec(shp, lambda b: (0,) * len(shp), pipeline_mode=pl.Buffered(1))
    return pl.pallas_call(
        functools.partial(_s5_kernel, n_lat, n_ctx),
        grid=(bsz,),
        in_specs=[pl.BlockSpec((1, t_all, 128), lambda b: (b, 0, 0)),
                  pl.BlockSpec((1, t_all, 128), lambda b: (b, 0, 1)),
                  const((ng, 256, 512)), const((ng, 256, 256)), const((ng, 256, 256)),
                  const((4, ng, 128)), const((ng, 1, 256))],
        out_specs=pl.BlockSpec((1, 2, t_all, 128), lambda b: (b, 0, 0, 0)),
        out_shape=jax.ShapeDtypeStruct((bsz, 2, t_all, 128), F32),
        scratch_shapes=[pltpu.VMEM((ng, nc, 256), F32), pltpu.VMEM((4, nc * ng, 128), F32),
                        pltpu.VMEM((2, nc * ng, 128), F32)],
        compiler_params=_cparams(("parallel",), V7X_VMEM_LIMIT_MB),
        name="s5_scan",
    )(s5u, s5u, ncat, tz, mcat, coef, dvec)


def _s5_weights(lam_re, lam_im, log_step, b_re, b_im, c_re, c_im):
    q, ng, gch = S5_CHUNK, S5_GROUPS, S5_GROUP_CH
    a_re = jnp.minimum(lam_re.astype(F32), S5_MAX_RE)
    a_im = lam_im.astype(F32)
    step = jnp.exp(log_step.astype(F32))[..., None]
    taus = jnp.arange(q + 1, dtype=F32)[:, None, None, None]
    mag = jnp.exp(taus * (a_re * step))
    pr, pi = mag * jnp.cos(taus * (a_im * step)), mag * jnp.sin(taus * (a_im * step))
    nr, ni, den = pr[1] - 1.0, pi[1], a_re * a_re + a_im * a_im
    fr, fi = ((nr * a_re + ni * a_im) / den)[..., None], ((ni * a_re - nr * a_im) / den)[..., None]
    b_r, b_i = b_re.astype(F32), b_im.astype(F32)
    bb_r, bb_i = fr * b_r - fi * b_i, fr * b_i + fi * b_r
    c_r = jnp.swapaxes(c_re.astype(F32), -1, -2)
    c_i = jnp.swapaxes(c_im.astype(F32), -1, -2)
    pw_r, pw_i = jnp.transpose(pr, (1, 2, 3, 0)), jnp.transpose(pi, (1, 2, 3, 0))

    def lagged(v_r, v_i, d, lags):
        e_r, e_i = jnp.repeat(pw_r[d][..., lags], gch, axis=-1), jnp.repeat(pw_i[d][..., lags], gch, axis=-1)
        n = e_r.shape[-1] // gch
        t_r, t_i = jnp.tile(v_r[d], (1, 1, n)), jnp.tile(v_i[d], (1, 1, n))
        return t_r * e_r - t_i * e_i, t_r * e_i + t_i * e_r

    def lag_kernels(d, lags):
        cp_r, cp_i = lagged(c_r, c_i, d, lags)
        return (jnp.einsum('gpk,gpn->gkn', bb_r[d], cp_r, precision=HI)
                - jnp.einsum('gpk,gpn->gkn', bb_i[d], cp_i, precision=HI))

    k_fwd = lag_kernels(0, slice(0, q))
    k_bwd = lag_kernels(1, slice(q - 1, None, -1))
    edge = (q - 1) * gch
    kfull = jnp.concatenate([k_bwd[..., :edge], k_bwd[..., edge:] + k_fwd[..., :gch], k_fwd[..., gch:]], axis=-1)
    tz = jnp.stack([kfull[:, :, (q - 1 - s) * gch:(2 * q - 1 - s) * gch] for s in range(q)],
                   axis=1).reshape(ng, q * gch, q * gch)
    nf_r, nf_i = (jnp.swapaxes(t, -1, -2) for t in lagged(bb_r, bb_i, 0, slice(q - 1, None, -1)))
    nb_r, nb_i = (jnp.swapaxes(t, -1, -2) for t in lagged(bb_r, bb_i, 1, slice(0, q)))
    ncat = jnp.concatenate([nf_r, nf_i, nb_r, nb_i, nf_i, nf_r, nb_i, nb_r], axis=-1)
    mf_r, mf_i = lagged(c_r, c_i, 0, slice(1, None))
    mb_r, mb_i = lagged(c_r, c_i, 1, slice(q, 0, -1))
    mcat = jnp.concatenate([mf_r, -mf_i, mb_r, -mb_i], axis=1)
    lr, li = pr[q], pi[q]
    coef = jnp.stack([jnp.concatenate([lr[0], lr[0]], axis=-1), jnp.concatenate([-li[0], li[0]], axis=-1),
                      jnp.concatenate([lr[1], lr[1]], axis=-1), jnp.concatenate([-li[1], li[1]], axis=-1)])
    return ncat.astype(BF16), tz.astype(BF16), mcat.astype(BF16), coef.astype(F32)


def _ssd_kernel(cps, xbc_ref, dt_ref, tri_ref, par_ref, exp_ref, y_ref, st_ref):
    direction = pl.program_id(1)

    @pl.when(pl.program_id(2) == 0)
    def _():
        st_ref[...] = jnp.zeros_like(st_ref)

    tri = tri_ref[0]
    mask = tri > 0
    lane = lax.broadcasted_iota(jnp.int32, (SSD_CHUNK, 128), 1)
    lo = lane < SSD_HEADDIM
    zero_b = jnp.zeros((SSD_CHUNK, 128), BF16)
    head_of_lane = lax.broadcasted_iota(jnp.int32, (1, SSD_INNER), 1) // SSD_HEADDIM
    expand = lambda t: jnp.dot(t.astype(BF16), exp_ref[...], preferred_element_type=F32)
    state = [st_ref[:, g * 256:(g + 1) * 256] for g in range(SSD_GROUPS)]
    for c in range(cps):
        sub = jnp.where(direction == 0, c, cps - 1 - c)
        rows = pl.ds(pl.multiple_of(sub * SSD_CHUNK, SSD_CHUNK), SSD_CHUNK)
        xs = xbc_ref[0, rows, 0:512].astype(F32)
        dt_c = _softplus(dt_ref[0, rows, :] + par_ref[0, 0:1, :])
        a_c = par_ref[0, 1:2, :] * dt_c
        cs = _dot_sel_lhs(tri, a_c)
        tot = jnp.sum(a_c, axis=0, keepdims=True)
        cs_t = cs.T
        xdt = xs * expand(dt_c)
        xdt_b = xdt.astype(BF16)
        xd_end = (xdt * expand(jnp.exp(tot - cs))).astype(BF16)
        e_cs = expand(jnp.exp(cs))
        tot_full = jnp.zeros((1, SSD_INNER), F32)
        for head in range(SSD_HEADS):
            tot_full = jnp.where(head_of_lane == head, tot[:, head:head + 1], tot_full)
        e_tot = jnp.exp(tot_full)
        for g in range(SSD_GROUPS):
            bm_b = xbc_ref[0, rows, 512 + g * 128:512 + (g + 1) * 128]
            cm_b = xbc_ref[0, rows, 768 + g * 128:768 + (g + 1) * 128]
            cb = lax.dot_general(cm_b, bm_b, (((1,), (1,)), ((), ())), preferred_element_type=F32)
            y_off = jnp.dot(cm_b, state[g].astype(BF16), preferred_element_type=F32) * e_cs[:, g * 256:(g + 1) * 256]
            for j in range(2):
                c0 = g * 256 + j * 128
                x_pair = xdt_b[:, c0:c0 + 128]
                acc = y_off[:, j * 128:(j + 1) * 128]
                for hh in range(2):
                    head = c0 // SSD_HEADDIM + hh
                    decay = jnp.exp(jnp.where(mask, cs[:, head:head + 1] - cs_t[head:head + 1, :], -1e30))
                    gm = (cb * decay).astype(BF16)
                    xh = jnp.where(lo if hh == 0 else jnp.logical_not(lo), x_pair, zero_b)
                    acc = acc + jnp.dot(gm, xh, preferred_element_type=F32)
                y_ref[0, 0, rows, c0:c0 + 128] = acc.astype(y_ref.dtype)
            state[g] = (state[g] * e_tot[:, g * 256:(g + 1) * 256]
                        + jnp.dot(bm_b.astype(F32).T.astype(BF16), xd_end[:, g * 256:(g + 1) * 256],
                                  preferred_element_type=F32))
    for g in range(SSD_GROUPS):
        st_ref[:, g * 256:(g + 1) * 256] = state[g]


def _ssd_scan(xbc_c, dt_raw, tri, par, expand, n_lat, n_ctx):
    bsz, t_all, _ = xbc_c.shape
    cps = SSD_CHUNKS_PER_STEP
    assert n_lat % cps == 0 and n_ctx % cps == 0
    n_lat, n_ctx = n_lat // cps, n_ctx // cps
    nc = n_lat + n_ctx
    rows = cps * SSD_CHUNK

    def blk(d, i):
        fwd = jnp.where(i < n_ctx, n_lat + i, i - n_ctx)
        return jnp.where(d == 0, fwd, nc - 1 - i)

    return pl.pallas_call(
        functools.partial(_ssd_kernel, cps),
        grid=(bsz, 2, nc),
        in_specs=[pl.BlockSpec((1, rows, SSD_XBC), lambda b, d, i: (b, blk(d, i), 0)),
                  pl.BlockSpec((1, rows, 128), lambda b, d, i: (b, blk(d, i), 0)),
                  pl.BlockSpec((1, SSD_CHUNK, SSD_CHUNK), lambda b, d, i: (d, 0, 0)),
                  pl.BlockSpec((1, 8, 128), lambda b, d, i: (d, 0, 0)),
                  pl.BlockSpec((128, SSD_INNER), lambda b, d, i: (0, 0))],
        out_specs=pl.BlockSpec((1, 1, rows, SSD_INNER), lambda b, d, i: (b, d, blk(d, i), 0)),
        out_shape=jax.ShapeDtypeStruct((bsz, 2, t_all, SSD_INNER), BF16),
        scratch_shapes=[pltpu.VMEM((SSD_STATE, SSD_INNER), F32)],
        compiler_params=_cparams(("parallel", "parallel", "arbitrary")),
        name="ssd_scan",
    )(xbc_c, dt_raw, tri, par, expand)


def _hy_outer_forward(tile, fa_ref, a2_ref, h1, kp):
    dot = functools.partial(jnp.dot, preferred_element_type=F32)
    fa = fa_ref[...]

    def body(i8, carry):
        r0 = pl.multiple_of(i8 * 8, 8)
        x = jnp.concatenate([tile(n1, r0) for n1 in range(h1)], axis=0)
        out = dot(fa, x.astype(BF16))
        for j in range(8):
            row = pl.multiple_of((i8 * 8 + j) * HY_A_PITCH, 8)
            for s in range(2):
                a2_ref[s, pl.ds(row, kp), :] = out[j * kp:(j + 1) * kp, s * 128:(s + 1) * 128]
        return carry

    lax.fori_loop(0, FFT_N2 // 8, body, 0, unroll=2)


def _hy_lat_kernel(k1n, kgrp, v_ref, x1_ref, x2_ref, fa_ref, ga_ref, mf_ref, mi_ref, kr_ref, ki_ref, bias_ref,
                   o_ref, z_ref, a2_ref, b2_ref):
    order = pl.program_id(1)
    ph = pl.program_id(2)
    n2 = FFT_N2
    kp, h1 = fa_ref.shape[0] // 8, fa_ref.shape[1] // 8
    ks = ga_ref.shape[1] // 8
    seq = h1 * n2
    ngroups = k1n // kgrp
    dot = functools.partial(jnp.dot, preferred_element_type=F32)

    @pl.when(jnp.logical_and(order == 0, ph == 0))
    def _():
        for j in range(2):
            z_ref[j] = v_ref[0, :, j * 128:(j + 1) * 128].astype(F32)
            b2_ref[j, 2 * k1n * HY_B_PITCH:ks * HY_B_PITCH, :] = jnp.zeros(((ks - 2 * k1n) * HY_B_PITCH, 128), F32)

    @pl.when(ph == 0)
    def _():
        tile = lambda n1, r0: jnp.concatenate([z_ref[j, pl.ds(n1 * n2 + r0, 8), :] for j in range(2)], axis=1)
        _hy_outer_forward(tile, fa_ref, a2_ref, h1, kp)

    @pl.when(jnp.logical_and(ph >= 1, ph <= ngroups))
    def _():
        for t in range(kgrp):
            k1 = (ph - 1) * kgrp + t
            are = jnp.concatenate([a2_ref[j, pl.ds(2 * k1, n2, stride=HY_A_PITCH), :] for j in range(2)], axis=1)
            aim = jnp.concatenate([a2_ref[j, pl.ds(2 * k1 + 1, n2, stride=HY_A_PITCH), :] for j in range(2)], axis=1)
            x = dot(mf_ref[k1], jnp.concatenate([are, aim], axis=0).astype(BF16))
            xr, xi = x[:n2], x[n2:]
            kr, ki = kr_ref[t].astype(F32), ki_ref[t].astype(F32)
            y = jnp.concatenate([xr * kr - xi * ki, xr * ki + xi * kr], axis=0).astype(BF16)
            bq = dot(mi_ref[k1], y)
            row = pl.multiple_of(2 * k1 * HY_B_PITCH, 8)
            for j in range(2):
                b2_ref[j, pl.ds(row, n2), :] = bq[:n2, j * 128:(j + 1) * 128]
                b2_ref[j, pl.ds(row + HY_B_PITCH, n2), :] = bq[n2:, j * 128:(j + 1) * 128]

    @pl.when(ph == ngroups + 1)
    def _():
        ga = ga_ref[...]

        def body(i8, carry):
            r0 = pl.multiple_of(i8 * 8, 8)
            rows = jnp.concatenate(
                [jnp.concatenate([b2_ref[j, pl.ds(k * HY_B_PITCH + r0, 8), :] for j in range(2)], axis=1)
                 for k in range(ks)], axis=0)
            y = dot(ga, rows.astype(BF16))
            for n1 in range(h1):
                for j in range(2):
                    o_ref[0, j, pl.ds(n1 * n2 + r0, 8), :] = y[n1 * 8:(n1 + 1) * 8, j * 128:(j + 1) * 128]
            return carry

        lax.fori_loop(0, n2 // 8, body, 0, unroll=2)
        rb = min(256, seq)

        def gate(i, carry):
            r0 = pl.multiple_of(i * rb, 8)
            for j in range(2):
                conv = o_ref[0, j, pl.ds(r0, rb), :] + z_ref[j, pl.ds(r0, rb), :] * bias_ref[0, :, j * 128:(j + 1) * 128]
                x1 = x1_ref[0, pl.ds(r0, rb), j * 128:(j + 1) * 128].astype(F32)
                x2 = x2_ref[0, pl.ds(r0, rb), j * 128:(j + 1) * 128].astype(F32)
                z_ref[j, pl.ds(r0, rb), :] = x1 * conv
                o_ref[0, j, pl.ds(r0, rb), :] = x2 * conv
            return carry

        lax.fori_loop(0, seq // rb, gate, 0)


def _hy_spec_kernel(k1n, ts_ref, td_ref, fa_ref, mf_ref, kr_ref, ki_ref, a2_ref):
    n2 = FFT_N2
    kp, h1 = fa_ref.shape[0] // 8, fa_ref.shape[1] // 8
    dot = functools.partial(jnp.dot, preferred_element_type=F32)
    tile = lambda n1, r0: jnp.concatenate([ts_ref[pl.ds(n1 * n2 + r0, 8), :], td_ref[pl.ds(n1 * n2 + r0, 8), :]], axis=1)
    _hy_outer_forward(tile, fa_ref, a2_ref, h1, kp)

    def per_k1(k1, carry):
        are = jnp.concatenate([a2_ref[j, pl.ds(2 * k1, n2, stride=HY_A_PITCH), :] for j in range(2)], axis=1)
        aim = jnp.concatenate([a2_ref[j, pl.ds(2 * k1 + 1, n2, stride=HY_A_PITCH), :] for j in range(2)], axis=1)
        x = dot(mf_ref[k1], jnp.concatenate([are, aim], axis=0).astype(BF16))
        kr_ref[k1] = x[:n2, 0:128].astype(kr_ref.dtype)
        ki_ref[k1] = x[n2:, 128:256].astype(ki_ref.dtype)
        return carry

    lax.fori_loop(0, k1n, per_k1, 0)


def _hyena_spectrum(tsum, tdiff, fa, mfwd):
    seq, lanes = tsum.shape
    n2 = FFT_N2
    k1n = mfwd.shape[0]
    taps = pl.BlockSpec((seq, 128), lambda j: (0, j))
    out = pl.BlockSpec((k1n, n2, 128), lambda j: (0, 0, j))
    return pl.pallas_call(
        functools.partial(_hy_spec_kernel, k1n),
        grid=(lanes // 128,),
        in_specs=[taps, taps, pl.BlockSpec(fa.shape, lambda j: (0, 0)),
                  pl.BlockSpec((k1n, 2 * n2, 2 * n2), lambda j: (0, 0, 0))],
        out_specs=[out, out],
        out_shape=[jax.ShapeDtypeStruct((k1n, n2, lanes), BF16)] * 2,
        scratch_shapes=[pltpu.VMEM((2, n2 * HY_A_PITCH, 128), F32)],
        compiler_params=_cparams(("parallel",), 40),
        name="hyena_spectrum",
    )(tsum, tdiff, fa, mfwd)


def _hyena_lat(v, x1, x2, fa, ga, mfwd, minv, kr, ki, bias, seq):
    bsz, t_all, w = v.shape
    n2 = FFT_N2
    ks = ga.shape[1] // 8
    k1n = mfwd.shape[0]
    kgrp = next(g for g in (HY_K1_PER_STEP, 3, 1) if k1n % g == 0)
    ngroups = k1n // kgrp
    tok = pl.BlockSpec((1, seq, w), lambda b, o, p: (b, 0, 0), pipeline_mode=pl.Buffered(1))
    full = lambda shp: pl.BlockSpec(shp, lambda b, o, p: (0,) * len(shp))
    grp = lambda p: jnp.clip(p - 1, 0, ngroups - 1)
    mat = pl.BlockSpec((k1n, 2 * n2, 2 * n2), lambda b, o, p: (0, 0, 0), pipeline_mode=pl.Buffered(1))
    spec = pl.BlockSpec((kgrp, n2, w), lambda b, o, p: (grp(p), 0, o))
    return pl.pallas_call(
        functools.partial(_hy_lat_kernel, k1n, kgrp),
        grid=(bsz, HY_ORDER, ngroups + 2),
        in_specs=[tok, tok, tok, full(fa.shape), full(ga.shape), mat, mat, spec, spec,
                  pl.BlockSpec((1, 1, w), lambda b, o, p: (o, 0, 0))],
        out_specs=pl.BlockSpec((1, 2, seq, 128), lambda b, o, p: (b, 0, 0, 0)),
        out_shape=jax.ShapeDtypeStruct((bsz, 2, seq, 128), F32),
        scratch_shapes=[pltpu.VMEM((2, seq, 128), F32),
                        pltpu.VMEM((2, n2 * HY_A_PITCH, 128), F32),
                        pltpu.VMEM((2, ks * HY_B_PITCH, 128), F32)],
        compiler_params=_cparams(("parallel", "arbitrary", "arbitrary"), V7X_VMEM_LIMIT_MB),
        name="hyena_lat",
    )(v, x1, x2, fa, ga, mfwd, minv, kr, ki, bias)


def _hyena_lat_consts(seq):
    n = 2 * seq
    n2 = FFT_N2
    n1 = n // n2
    h1 = n1 // 2
    k1n = n1 // 2 + 1
    kp = -(-2 * k1n // 8) * 8
    ks = -(-2 * k1n // 16) * 16
    assert kp <= HY_A_PITCH and n2 <= HY_B_PITCH
    k1 = np.arange(k1n)
    m1 = np.arange(h1)
    ang = 2.0 * np.pi * np.outer(k1, m1) / n1
    fa = np.zeros((kp, h1))
    fa[0:2 * k1n:2] = np.cos(ang)
    fa[1:2 * k1n:2] = -np.sin(ang)
    wgt = np.where((k1 == 0) | (k1 == n1 // 2), 1.0, 2.0) / n
    ga = np.zeros((h1, ks))
    ga[:, 0:2 * k1n:2] = (np.cos(ang) * wgt[:, None]).T
    ga[:, 1:2 * k1n:2] = (-np.sin(ang) * wgt[:, None]).T
    k2 = np.arange(n2)
    m2 = np.arange(n2)
    kk = k1[:, None, None] + n1 * k2[None, :, None]
    th = 2.0 * np.pi * ((kk * m2[None, None, :]) % n) / n
    mc, ms = np.cos(th), np.sin(th)
    mfwd = np.concatenate([np.concatenate([mc, ms], axis=2), np.concatenate([-ms, mc], axis=2)], axis=1)
    mct, mst = np.transpose(mc, (0, 2, 1)), np.transpose(ms, (0, 2, 1))
    minv = np.concatenate([np.concatenate([mct, -mst], axis=2), np.concatenate([mst, mct], axis=2)], axis=1)
    eye = np.eye(8)
    fa8 = np.einsum('kn,jJ->jknJ', fa, eye).reshape(8 * kp, 8 * h1)
    ga8 = np.einsum('nk,jJ->njkJ', ga, eye).reshape(8 * h1, 8 * ks)
    return tuple(jnp.asarray(t, BF16) for t in (fa8, ga8, mfwd, minv))


def _hy_ctx_kernel(v_ref, x1_ref, x2_ref, fc_ref, fs_ref, gc_ref, gs_ref, ts_ref, td_ref, bias_ref, z_ref):
    d = functools.partial(jnp.dot, preferred_element_type=F32)
    w = v_ref.shape[2]
    kr_all = d(fc_ref[...], ts_ref[...].astype(BF16))
    ki_all = d(fs_ref[...], td_ref[...].astype(BF16))

    def conv(u, o):
        ub = u.astype(BF16)
        cr, ci = d(fc_ref[...], ub), d(fs_ref[...], ub)
        kr, ki = kr_all[:, o * w:(o + 1) * w], ki_all[:, o * w:(o + 1) * w]
        pr = (cr * kr - ci * ki).astype(BF16)
        pi = (cr * ki + ci * kr).astype(BF16)
        return d(gc_ref[...], pr) + d(gs_ref[...], pi) + u * bias_ref[o]

    z = x1_ref[0].astype(F32) * conv(v_ref[0].astype(F32), 0)
    z = x2_ref[0].astype(F32) * conv(z, 1)
    for j in range(2):
        z_ref[0, j] = z[:, j * 128:(j + 1) * 128]


def _hyena_ctx(v, x1, x2, fc, fs, gc, gs, tsum, tdiff, bias, seq, ctx_len):
    bsz, _, w = v.shape
    blk = seq // ctx_len
    kpad = fc.shape[0]
    tok = pl.BlockSpec((1, ctx_len, w), lambda b: (b, blk, 0))
    full = lambda shp: pl.BlockSpec(shp, lambda b: (0,) * len(shp))
    return pl.pallas_call(
        _hy_ctx_kernel, grid=(bsz,),
        in_specs=[tok, tok, tok, full((kpad, ctx_len)), full((kpad, ctx_len)), full((ctx_len, kpad)),
                  full((ctx_len, kpad)), full((ctx_len, HY_ORDER * w)), full((ctx_len, HY_ORDER * w)), full((2, 1, w))],
        out_specs=pl.BlockSpec((1, 2, ctx_len, 128), lambda b: (b, 0, 0, 0)),
        out_shape=jax.ShapeDtypeStruct((bsz, 2, ctx_len, 128), F32),
        compiler_params=_cparams(("parallel",)),
        name="hyena_ctx",
    )(v, x1, x2, fc, fs, gc, gs, tsum, tdiff, bias)


def _hyena_ctx_consts(ctx_len):
    n = 2 * ctx_len
    nk = ctx_len + 1
    kpad = -(-nk // 128) * 128
    k = np.arange(nk)
    m = np.arange(ctx_len)
    ang = 2.0 * np.pi * np.outer(k, m) / n
    fc = np.zeros((kpad, ctx_len))
    fs = np.zeros((kpad, ctx_len))
    fc[:nk] = np.cos(ang)
    fs[:nk] = -np.sin(ang)
    wgt = np.where((k == 0) | (k == ctx_len), 1.0, 2.0) / n
    gc = np.zeros((ctx_len, kpad))
    gs = np.zeros((ctx_len, kpad))
    gc[:, :nk] = (np.cos(ang) * wgt[:, None]).T
    gs[:, :nk] = (-np.sin(ang) * wgt[:, None]).T
    return tuple(jnp.asarray(t, BF16) for t in (fc, fs, gc, gs))


def _hyena_filter_taps(length, w1, b1, w2, b2, w3, freq, decay):
    pos = jnp.arange(length, dtype=F32)
    t = pos / max(length - 1, 1)
    bands = jnp.linspace(1e-4, HY_BANDS - 1, HY_BANDS, dtype=F32)
    ang = (2.0 * math.pi / length) * pos[:, None] * bands
    feats = jnp.concatenate([t[:, None], jnp.cos(ang), -jnp.sin(ang)], axis=-1)
    freq = freq.astype(F32)
    mm = functools.partial(jnp.matmul, precision=HI)
    hid = jnp.sin(freq * (mm(feats, w1.astype(F32)) + b1.astype(F32)))
    hid = jnp.sin(freq * (mm(hid, w2.astype(F32)) + b2.astype(F32)))
    h = mm(hid, w3.astype(F32)) * jnp.exp(-t[:, None] * jnp.abs(decay.astype(F32)))
    half = HY_ORDER * HY_WIDTH
    h_fwd, h_bwd = h[:, :half], h[:, half:]
    l1 = (jnp.abs(h_fwd[0] + h_bwd[0]) + jnp.sum(jnp.abs(h_fwd[1:]), axis=0) + jnp.sum(jnp.abs(h_bwd[1:]), axis=0))
    return (h_fwd + h_bwd) / l1, (h_fwd - h_bwd) / l1


def _mixout_kernel(nt_lat, n_sub, lat_only, *refs):
    if lat_only:
        (xl_ref, m_ref, s5y_ref, yf_ref, yb_ref, xs_ref, z_ref, hyl_ref,
         wglu_ref, bglu_ref, vec512_ref, wout_ref, win_ref, wout2_ref, ln_ref, o_ref) = refs
    else:
        (xl_ref, xc_ref, m_ref, s5y_ref, yf_ref, yb_ref, xs_ref, z_ref, hyl_ref, hyc_ref,
         wglu_ref, bglu_ref, vec512_ref, wout_ref, win_ref, wout2_ref, ln_ref, o_ref) = refs
        is_lat = pl.program_id(1) < nt_lat
    m = m_ref[0, 0]
    d = functools.partial(jnp.dot, preferred_element_type=F32)
    for sub in range(n_sub):
        r = slice(sub * ROW_TILE, (sub + 1) * ROW_TILE)
        if lat_only:
            x_in = xl_ref[0, r, :]
            hy = [hyl_ref[0, j, r, :].astype(BF16) for j in range(2)]
        else:
            x_in = jnp.where(is_lat, xl_ref[0, r, :], xc_ref[0, r, :])
            hy = [jnp.where(is_lat, hyl_ref[0, j, r, :], hyc_ref[0, j, r, :]).astype(BF16) for j in range(2)]
        y5 = _gelu_tanh(jnp.concatenate([s5y_ref[0, 0, r, :], s5y_ref[0, 1, r, :]], axis=1))
        y5 = y5 * _sigmoid(d(y5.astype(BF16), wglu_ref[...]) + bglu_ref[...])
        ys = (yf_ref[0, 0, r, :].astype(F32) + yb_ref[0, 0, r, :].astype(F32)
              + vec512_ref[0:1, :] * xs_ref[0, r, :].astype(F32))
        gsd = ys * _silu(z_ref[0, r, :].astype(F32))
        gsd = gsd * lax.rsqrt(jnp.mean(gsd * gsd, axis=-1, keepdims=True) + LN_EPS) * vec512_ref[1:2, :]
        mix = (d(y5.astype(BF16), wout_ref[0:256, :]) + d(gsd.astype(BF16), wout_ref[256:768, :])
               + d(hy[0], wout_ref[768:896, :]) + d(hy[1], wout_ref[896:1024, :]))
        x = _standardise(ALPHA * x_in + m[2:3] * mix) * ln_ref[0:1, :] + ln_ref[1:2, :]
        h = (_standardise(x) * (1.0 + m[4:5]) + m[3:4]).astype(BF16)
        gate = d(h, win_ref[:, 0:FFN_HIDDEN])
        up = d(h, win_ref[:, FFN_HIDDEN:2 * FFN_HIDDEN])
        act = (_silu(gate) * up).astype(BF16)
        x = _standardise(ALPHA * x + m[5:6] * d(act, wout2_ref[...]))
        o_ref[0, r, :] = x * ln_ref[2:3, :] + ln_ref[3:4, :]


def _mixout_ffn(x_lat, x_ctx, ctx_blk, mods, s5y, yssd, xbc_c, z, hy_lat, hy_ctx, wglu, bglu, vec512, wout, win,
                wout2, layer, ln, nt_lat, rows):
    bsz = x_lat.shape[0]
    lat_only = rows == nt_lat * ROW_TILE
    n_sub = MIX_SUB_TILES if lat_only and rows % (MIX_SUB_TILES * ROW_TILE) == 0 else 1
    tile = n_sub * ROW_TILE
    tok = lambda w: pl.BlockSpec((1, tile, w), lambda b, i: (b, i, 0))
    halves = pl.BlockSpec((1, 2, tile, 128), lambda b, i: (b, 0, i, 0))
    full = lambda shp: pl.BlockSpec(shp, lambda b, i: (0,) * len(shp))
    const = lambda shp: pl.BlockSpec((None,) + shp, lambda b, i: (layer, 0, 0), pipeline_mode=pl.Buffered(1))
    ssd_dir = lambda dirn: pl.BlockSpec((1, 1, tile, SSD_INNER), lambda b, i: (b, dirn, i, 0))
    if lat_only:
        x_specs, x_args = [tok(D_MODEL)], (x_lat,)
        hy_specs, hy_args = [halves], (hy_lat,)
        mod_spec = pl.BlockSpec((1, 1, 6, D_MODEL), lambda b, i: (b, 0, 0, 0))
    else:
        x_specs, x_args = _row_specs(nt_lat, ctx_blk, D_MODEL), (x_lat, x_ctx)
        hy_specs = [pl.BlockSpec((1, 2, tile, 128), lambda b, i: (b, 0, jnp.minimum(i, nt_lat - 1), 0)),
                    pl.BlockSpec((1, 2, tile, 128), lambda b, i: (b, 0, 0, 0))]
        hy_args = (hy_lat, hy_ctx)
        mod_spec = pl.BlockSpec((1, 1, 6, D_MODEL), lambda b, i: (b, jnp.where(i < nt_lat, 0, 1), 0, 0))
    return pl.pallas_call(
        functools.partial(_mixout_kernel, nt_lat, n_sub, lat_only), grid=(bsz, rows // tile),
        in_specs=x_specs + [mod_spec, halves, ssd_dir(0), ssd_dir(1), tok(SSD_INNER), tok(SSD_INNER)] + hy_specs + [
                  full((256, 256)), full((1, 256)), full((2, 512)),
                  const((D_MODEL, D_MODEL)), const((D_MODEL, 2 * FFN_HIDDEN)), const((FFN_HIDDEN, D_MODEL)),
                  full((4, D_MODEL))],
        out_specs=tok(D_MODEL),
        out_shape=jax.ShapeDtypeStruct((bsz, rows, D_MODEL), F32),
        compiler_params=_cparams(("parallel", "parallel"), V7X_VMEM_LIMIT_MB),
        name="mixout_ffn",
    )(*x_args, mods, s5y, yssd, yssd, xbc_c, z, *hy_args, wglu, bglu, vec512, wout, win, wout2, ln)


def _layer(x_lat, x_ctx, ctx_blk, mods, p, seq, ctx_len, want_ctx):
    nt_lat = seq // ROW_TILE
    t_all = seq + ctx_len
    layer = p["layer"]
    s5u, z, xbc_c, v, x1, x2, dt_raw = _inproj(x_lat, x_ctx, ctx_blk, mods, p["w_in"], layer, p["ssd_conv_w"],
                                               p["ssd_conv_b"], p["hy_conv_w"], p["hy_conv_b"], nt_lat)
    s5y = _s5_scan(s5u, p["s5_ncat"], p["s5_tz"], p["s5_mcat"], p["s5_coef"], p["s5_dvec"],
                   seq // S5_CHUNK, ctx_len // S5_CHUNK)
    yssd = _ssd_scan(xbc_c, dt_raw, p["ssd_tri"], p["ssd_par"], p["ssd_expand"],
                     seq // SSD_CHUNK, ctx_len // SSD_CHUNK)
    hy_lat = _hyena_lat(v, x1, x2, p["hy_fa"], p["hy_ga"], p["hy_mfwd"], p["hy_minv"],
                        p["hy_kr"], p["hy_ki"], p["hy_bias"], seq)
    if want_ctx:
        hy_ctx = _hyena_ctx(v, x1, x2, *p["hy_ctx_mats"], *p["hy_ctx_taps"], p["hy_bias"], seq, ctx_len)
        rows = t_all
    else:
        hy_ctx, rows = hy_lat, seq
    return _mixout_ffn(x_lat, x_ctx, ctx_blk, mods, s5y, yssd, xbc_c, z, hy_lat, hy_ctx, p["s5_wglu"], p["s5_bglu"],
                       p["ssd_vec"], p["w_out"], p["ffn_w_in"], p["ffn_w_out"], layer, p["ln"], nt_lat, rows)


def kernel(x, c, ctx, c_ctx, w_mod, b_mod, w_in, s5_lam_re, s5_lam_im, s5_log_step, s5_b_re, s5_b_im, s5_c_re, s5_c_im, s5_d, s5_w_glu, s5_b_glu, ssd_conv_w, ssd_conv_b, ssd_dt_bias, ssd_a_log, ssd_d, ssd_norm_w, hy_conv_w, hy_conv_b, hy_w1, hy_b1, hy_w2, hy_b2, hy_w3, hy_freq, hy_decay, hy_bias, w_out, ln1_g, ln1_b, ffn_w_in, ffn_w_out, ln2_g, ln2_b):
    bsz, seq, _ = x.shape
    ctx_len = ctx.shape[1]
    assert bsz == 8 and seq % ROW_TILE == 0 and ctx_len == ROW_TILE

    fa, ga, mfwd, minv = _hyena_lat_consts(seq)
    ctx_mats = _hyena_ctx_consts(ctx_len)
    tt = np.arange(SSD_CHUNK)
    tri = jnp.asarray(np.stack([tt[None, :] <= tt[:, None], tt[None, :] >= tt[:, None]]), BF16)
    expand = jnp.asarray(np.repeat(np.eye(128, SSD_HEADS), SSD_HEADDIM, axis=1)[:, :SSD_INNER], BF16)
    cvec = jnp.zeros((16, D_MODEL), F32).at[:bsz].set(c.astype(F32)).at[bsz].set(c_ctx.astype(F32))

    w_p = jnp.concatenate([w_in[..., 0:REF_COL_DT], w_in[..., REF_COL_HY:MIX_IN], w_in[..., REF_COL_DT:REF_COL_HY],
                           jnp.zeros((DEPTH, D_MODEL, MIX_IN_PAD - MIX_IN), w_in.dtype)], axis=-1).astype(BF16)
    w_out_b, ffn_in_b, ffn_out_b = w_out.astype(BF16), ffn_w_in.astype(BF16), ffn_w_out.astype(BF16)

    x_lat, x_ctx, ctx_blk = x.astype(F32), ctx.astype(F32), 0
    for l in range(DEPTH):
        want_ctx = l < DEPTH - 1
        mod16 = _modulation(cvec, w_mod.astype(F32), b_mod[l].astype(F32).reshape(1, -1), l)
        mods = jnp.stack([mod16[:bsz].reshape(bsz, 6, D_MODEL),
                          jnp.broadcast_to(mod16[bsz].reshape(1, 6, D_MODEL), (bsz, 6, D_MODEL))], axis=1)
        ncat, tz, mcat, coef = _s5_weights(s5_lam_re[l], s5_lam_im[l], s5_log_step[l], s5_b_re[l], s5_b_im[l],
                                           s5_c_re[l], s5_c_im[l])
        rep = lambda t: jnp.repeat(t.astype(F32), SSD_HEADDIM, axis=-1)
        par = jnp.zeros((2, 8, 128), F32)
        par = par.at[:, 0, :SSD_HEADS].set(ssd_dt_bias[l].astype(F32))
        par = par.at[:, 1, :SSD_HEADS].set(-jnp.exp(ssd_a_log[l].astype(F32)))
        hy_args = (hy_w1[l], hy_b1[l], hy_w2[l], hy_b2[l], hy_w3[l], hy_freq[l], hy_decay[l])
        kr, ki = _hyena_spectrum(*_hyena_filter_taps(seq, *hy_args), fa, mfwd)
        p = dict(
            layer=l, w_in=w_p, s5_ncat=ncat, s5_tz=tz, s5_mcat=mcat, s5_coef=coef,
            ssd_conv_w=ssd_conv_w[l].astype(F32), ssd_conv_b=ssd_conv_b[l].astype(F32),
            ssd_tri=tri, ssd_par=par, ssd_expand=expand,
            hy_conv_w=hy_conv_w[l].astype(F32), hy_conv_b=hy_conv_b[l].astype(F32),
            hy_fa=fa, hy_ga=ga, hy_mfwd=mfwd, hy_minv=minv, hy_kr=kr, hy_ki=ki,
            hy_bias=hy_bias[l].astype(F32)[:, None, :],
            s5_wglu=s5_w_glu[l].astype(BF16),
            s5_bglu=s5_b_glu[l].astype(F32).reshape(1, S5_WIDTH),
            s5_dvec=jnp.tile(s5_d[l].astype(F32).reshape(S5_GROUPS, 1, S5_GROUP_CH), (1, 1, S5_CHUNK)),
            ssd_vec=jnp.stack([rep(ssd_d[l]), ssd_norm_w[l].astype(F32)]),
            w_out=w_out_b, ln=jnp.stack([ln1_g[l], ln1_b[l], ln2_g[l], ln2_b[l]]).astype(F32),
            ffn_w_in=ffn_in_b, ffn_w_out=ffn_out_b,
        )
        if want_ctx:
            p.update(hy_ctx_mats=ctx_mats, hy_ctx_taps=_hyena_filter_taps(ctx_len, *hy_args))
        x_lat = _layer(x_lat, x_ctx, ctx_blk, mods, p, seq, ctx_len, want_ctx)
        x_ctx, ctx_blk = x_lat, seq // ROW_TILE
    return x_lat.astype(x.dtype)
```

```python
import functools
import math

import numpy as np
import jax
import jax.numpy as jnp
from jax import lax
from jax.experimental import pallas as pl
from jax.experimental.pallas import tpu as pltpu

F32 = jnp.float32
BF16 = jnp.bfloat16
HI = lax.Precision.HIGHEST

D_MODEL = 1024
DEPTH = 2
S5_WIDTH = 256
S5_GROUP_CH = 16
S5_GROUPS = 16
S5_MAX_RE = -1e-4
S5_CHUNK = 16
SSD_INNER = 512
SSD_HEADDIM = 64
SSD_HEADS = 8
SSD_GROUPS = 2
SSD_STATE = 128
SSD_CONV = 5
SSD_CHUNK = 128
SSD_CHUNKS_PER_STEP = 2
SSD_XBC = 1024
HY_WIDTH = 256
HY_ORDER = 2
HY_SHORT = 3
HY_BANDS = 16
HY_IN = 768
COL_Z = S5_WIDTH
COL_XBC = COL_Z + SSD_INNER
COL_HY = COL_XBC + SSD_XBC
COL_DT = COL_HY + HY_IN
MIX_IN_PAD = COL_DT + 128
REF_COL_DT = COL_HY
REF_COL_HY = REF_COL_DT + SSD_HEADS
MIX_IN = REF_COL_HY + HY_IN
FFN_HIDDEN = 2816
ALPHA = (2 * DEPTH) ** 0.25
LN_EPS = 1e-6

ROW_TILE = 256
MIX_SUB_TILES = 2
CONV_GROUP = 256
HALO = 16
FFT_N2 = 128
HY_A_PITCH = 72
HY_K1_PER_STEP = 11
HY_B_PITCH = 136
V7X_VMEM_LIMIT_MB = 56


def _cparams(sem, vmem_mb=None):
    kw = dict(dimension_semantics=sem)
    if vmem_mb is not None:
        kw["vmem_limit_bytes"] = vmem_mb * 2 ** 20
    return pltpu.CompilerParams(**kw)


def _standardise(x):
    mu = jnp.mean(x, axis=-1, keepdims=True)
    xc = x - mu
    var = jnp.mean(xc * xc, axis=-1, keepdims=True)
    return xc * lax.rsqrt(var + LN_EPS)


def _sigmoid(x):
    return 1.0 / (1.0 + jnp.exp(-x))


def _silu(x):
    return x * _sigmoid(x)


def _gelu_tanh(x):
    return 0.5 * x * (1.0 + jnp.tanh(0.7978845608028654 * (x + 0.044715 * (x * x * x))))


def _softplus(x):
    return jnp.maximum(x, 0.0) + jnp.log(1.0 + jnp.exp(-jnp.abs(x)))


def _split3(a):
    a1 = a.astype(BF16)
    r1 = a - a1.astype(F32)
    a2 = r1.astype(BF16)
    a3 = (r1 - a2.astype(F32)).astype(BF16)
    return a1, a2, a3


def _dot_sel_lhs(sel, a):
    a1, a2, a3 = _split3(a)
    d = functools.partial(jnp.dot, preferred_element_type=F32)
    return d(sel, a1) + d(sel, a2) + d(sel, a3)


def _mod_kernel(c_ref, w_ref, b_ref, o_ref):
    ca = _silu(c_ref[...])
    c1, c2, c3 = _split3(ca)
    w1, w2, w3 = _split3(w_ref[...])
    d = functools.partial(jnp.dot, preferred_element_type=F32)
    acc = d(c1, w1) + d(c1, w2) + d(c2, w1) + d(c1, w3) + d(c2, w2) + d(c3, w1)
    o_ref[...] = acc + b_ref[...]


def _modulation(cvec, w, b, layer):
    n = w.shape[2]
    tn = 1536
    return pl.pallas_call(
        _mod_kernel,
        grid=(n // tn,),
        in_specs=[pl.BlockSpec((16, D_MODEL), lambda j: (0, 0)),
                  pl.BlockSpec((None, D_MODEL, tn), lambda j: (layer, 0, j)),
                  pl.BlockSpec((1, tn), lambda j: (0, j))],
        out_specs=pl.BlockSpec((16, tn), lambda j: (0, j)),
        out_shape=jax.ShapeDtypeStruct((16, n), F32),
        compiler_params=_cparams(("arbitrary",), 40),
        name="modulation",
    )(cvec, w, b)


def _rows_of(nt_lat, xl_ref, xc_ref):
    return jnp.where(pl.program_id(1) < nt_lat, xl_ref[0], xc_ref[0])


def _row_specs(nt_lat, ctx_blk, width):
    return [pl.BlockSpec((1, ROW_TILE, width), lambda b, i: (b, jnp.minimum(i, nt_lat - 1), 0)),
            pl.BlockSpec((1, ROW_TILE, width), lambda b, i: (b, ctx_blk, 0))]


def _dwconv_rows(ext_ref, w_ref, b_ref, wcol, taps, act, o_ref, ocol):
    pad = taps // 2
    rb = 64
    for c in range(0, CONV_GROUP, 128):
        wk = [w_ref[k:k + 1, wcol + c:wcol + c + 128] for k in range(taps)]
        bias = b_ref[0:1, wcol + c:wcol + c + 128]
        for r0 in range(0, ROW_TILE, rb):
            blk = ext_ref[HALO - 8 + r0:HALO + 8 + r0 + rb, c:c + 128]
            acc = bias + wk[pad] * blk[8:8 + rb]
            for k in range(taps):
                if k != pad:
                    acc = acc + wk[k] * pltpu.roll(blk, (pad - k) % (rb + 16), axis=0)[8:8 + rb]
            if act:
                acc = _silu(acc)
            o_ref[0, r0:r0 + rb, ocol + c:ocol + c + 128] = acc.astype(o_ref.dtype)


def _inproj_kernel(nt_lat, xl_ref, xc_ref, xp_ref, xn_ref, m_ref, w_ref, w5_ref, b5_ref, w3_ref, b3_ref,
                   s5_ref, z_ref, xbc_ref, v_ref, x1_ref, x2_ref, dt_ref, *ext_refs):
    i = pl.program_id(1)
    m = m_ref[0, 0]
    mod = lambda x: (_standardise(x) * (1.0 + m[1:2]) + m[0:1]).astype(BF16)
    d = functools.partial(jnp.dot, preferred_element_type=F32)
    h = mod(_rows_of(nt_lat, xl_ref, xc_ref))
    hp, hn = mod(xp_ref[0]), mod(xn_ref[0])
    has_prev = jnp.logical_and(i > 0, i < nt_lat)
    has_next = i < nt_lat - 1
    n5 = SSD_XBC // CONV_GROUP
    hy_outs = (v_ref, x1_ref, x2_ref)

    def project(k):
        wc = w_ref[:, COL_XBC + k * CONV_GROUP:COL_XBC + (k + 1) * CONV_GROUP]
        e = ext_refs[k]
        e[0:HALO, :] = jnp.where(has_prev, d(hp, wc), 0.0)
        e[HALO:HALO + ROW_TILE, :] = d(h, wc)
        e[HALO + ROW_TILE:2 * HALO + ROW_TILE, :] = jnp.where(has_next, d(hn, wc), 0.0)

    def conv(k):
        if k < n5:
            _dwconv_rows(ext_refs[k], w5_ref, b5_ref, k * CONV_GROUP, SSD_CONV, True, xbc_ref, k * CONV_GROUP)
        else:
            _dwconv_rows(ext_refs[k], w3_ref, b3_ref, (k - n5) * CONV_GROUP, HY_SHORT, False, hy_outs[k - n5], 0)

    ngroups = len(ext_refs)
    project(0)
    for k in range(ngroups):
        if k + 1 < ngroups:
            project(k + 1)
        conv(k)
    s5_ref[0] = d(h, w_ref[:, 0:COL_Z])
    z_ref[0] = d(h, w_ref[:, COL_Z:COL_XBC]).astype(z_ref.dtype)
    dt_ref[0] = d(h, w_ref[:, COL_DT:MIX_IN_PAD])


def _inproj(x_lat, x_ctx, ctx_blk, mods, w_p, layer, conv5_w, conv5_b, conv3_w, conv3_b, nt_lat):
    bsz = x_lat.shape[0]
    nt = nt_lat + 1
    t_all = nt * ROW_TILE
    rh = ROW_TILE // HALO
    last = nt_lat * rh - 1
    full = lambda shp: pl.BlockSpec(shp, lambda b, i: (0,) * len(shp))
    outs = ((256, F32), (512, BF16), (SSD_XBC, BF16), (HY_WIDTH, BF16), (HY_WIDTH, BF16), (HY_WIDTH, BF16), (128, F32))
    return pl.pallas_call(
        functools.partial(_inproj_kernel, nt_lat),
        grid=(bsz, nt),
        in_specs=_row_specs(nt_lat, ctx_blk, D_MODEL) + [
                  pl.BlockSpec((1, HALO, D_MODEL), lambda b, i: (b, jnp.clip(i * rh - 1, 0, last), 0)),
                  pl.BlockSpec((1, HALO, D_MODEL), lambda b, i: (b, jnp.clip((i + 1) * rh, 0, last), 0)),
                  pl.BlockSpec((1, 1, 6, D_MODEL), lambda b, i: (b, jnp.where(i < nt_lat, 0, 1), 0, 0)),
                  pl.BlockSpec((None, D_MODEL, MIX_IN_PAD), lambda b, i: (layer, 0, 0)), full((SSD_CONV, SSD_XBC)), full((1, SSD_XBC)),
                  full((HY_SHORT, HY_IN)), full((1, HY_IN))],
        out_specs=[pl.BlockSpec((1, ROW_TILE, w), lambda b, i: (b, i, 0)) for w, _ in outs],
        out_shape=[jax.ShapeDtypeStruct((bsz, t_all, w), dt) for w, dt in outs],
        scratch_shapes=[pltpu.VMEM((ROW_TILE + 2 * HALO, CONV_GROUP), F32)] * ((SSD_XBC + HY_IN) // CONV_GROUP),
        compiler_params=_cparams(("parallel", "parallel"), 40),
        name="inproj",
    )(x_lat, x_ctx, x_lat, x_lat, mods, w_p, conv5_w, conv5_b.reshape(1, -1), conv3_w, conv3_b.reshape(1, -1))


def _s5_kernel(n_lat, n_ctx, u0_ref, u1_ref, ncat_ref, tz_ref, mcat_ref, coef_ref, dvec_ref, y_ref,
               ut_ref, s_ref, hp_ref):
    q, gch, ng = S5_CHUNK, S5_GROUP_CH, S5_GROUPS
    nc = n_lat + n_ctx
    dot = functools.partial(jnp.dot, preferred_element_type=F32)
    u_refs = (u0_ref, u1_ref)
    gpl = 128 // gch
    for j in range(2):
        for s in range(q):
            rows = u_refs[j][0, pl.ds(s, nc, stride=q), :]
            for gg in range(gpl):
                ut_ref[gpl * j + gg, :, s * gch:(s + 1) * gch] = rows[:, gg * gch:(gg + 1) * gch]
    for g in range(ng):
        sg = dot(ut_ref[g].astype(BF16), ncat_ref[g])
        for k in range(4):
            s_ref[k, pl.ds(g, nc, stride=ng), :] = sg[:, k * 128:(k + 1) * 128]
    c1f, c2f, c1b, c2b = coef_ref[0], coef_ref[1], coef_ref[2], coef_ref[3]

    def step(cf, cb, carry):
        hf, hsf, hb, hsb = carry
        rf = pl.multiple_of(cf * ng, ng)
        rb = pl.multiple_of(cb * ng, ng)
        hp_ref[0, pl.ds(rf, ng), :] = hf
        hp_ref[1, pl.ds(rb, ng), :] = hb
        sf = s_ref[0, pl.ds(rf, ng), :]
        sb = s_ref[1, pl.ds(rb, ng), :]
        ssf = s_ref[2, pl.ds(rf, ng), :]
        ssb = s_ref[3, pl.ds(rb, ng), :]
        return (c1f * hf + c2f * hsf + sf, c1f * hsf - c2f * hf + ssf,
                c1b * hb + c2b * hsb + sb, c1b * hsb - c2b * hb + ssb)

    z = jnp.zeros((ng, 128), F32)
    carry = lax.fori_loop(0, n_ctx, lambda i, c: step(n_lat + i, n_lat + n_ctx - 1 - i, c), (z, z, z, z))
    lax.fori_loop(0, n_lat, lambda i, c: step(i, n_lat - 1 - i, c), carry)
    for g in range(ng):
        ug = ut_ref[g]
        hp = jnp.concatenate([hp_ref[0, pl.ds(g, nc, stride=ng), :], hp_ref[1, pl.ds(g, nc, stride=ng), :]], axis=1)
        ut_ref[g] = dot(ug.astype(BF16), tz_ref[g]) + dot(hp.astype(BF16), mcat_ref[g]) + ug * dvec_ref[g]
    for j in range(2):
        for s in range(q):
            rows = jnp.concatenate([ut_ref[gpl * j + gg, :, s * gch:(s + 1) * gch] for gg in range(gpl)], axis=1)
            y_ref[0, j, pl.ds(s, nc, stride=q), :] = rows


def _s5_scan(s5u, ncat, tz, mcat, coef, dvec, n_lat, n_ctx):
    bsz, t_all, _ = s5u.shape
    nc = n_lat + n_ctx
    ng = S5_GROUPS
    const = lambda shp: pl.BlockSpec(shp, lambda b: (0,) * len(shp), pipeline_mode=pl.Buffered(1))
    return pl.pallas_call(
        functools.partial(_s5_kernel, n_lat, n_ctx),
        grid=(bsz,),
        in_specs=[pl.BlockSpec((1, t_all, 128), lambda b: (b, 0, 0)),
                  pl.BlockSpec((1, t_all, 128), lambda b: (b, 0, 1)),
                  const((ng, 256, 512)), const((ng, 256, 256)), const((ng, 256, 256)),
                  const((4, ng, 128)), const((ng, 1, 256))],
        out_specs=pl.BlockSpec((1, 2, t_all, 128), lambda b: (b, 0, 0, 0)),
        out_shape=jax.ShapeDtypeStruct((bsz, 2, t_all, 128), F32),
        scratch_shapes=[pltpu.VMEM((ng, nc, 256), F32), pltpu.VMEM((4, nc * ng, 128), F32),
                        pltpu.VMEM((2, nc * ng, 128), F32)],
        compiler_params=_cparams(("parallel",), V7X_VMEM_LIMIT_MB),
        name="s5_scan",
    )(s5u, s5u, ncat, tz, mcat, coef, dvec)


def _s5_weights(lam_re, lam_im, log_step, b_re, b_im, c_re, c_im):
    q, ng, gch = S5_CHUNK, S5_GROUPS, S5_GROUP_CH
    a_re = jnp.minimum(lam_re.astype(F32), S5_MAX_RE)
    a_im = lam_im.astype(F32)
    step = jnp.exp(log_step.astype(F32))[..., None]
    taus = jnp.arange(q + 1, dtype=F32)[:, None, None, None]
    mag = jnp.exp(taus * (a_re * step))
    pr, pi = mag * jnp.cos(taus * (a_im * step)), mag * jnp.sin(taus * (a_im * step))
    nr, ni, den = pr[1] - 1.0, pi[1], a_re * a_re + a_im * a_im
    fr, fi = ((nr * a_re + ni * a_im) / den)[..., None], ((ni * a_re - nr * a_im) / den)[..., None]
    b_r, b_i = b_re.astype(F32), b_im.astype(F32)
    bb_r, bb_i = fr * b_r - fi * b_i, fr * b_i + fi * b_r
    c_r = jnp.swapaxes(c_re.astype(F32), -1, -2)
    c_i = jnp.swapaxes(c_im.astype(F32), -1, -2)
    pw_r, pw_i = jnp.transpose(pr, (1, 2, 3, 0)), jnp.transpose(pi, (1, 2, 3, 0))

    def lagged(v_r, v_i, d, lags):
        e_r, e_i = jnp.repeat(pw_r[d][..., lags], gch, axis=-1), jnp.repeat(pw_i[d][..., lags], gch, axis=-1)
        n = e_r.shape[-1] // gch
        t_r, t_i = jnp.tile(v_r[d], (1, 1, n)), jnp.tile(v_i[d], (1, 1, n))
        return t_r * e_r - t_i * e_i, t_r * e_i + t_i * e_r

    def lag_kernels(d, lags):
        cp_r, cp_i = lagged(c_r, c_i, d, lags)
        return (jnp.einsum('gpk,gpn->gkn', bb_r[d], cp_r, precision=HI)
                - jnp.einsum('gpk,gpn->gkn', bb_i[d], cp_i, precision=HI))

    k_fwd = lag_kernels(0, slice(0, q))
    k_bwd = lag_kernels(1, slice(q - 1, None, -1))
    edge = (q - 1) * gch
    kfull = jnp.concatenate([k_bwd[..., :edge], k_bwd[..., edge:] + k_fwd[..., :gch], k_fwd[..., gch:]], axis=-1)
    tz = jnp.stack([kfull[:, :, (q - 1 - s) * gch:(2 * q - 1 - s) * gch] for s in range(q)],
                   axis=1).reshape(ng, q * gch, q * gch)
    nf_r, nf_i = (jnp.swapaxes(t, -1, -2) for t in lagged(bb_r, bb_i, 0, slice(q - 1, None, -1)))
    nb_r, nb_i = (jnp.swapaxes(t, -1, -2) for t in lagged(bb_r, bb_i, 1, slice(0, q)))
    ncat = jnp.concatenate([nf_r, nf_i, nb_r, nb_i, nf_i, nf_r, nb_i, nb_r], axis=-1)
    mf_r, mf_i = lagged(c_r, c_i, 0, slice(1, None))
    mb_r, mb_i = lagged(c_r, c_i, 1, slice(q, 0, -1))
    mcat = jnp.concatenate([mf_r, -mf_i, mb_r, -mb_i], axis=1)
    lr, li = pr[q], pi[q]
    coef = jnp.stack([jnp.concatenate([lr[0], lr[0]], axis=-1), jnp.concatenate([-li[0], li[0]], axis=-1),
                      jnp.concatenate([lr[1], lr[1]], axis=-1), jnp.concatenate([-li[1], li[1]], axis=-1)])
    return ncat.astype(BF16), tz.astype(BF16), mcat.astype(BF16), coef.astype(F32)


def _ssd_kernel(cps, xbc_ref, dt_ref, tri_ref, par_ref, exp_ref, y_ref, st_ref):
    direction = pl.program_id(1)

    @pl.when(pl.program_id(2) == 0)
    def _():
        st_ref[...] = jnp.zeros_like(st_ref)

    tri = tri_ref[0]
    mask = tri > 0
    lane = lax.broadcasted_iota(jnp.int32, (SSD_CHUNK, 128), 1)
    lo = lane < SSD_HEADDIM
    zero_b = jnp.zeros((SSD_CHUNK, 128), BF16)
    head_of_lane = lax.broadcasted_iota(jnp.int32, (1, SSD_INNER), 1) // SSD_HEADDIM
    expand = lambda t: jnp.dot(t.astype(BF16), exp_ref[...], preferred_element_type=F32)
    state = [st_ref[:, g * 256:(g + 1) * 256] for g in range(SSD_GROUPS)]
    for c in range(cps):
        sub = jnp.where(direction == 0, c, cps - 1 - c)
        rows = pl.ds(pl.multiple_of(sub * SSD_CHUNK, SSD_CHUNK), SSD_CHUNK)
        xs = xbc_ref[0, rows, 0:512].astype(F32)
        dt_c = _softplus(dt_ref[0, rows, :] + par_ref[0, 0:1, :])
        a_c = par_ref[0, 1:2, :] * dt_c
        cs = _dot_sel_lhs(tri, a_c)
        tot = jnp.sum(a_c, axis=0, keepdims=True)
        cs_t = cs.T
        xdt = xs * expand(dt_c)
        xdt_b = xdt.astype(BF16)
        xd_end = (xdt * expand(jnp.exp(tot - cs))).astype(BF16)
        e_cs = expand(jnp.exp(cs))
        tot_full = jnp.zeros((1, SSD_INNER), F32)
        for head in range(SSD_HEADS):
            tot_full = jnp.where(head_of_lane == head, tot[:, head:head + 1], tot_full)
        e_tot = jnp.exp(tot_full)
        for g in range(SSD_GROUPS):
            bm_b = xbc_ref[0, rows, 512 + g * 128:512 + (g + 1) * 128]
            cm_b = xbc_ref[0, rows, 768 + g * 128:768 + (g + 1) * 128]
            cb = lax.dot_general(cm_b, bm_b, (((1,), (1,)), ((), ())), preferred_element_type=F32)
            y_off = jnp.dot(cm_b, state[g].astype(BF16), preferred_element_type=F32) * e_cs[:, g * 256:(g + 1) * 256]
            for j in range(2):
                c0 = g * 256 + j * 128
                x_pair = xdt_b[:, c0:c0 + 128]
                acc = y_off[:, j * 128:(j + 1) * 128]
                for hh in range(2):
                    head = c0 // SSD_HEADDIM + hh
                    decay = jnp.exp(jnp.where(mask, cs[:, head:head + 1] - cs_t[head:head + 1, :], -1e30))
                    gm = (cb * decay).astype(BF16)
                    xh = jnp.where(lo if hh == 0 else jnp.logical_not(lo), x_pair, zero_b)
                    acc = acc + jnp.dot(gm, xh, preferred_element_type=F32)
                y_ref[0, 0, rows, c0:c0 + 128] = acc.astype(y_ref.dtype)
            state[g] = (state[g] * e_tot[:, g * 256:(g + 1) * 256]
                        + jnp.dot(bm_b.astype(F32).T.astype(BF16), xd_end[:, g * 256:(g + 1) * 256],
                                  preferred_element_type=F32))
    for g in range(SSD_GROUPS):
        st_ref[:, g * 256:(g + 1) * 256] = state[g]


def _ssd_scan(xbc_c, dt_raw, tri, par, expand, n_lat, n_ctx):
    bsz, t_all, _ = xbc_c.shape
    cps = SSD_CHUNKS_PER_STEP
    assert n_lat % cps == 0 and n_ctx % cps == 0
    n_lat, n_ctx = n_lat // cps, n_ctx // cps
    nc = n_lat + n_ctx
    rows = cps * SSD_CHUNK

    def blk(d, i):
        fwd = jnp.where(i < n_ctx, n_lat + i, i - n_ctx)
        return jnp.where(d == 0, fwd, nc - 1 - i)

    return pl.pallas_call(
        functools.partial(_ssd_kernel, cps),
        grid=(bsz, 2, nc),
        in_specs=[pl.BlockSpec((1, rows, SSD_XBC), lambda b, d, i: (b, blk(d, i), 0)),
                  pl.BlockSpec((1, rows, 128), lambda b, d, i: (b, blk(d, i), 0)),
                  pl.BlockSpec((1, SSD_CHUNK, SSD_CHUNK), lambda b, d, i: (d, 0, 0)),
                  pl.BlockSpec((1, 8, 128), lambda b, d, i: (d, 0, 0)),
                  pl.BlockSpec((128, SSD_INNER), lambda b, d, i: (0, 0))],
        out_specs=pl.BlockSpec((1, 1, rows, SSD_INNER), lambda b, d, i: (b, d, blk(d, i), 0)),
        out_shape=jax.ShapeDtypeStruct((bsz, 2, t_all, SSD_INNER), BF16),
        scratch_shapes=[pltpu.VMEM((SSD_STATE, SSD_INNER), F32)],
        compiler_params=_cparams(("parallel", "parallel", "arbitrary")),
        name="ssd_scan",
    )(xbc_c, dt_raw, tri, par, expand)


def _hy_outer_forward(tile, fa_ref, a2_ref, h1, kp):
    dot = functools.partial(jnp.dot, preferred_element_type=F32)
    fa = fa_ref[...]

    def body(i8, carry):
        r0 = pl.multiple_of(i8 * 8, 8)
        x = jnp.concatenate([tile(n1, r0) for n1 in range(h1)], axis=0)
        out = dot(fa, x.astype(BF16))
        for j in range(8):
            row = pl.multiple_of((i8 * 8 + j) * HY_A_PITCH, 8)
            for s in range(2):
                a2_ref[s, pl.ds(row, kp), :] = out[j * kp:(j + 1) * kp, s * 128:(s + 1) * 128]
        return carry

    lax.fori_loop(0, FFT_N2 // 8, body, 0, unroll=2)


def _hy_lat_kernel(k1n, kgrp, v_ref, x1_ref, x2_ref, fa_ref, ga_ref, mf_ref, mi_ref, kr_ref, ki_ref, bias_ref,
                   o_ref, z_ref, a2_ref, b2_ref):
    order = pl.program_id(1)
    ph = pl.program_id(2)
    n2 = FFT_N2
    kp, h1 = fa_ref.shape[0] // 8, fa_ref.shape[1] // 8
    ks = ga_ref.shape[1] // 8
    seq = h1 * n2
    ngroups = k1n // kgrp
    dot = functools.partial(jnp.dot, preferred_element_type=F32)

    @pl.when(jnp.logical_and(order == 0, ph == 0))
    def _():
        for j in range(2):
            z_ref[j] = v_ref[0, :, j * 128:(j + 1) * 128].astype(F32)
            b2_ref[j, 2 * k1n * HY_B_PITCH:ks * HY_B_PITCH, :] = jnp.zeros(((ks - 2 * k1n) * HY_B_PITCH, 128), F32)

    @pl.when(ph == 0)
    def _():
        tile = lambda n1, r0: jnp.concatenate([z_ref[j, pl.ds(n1 * n2 + r0, 8), :] for j in range(2)], axis=1)
        _hy_outer_forward(tile, fa_ref, a2_ref, h1, kp)

    @pl.when(jnp.logical_and(ph >= 1, ph <= ngroups))
    def _():
        for t in range(kgrp):
            k1 = (ph - 1) * kgrp + t
            are = jnp.concatenate([a2_ref[j, pl.ds(2 * k1, n2, stride=HY_A_PITCH), :] for j in range(2)], axis=1)
            aim = jnp.concatenate([a2_ref[j, pl.ds(2 * k1 + 1, n2, stride=HY_A_PITCH), :] for j in range(2)], axis=1)
            x = dot(mf_ref[k1], jnp.concatenate([are, aim], axis=0).astype(BF16))
            xr, xi = x[:n2], x[n2:]
            kr, ki = kr_ref[t].astype(F32), ki_ref[t].astype(F32)
            y = jnp.concatenate([xr * kr - xi * ki, xr * ki + xi * kr], axis=0).astype(BF16)
            bq = dot(mi_ref[k1], y)
            row = pl.multiple_of(2 * k1 * HY_B_PITCH, 8)
            for j in range(2):
                b2_ref[j, pl.ds(row, n2), :] = bq[:n2, j * 128:(j + 1) * 128]
                b2_ref[j, pl.ds(row + HY_B_PITCH, n2), :] = bq[n2:, j * 128:(j + 1) * 128]

    @pl.when(ph == ngroups + 1)
    def _():
        ga = ga_ref[...]

        def body(i8, carry):
            r0 = pl.multiple_of(i8 * 8, 8)
            rows = jnp.concatenate(
                [jnp.concatenate([b2_ref[j, pl.ds(k * HY_B_PITCH + r0, 8), :] for j in range(2)], axis=1)
                 for k in range(ks)], axis=0)
            y = dot(ga, rows.astype(BF16))
            for n1 in range(h1):
                for j in range(2):
                    o_ref[0, j, pl.ds(n1 * n2 + r0, 8), :] = y[n1 * 8:(n1 + 1) * 8, j * 128:(j + 1) * 128]
            return carry

        lax.fori_loop(0, n2 // 8, body, 0, unroll=2)
        rb = min(256, seq)

        def gate(i, carry):
            r0 = pl.multiple_of(i * rb, 8)
            for j in range(2):
                conv = o_ref[0, j, pl.ds(r0, rb), :] + z_ref[j, pl.ds(r0, rb), :] * bias_ref[0, :, j * 128:(j + 1) * 128]
                x1 = x1_ref[0, pl.ds(r0, rb), j * 128:(j + 1) * 128].astype(F32)
                x2 = x2_ref[0, pl.ds(r0, rb), j * 128:(j + 1) * 128].astype(F32)
                z_ref[j, pl.ds(r0, rb), :] = x1 * conv
                o_ref[0, j, pl.ds(r0, rb), :] = x2 * conv
            return carry

        lax.fori_loop(0, seq // rb, gate, 0)


def _hy_spec_kernel(k1n, ts_ref, td_ref, fa_ref, mf_ref, kr_ref, ki_ref, a2_ref):
    n2 = FFT_N2
    kp, h1 = fa_ref.shape[0] // 8, fa_ref.shape[1] // 8
    dot = functools.partial(jnp.dot, preferred_element_type=F32)
    tile = lambda n1, r0: jnp.concatenate([ts_ref[pl.ds(n1 * n2 + r0, 8), :], td_ref[pl.ds(n1 * n2 + r0, 8), :]], axis=1)
    _hy_outer_forward(tile, fa_ref, a2_ref, h1, kp)

    def per_k1(k1, carry):
        are = jnp.concatenate([a2_ref[j, pl.ds(2 * k1, n2, stride=HY_A_PITCH), :] for j in range(2)], axis=1)
        aim = jnp.concatenate([a2_ref[j, pl.ds(2 * k1 + 1, n2, stride=HY_A_PITCH), :] for j in range(2)], axis=1)
        x = dot(mf_ref[k1], jnp.concatenate([are, aim], axis=0).astype(BF16))
        kr_ref[k1] = x[:n2, 0:128].astype(kr_ref.dtype)
        ki_ref[k1] = x[n2:, 128:256].astype(ki_ref.dtype)
        return carry

    lax.fori_loop(0, k1n, per_k1, 0)


def _hyena_spectrum(tsum, tdiff, fa, mfwd):
    seq, lanes = tsum.shape
    n2 = FFT_N2
    k1n = mfwd.shape[0]
    taps = pl.BlockSpec((seq, 128), lambda j: (0, j))
    out = pl.BlockSpec((k1n, n2, 128), lambda j: (0, 0, j))
    return pl.pallas_call(
        functools.partial(_hy_spec_kernel, k1n),
        grid=(lanes // 128,),
        in_specs=[taps, taps, pl.BlockSpec(fa.shape, lambda j: (0, 0)),
                  pl.BlockSpec((k1n, 2 * n2, 2 * n2), lambda j: (0, 0, 0))],
        out_specs=[out, out],
        out_shape=[jax.ShapeDtypeStruct((k1n, n2, lanes), BF16)] * 2,
        scratch_shapes=[pltpu.VMEM((2, n2 * HY_A_PITCH, 128), F32)],
        compiler_params=_cparams(("parallel",), 40),
        name="hyena_spectrum",
    )(tsum, tdiff, fa, mfwd)


def _hyena_lat(v, x1, x2, fa, ga, mfwd, minv, kr, ki, bias, seq):
    bsz, t_all, w = v.shape
    n2 = FFT_N2
    ks = ga.shape[1] // 8
    k1n = mfwd.shape[0]
    kgrp = next(g for g in (HY_K1_PER_STEP, 3, 1) if k1n % g == 0)
    ngroups = k1n // kgrp
    tok = pl.BlockSpec((1, seq, w), lambda b, o, p: (b, 0, 0), pipeline_mode=pl.Buffered(1))
    full = lambda shp: pl.BlockSpec(shp, lambda b, o, p: (0,) * len(shp))
    grp = lambda p: jnp.clip(p - 1, 0, ngroups - 1)
    mat = pl.BlockSpec((k1n, 2 * n2, 2 * n2), lambda b, o, p: (0, 0, 0), pipeline_mode=pl.Buffered(1))
    spec = pl.BlockSpec((kgrp, n2, w), lambda b, o, p: (grp(p), 0, o))
    return pl.pallas_call(
        functools.partial(_hy_lat_kernel, k1n, kgrp),
        grid=(bsz, HY_ORDER, ngroups + 2),
        in_specs=[tok, tok, tok, full(fa.shape), full(ga.shape), mat, mat, spec, spec,
                  pl.BlockSpec((1, 1, w), lambda b, o, p: (o, 0, 0))],
        out_specs=pl.BlockSpec((1, 2, seq, 128), lambda b, o, p: (b, 0, 0, 0)),
        out_shape=jax.ShapeDtypeStruct((bsz, 2, seq, 128), F32),
        scratch_shapes=[pltpu.VMEM((2, seq, 128), F32),
                        pltpu.VMEM((2, n2 * HY_A_PITCH, 128), F32),
                        pltpu.VMEM((2, ks * HY_B_PITCH, 128), F32)],
        compiler_params=_cparams(("parallel", "arbitrary", "arbitrary"), V7X_VMEM_LIMIT_MB),
        name="hyena_lat",
    )(v, x1, x2, fa, ga, mfwd, minv, kr, ki, bias)


def _hyena_lat_consts(seq):
    n = 2 * seq
    n2 = FFT_N2
    n1 = n // n2
    h1 = n1 // 2
    k1n = n1 // 2 + 1
    kp = -(-2 * k1n // 8) * 8
    ks = -(-2 * k1n // 16) * 16
    assert kp <= HY_A_PITCH and n2 <= HY_B_PITCH
    k1 = np.arange(k1n)
    m1 = np.arange(h1)
    ang = 2.0 * np.pi * np.outer(k1, m1) / n1
    fa = np.zeros((kp, h1))
    fa[0:2 * k1n:2] = np.cos(ang)
    fa[1:2 * k1n:2] = -np.sin(ang)
    wgt = np.where((k1 == 0) | (k1 == n1 // 2), 1.0, 2.0) / n
    ga = np.zeros((h1, ks))
    ga[:, 0:2 * k1n:2] = (np.cos(ang) * wgt[:, None]).T
    ga[:, 1:2 * k1n:2] = (-np.sin(ang) * wgt[:, None]).T
    k2 = np.arange(n2)
    m2 = np.arange(n2)
    kk = k1[:, None, None] + n1 * k2[None, :, None]
    th = 2.0 * np.pi * ((kk * m2[None, None, :]) % n) / n
    mc, ms = np.cos(th), np.sin(th)
    mfwd = np.concatenate([np.concatenate([mc, ms], axis=2), np.concatenate([-ms, mc], axis=2)], axis=1)
    mct, mst = np.transpose(mc, (0, 2, 1)), np.transpose(ms, (0, 2, 1))
    minv = np.concatenate([np.concatenate([mct, -mst], axis=2), np.concatenate([mst, mct], axis=2)], axis=1)
    eye = np.eye(8)
    fa8 = np.einsum('kn,jJ->jknJ', fa, eye).reshape(8 * kp, 8 * h1)
    ga8 = np.einsum('nk,jJ->njkJ', ga, eye).reshape(8 * h1, 8 * ks)
    return tuple(jnp.asarray(t, BF16) for t in (fa8, ga8, mfwd, minv))


def _hy_ctx_kernel(v_ref, x1_ref, x2_ref, fc_ref, fs_ref, gc_ref, gs_ref, ts_ref, td_ref, bias_ref, z_ref):
    d = functools.partial(jnp.dot, preferred_element_type=F32)
    w = v_ref.shape[2]
    kr_all = d(fc_ref[...], ts_ref[...].astype(BF16))
    ki_all = d(fs_ref[...], td_ref[...].astype(BF16))

    def conv(u, o):
        ub = u.astype(BF16)
        cr, ci = d(fc_ref[...], ub), d(fs_ref[...], ub)
        kr, ki = kr_all[:, o * w:(o + 1) * w], ki_all[:, o * w:(o + 1) * w]
        pr = (cr * kr - ci * ki).astype(BF16)
        pi = (cr * ki + ci * kr).astype(BF16)
        return d(gc_ref[...], pr) + d(gs_ref[...], pi) + u * bias_ref[o]

    z = x1_ref[0].astype(F32) * conv(v_ref[0].astype(F32), 0)
    z = x2_ref[0].astype(F32) * conv(z, 1)
    for j in range(2):
        z_ref[0, j] = z[:, j * 128:(j + 1) * 128]


def _hyena_ctx(v, x1, x2, fc, fs, gc, gs, tsum, tdiff, bias, seq, ctx_len):
    bsz, _, w = v.shape
    blk = seq // ctx_len
    kpad = fc.shape[0]
    tok = pl.BlockSpec((1, ctx_len, w), lambda b: (b, blk, 0))
    full = lambda shp: pl.BlockSpec(shp, lambda b: (0,) * len(shp))
    return pl.pallas_call(
        _hy_ctx_kernel, grid=(bsz,),
        in_specs=[tok, tok, tok, full((kpad, ctx_len)), full((kpad, ctx_len)), full((ctx_len, kpad)),
                  full((ctx_len, kpad)), full((ctx_len, HY_ORDER * w)), full((ctx_len, HY_ORDER * w)), full((2, 1, w))],
        out_specs=pl.BlockSpec((1, 2, ctx_len, 128), lambda b: (b, 0, 0, 0)),
        out_shape=jax.ShapeDtypeStruct((bsz, 2, ctx_len, 128), F32),
        compiler_params=_cparams(("parallel",)),
        name="hyena_ctx",
    )(v, x1, x2, fc, fs, gc, gs, tsum, tdiff, bias)


def _hyena_ctx_consts(ctx_len):
    n = 2 * ctx_len
    nk = ctx_len + 1
    kpad = -(-nk // 128) * 128
    k = np.arange(nk)
    m = np.arange(ctx_len)
    ang = 2.0 * np.pi * np.outer(k, m) / n
    fc = np.zeros((kpad, ctx_len))
    fs = np.zeros((kpad, ctx_len))
    fc[:nk] = np.cos(ang)
    fs[:nk] = -np.sin(ang)
    wgt = np.where((k == 0) | (k == ctx_len), 1.0, 2.0) / n
    gc = np.zeros((ctx_len, kpad))
    gs = np.zeros((ctx_len, kpad))
    gc[:, :nk] = (np.cos(ang) * wgt[:, None]).T
    gs[:, :nk] = (-np.sin(ang) * wgt[:, None]).T
    return tuple(jnp.asarray(t, BF16) for t in (fc, fs, gc, gs))


def _hyena_filter_taps(length, w1, b1, w2, b2, w3, freq, decay):
    pos = jnp.arange(length, dtype=F32)
    t = pos / max(length - 1, 1)
    bands = jnp.linspace(1e-4, HY_BANDS - 1, HY_BANDS, dtype=F32)
    ang = (2.0 * math.pi / length) * pos[:, None] * bands
    feats = jnp.concatenate([t[:, None], jnp.cos(ang), -jnp.sin(ang)], axis=-1)
    freq = freq.astype(F32)
    mm = functools.partial(jnp.matmul, precision=HI)
    hid = jnp.sin(freq * (mm(feats, w1.astype(F32)) + b1.astype(F32)))
    hid = jnp.sin(freq * (mm(hid, w2.astype(F32)) + b2.astype(F32)))
    h = mm(hid, w3.astype(F32)) * jnp.exp(-t[:, None] * jnp.abs(decay.astype(F32)))
    half = HY_ORDER * HY_WIDTH
    h_fwd, h_bwd = h[:, :half], h[:, half:]
    l1 = (jnp.abs(h_fwd[0] + h_bwd[0]) + jnp.sum(jnp.abs(h_fwd[1:]), axis=0) + jnp.sum(jnp.abs(h_bwd[1:]), axis=0))
    return (h_fwd + h_bwd) / l1, (h_fwd - h_bwd) / l1


def _mixout_kernel(nt_lat, n_sub, lat_only, *refs):
    if lat_only:
        (xl_ref, m_ref, s5y_ref, yf_ref, yb_ref, xs_ref, z_ref, hyl_ref,
         wglu_ref, bglu_ref, vec512_ref, wout_ref, win_ref, wout2_ref, ln_ref, o_ref) = refs
    else:
        (xl_ref, xc_ref, m_ref, s5y_ref, yf_ref, yb_ref, xs_ref, z_ref, hyl_ref, hyc_ref,
         wglu_ref, bglu_ref, vec512_ref, wout_ref, win_ref, wout2_ref, ln_ref, o_ref) = refs
        is_lat = pl.program_id(1) < nt_lat
    m = m_ref[0, 0]
    d = functools.partial(jnp.dot, preferred_element_type=F32)
    for sub in range(n_sub):
        r = slice(sub * ROW_TILE, (sub + 1) * ROW_TILE)
        if lat_only:
            x_in = xl_ref[0, r, :]
            hy = [hyl_ref[0, j, r, :].astype(BF16) for j in range(2)]
        else:
            x_in = jnp.where(is_lat, xl_ref[0, r, :], xc_ref[0, r, :])
            hy = [jnp.where(is_lat, hyl_ref[0, j, r, :], hyc_ref[0, j, r, :]).astype(BF16) for j in range(2)]
        y5 = _gelu_tanh(jnp.concatenate([s5y_ref[0, 0, r, :], s5y_ref[0, 1, r, :]], axis=1))
        y5 = y5 * _sigmoid(d(y5.astype(BF16), wglu_ref[...]) + bglu_ref[...])
        ys = (yf_ref[0, 0, r, :].astype(F32) + yb_ref[0, 0, r, :].astype(F32)
              + vec512_ref[0:1, :] * xs_ref[0, r, :].astype(F32))
        gsd = ys * _silu(z_ref[0, r, :].astype(F32))
        gsd = gsd * lax.rsqrt(jnp.mean(gsd * gsd, axis=-1, keepdims=True) + LN_EPS) * vec512_ref[1:2, :]
        mix = (d(y5.astype(BF16), wout_ref[0:256, :]) + d(gsd.astype(BF16), wout_ref[256:768, :])
               + d(hy[0], wout_ref[768:896, :]) + d(hy[1], wout_ref[896:1024, :]))
        x = _standardise(ALPHA * x_in + m[2:3] * mix) * ln_ref[0:1, :] + ln_ref[1:2, :]
        h = (_standardise(x) * (1.0 + m[4:5]) + m[3:4]).astype(BF16)
        gate = d(h, win_ref[:, 0:FFN_HIDDEN])
        up = d(h, win_ref[:, FFN_HIDDEN:2 * FFN_HIDDEN])
        act = (_silu(gate) * up).astype(BF16)
        x = _standardise(ALPHA * x + m[5:6] * d(act, wout2_ref[...]))
        o_ref[0, r, :] = x * ln_ref[2:3, :] + ln_ref[3:4, :]


def _mixout_ffn(x_lat, x_ctx, ctx_blk, mods, s5y, yssd, xbc_c, z, hy_lat, hy_ctx, wglu, bglu, vec512, wout, win,
                wout2, layer, ln, nt_lat, rows):
    bsz = x_lat.shape[0]
    lat_only = rows == nt_lat * ROW_TILE
    n_sub = MIX_SUB_TILES if lat_only and rows % (MIX_SUB_TILES * ROW_TILE) == 0 else 1
    tile = n_sub * ROW_TILE
    tok = lambda w: pl.BlockSpec((1, tile, w), lambda b, i: (b, i, 0))
    halves = pl.BlockSpec((1, 2, tile, 128), lambda b, i: (b, 0, i, 0))
    full = lambda shp: pl.BlockSpec(shp, lambda b, i: (0,) * len(shp))
    const = lambda shp: pl.BlockSpec((None,) + shp, lambda b, i: (layer, 0, 0), pipeline_mode=pl.Buffered(1))
    ssd_dir = lambda dirn: pl.BlockSpec((1, 1, tile, SSD_INNER), lambda b, i: (b, dirn, i, 0))
    if lat_only:
        x_specs, x_args = [tok(D_MODEL)], (x_lat,)
        hy_specs, hy_args = [halves], (hy_lat,)
        mod_spec = pl.BlockSpec((1, 1, 6, D_MODEL), lambda b, i: (b, 0, 0, 0))
    else:
        x_specs, x_args = _row_specs(nt_lat, ctx_blk, D_MODEL), (x_lat, x_ctx)
        hy_specs = [pl.BlockSpec((1, 2, tile, 128), lambda b, i: (b, 0, jnp.minimum(i, nt_lat - 1), 0)),
                    pl.BlockSpec((1, 2, tile, 128), lambda b, i: (b, 0, 0, 0))]
        hy_args = (hy_lat, hy_ctx)
        mod_spec = pl.BlockSpec((1, 1, 6, D_MODEL), lambda b, i: (b, jnp.where(i < nt_lat, 0, 1), 0, 0))
    return pl.pallas_call(
        functools.partial(_mixout_kernel, nt_lat, n_sub, lat_only), grid=(bsz, rows // tile),
        in_specs=x_specs + [mod_spec, halves, ssd_dir(0), ssd_dir(1), tok(SSD_INNER), tok(SSD_INNER)] + hy_specs + [
                  full((256, 256)), full((1, 256)), full((2, 512)),
                  const((D_MODEL, D_MODEL)), const((D_MODEL, 2 * FFN_HIDDEN)), const((FFN_HIDDEN, D_MODEL)),
                  full((4, D_MODEL))],
        out_specs=tok(D_MODEL),
        out_shape=jax.ShapeDtypeStruct((bsz, rows, D_MODEL), F32),
        compiler_params=_cparams(("parallel", "parallel"), V7X_VMEM_LIMIT_MB),
        name="mixout_ffn",
    )(*x_args, mods, s5y, yssd, yssd, xbc_c, z, *hy_args, wglu, bglu, vec512, wout, win, wout2, ln)


def _layer(x_lat, x_ctx, ctx_blk, mods, p, seq, ctx_len, want_ctx):
    nt_lat = seq // ROW_TILE
    t_all = seq + ctx_len
    layer = p["layer"]
    s5u, z, xbc_c, v, x1, x2, dt_raw = _inproj(x_lat, x_ctx, ctx_blk, mods, p["w_in"], layer, p["ssd_conv_w"],
                                               p["ssd_conv_b"], p["hy_conv_w"], p["hy_conv_b"], nt_lat)
    s5y = _s5_scan(s5u, p["s5_ncat"], p["s5_tz"], p["s5_mcat"], p["s5_coef"], p["s5_dvec"],
                   seq // S5_CHUNK, ctx_len // S5_CHUNK)
    yssd = _ssd_scan(xbc_c, dt_raw, p["ssd_tri"], p["ssd_par"], p["ssd_expand"],
                     seq // SSD_CHUNK, ctx_len // SSD_CHUNK)
    hy_lat = _hyena_lat(v, x1, x2, p["hy_fa"], p["hy_ga"], p["hy_mfwd"], p["hy_minv"],
                        p["hy_kr"], p["hy_ki"], p["hy_bias"], seq)
    if want_ctx:
        hy_ctx = _hyena_ctx(v, x1, x2, *p["hy_ctx_mats"], *p["hy_ctx_taps"], p["hy_bias"], seq, ctx_len)
        rows = t_all
    else:
        hy_ctx, rows = hy_lat, seq
    return _mixout_ffn(x_lat, x_ctx, ctx_blk, mods, s5y, yssd, xbc_c, z, hy_lat, hy_ctx, p["s5_wglu"], p["s5_bglu"],
                       p["ssd_vec"], p["w_out"], p["ffn_w_in"], p["ffn_w_out"], layer, p["ln"], nt_lat, rows)


def kernel(x, c, ctx, c_ctx, w_mod, b_mod, w_in, s5_lam_re, s5_lam_im, s5_log_step, s5_b_re, s5_b_im, s5_c_re, s5_c_im, s5_d, s5_w_glu, s5_b_glu, ssd_conv_w, ssd_conv_b, ssd_dt_bias, ssd_a_log, ssd_d, ssd_norm_w, hy_conv_w, hy_conv_b, hy_w1, hy_b1, hy_w2, hy_b2, hy_w3, hy_freq, hy_decay, hy_bias, w_out, ln1_g, ln1_b, ffn_w_in, ffn_w_out, ln2_g, ln2_b):
    bsz, seq, _ = x.shape
    ctx_len = ctx.shape[1]
    assert bsz == 8 and seq % ROW_TILE == 0 and ctx_len == ROW_TILE

    fa, ga, mfwd, minv = _hyena_lat_consts(seq)
    ctx_mats = _hyena_ctx_consts(ctx_len)
    tt = np.arange(SSD_CHUNK)
    tri = jnp.asarray(np.stack([tt[None, :] <= tt[:, None], tt[None, :] >= tt[:, None]]), BF16)
    expand = jnp.asarray(np.repeat(np.eye(128, SSD_HEADS), SSD_HEADDIM, axis=1)[:, :SSD_INNER], BF16)
    cvec = jnp.zeros((16, D_MODEL), F32).at[:bsz].set(c.astype(F32)).at[bsz].set(c_ctx.astype(F32))

    w_p = jnp.concatenate([w_in[..., 0:REF_COL_DT], w_in[..., REF_COL_HY:MIX_IN], w_in[..., REF_COL_DT:REF_COL_HY],
                           jnp.zeros((DEPTH, D_MODEL, MIX_IN_PAD - MIX_IN), w_in.dtype)], axis=-1).astype(BF16)
    w_out_b, ffn_in_b, ffn_out_b = w_out.astype(BF16), ffn_w_in.astype(BF16), ffn_w_out.astype(BF16)

    x_lat, x_ctx, ctx_blk = x.astype(F32), ctx.astype(F32), 0
    for l in range(DEPTH):
        want_ctx = l < DEPTH - 1
        mod16 = _modulation(cvec, w_mod.astype(F32), b_mod[l].astype(F32).reshape(1, -1), l)
        mods = jnp.stack([mod16[:bsz].reshape(bsz, 6, D_MODEL),
                          jnp.broadcast_to(mod16[bsz].reshape(1, 6, D_MODEL), (bsz, 6, D_MODEL))], axis=1)
        ncat, tz, mcat, coef = _s5_weights(s5_lam_re[l], s5_lam_im[l], s5_log_step[l], s5_b_re[l], s5_b_im[l],
                                           s5_c_re[l], s5_c_im[l])
        rep = lambda t: jnp.repeat(t.astype(F32), SSD_HEADDIM, axis=-1)
        par = jnp.zeros((2, 8, 128), F32)
        par = par.at[:, 0, :SSD_HEADS].set(ssd_dt_bias[l].astype(F32))
        par = par.at[:, 1, :SSD_HEADS].set(-jnp.exp(ssd_a_log[l].astype(F32)))
        hy_args = (hy_w1[l], hy_b1[l], hy_w2[l], hy_b2[l], hy_w3[l], hy_freq[l], hy_decay[l])
        kr, ki = _hyena_spectrum(*_hyena_filter_taps(seq, *hy_args), fa, mfwd)
        p = dict(
            layer=l, w_in=w_p, s5_ncat=ncat, s5_tz=tz, s5_mcat=mcat, s5_coef=coef,
            ssd_conv_w=ssd_conv_w[l].astype(F32), ssd_conv_b=ssd_conv_b[l].astype(F32),
            ssd_tri=tri, ssd_par=par, ssd_expand=expand,
            hy_conv_w=hy_conv_w[l].astype(F32), hy_conv_b=hy_conv_b[l].astype(F32),
            hy_fa=fa, hy_ga=ga, hy_mfwd=mfwd, hy_minv=minv, hy_kr=kr, hy_ki=ki,
            hy_bias=hy_bias[l].astype(F32)[:, None, :],
            s5_wglu=s5_w_glu[l].astype(BF16),
            s5_bglu=s5_b_glu[l].astype(F32).reshape(1, S5_WIDTH),
            s5_dvec=jnp.tile(s5_d[l].astype(F32).reshape(S5_GROUPS, 1, S5_GROUP_CH), (1, 1, S5_CHUNK)),
            ssd_vec=jnp.stack([rep(ssd_d[l]), ssd_norm_w[l].astype(F32)]),
            w_out=w_out_b, ln=jnp.stack([ln1_g[l], ln1_b[l], ln2_g[l], ln2_b[l]]).astype(F32),
            ffn_w_in=ffn_in_b, ffn_w_out=ffn_out_b,
        )
        if want_ctx:
            p.update(hy_ctx_mats=ctx_mats, hy_ctx_taps=_hyena_filter_taps(ctx_len, *hy_args))
        x_lat = _layer(x_lat, x_ctx, ctx_blk, mods, p, seq, ctx_len, want_ctx)
        x_ctx, ctx_blk = x_lat, seq // ROW_TILE
    return x_lat.astype(x.dtype)
```

```python
import functools
import math

import numpy as np
import jax
import jax.numpy as jnp
from jax import lax
from jax.experimental import pallas as pl
from jax.experimental.pallas import tpu as pltpu

F32 = jnp.float32
BF16 = jnp.bfloat16
HI = lax.Precision.HIGHEST

D_MODEL = 1024
DEPTH = 2
S5_WIDTH = 256
S5_GROUP_CH = 16
S5_GROUPS = 16
S5_MAX_RE = -1e-4
S5_CHUNK = 16
SSD_INNER = 512
SSD_HEADDIM = 64
SSD_HEADS = 8
SSD_GROUPS = 2
SSD_STATE = 128
SSD_CONV = 5
SSD_CHUNK = 128
SSD_CHUNKS_PER_STEP = 2
SSD_XBC = 1024
HY_WIDTH = 256
HY_ORDER = 2
HY_SHORT = 3
HY_BANDS = 16
HY_IN = 768
COL_Z = S5_WIDTH
COL_XBC = COL_Z + SSD_INNER
COL_HY = COL_XBC + SSD_XBC
COL_DT = COL_HY + HY_IN
MIX_IN_PAD = COL_DT + 128
REF_COL_DT = COL_HY
REF_COL_HY = REF_COL_DT + SSD_HEADS
MIX_IN = REF_COL_HY + HY_IN
FFN_HIDDEN = 2816
ALPHA = (2 * DEPTH) ** 0.25
LN_EPS = 1e-6

ROW_TILE = 256
MIX_SUB_TILES = 2
CONV_GROUP = 256
HALO = 16
FFT_N2 = 128
HY_A_PITCH = 72
HY_K1_PER_STEP = 11
HY_B_PITCH = 136
V7X_VMEM_LIMIT_MB = 56


def _cparams(sem, vmem_mb=None):
    kw = dict(dimension_semantics=sem)
    if vmem_mb is not None:
        kw["vmem_limit_bytes"] = vmem_mb * 2 ** 20
    return pltpu.CompilerParams(**kw)


def _standardise(x):
    mu = jnp.mean(x, axis=-1, keepdims=True)
    xc = x - mu
    var = jnp.mean(xc * xc, axis=-1, keepdims=True)
    return xc * lax.rsqrt(var + LN_EPS)


def _sigmoid(x):
    return 1.0 / (1.0 + jnp.exp(-x))


def _silu(x):
    return x * _sigmoid(x)


def _gelu_tanh(x):
    return 0.5 * x * (1.0 + jnp.tanh(0.7978845608028654 * (x + 0.044715 * (x * x * x))))


def _softplus(x):
    return jnp.maximum(x, 0.0) + jnp.log(1.0 + jnp.exp(-jnp.abs(x)))


def _split3(a):
    a1 = a.astype(BF16)
    r1 = a - a1.astype(F32)
    a2 = r1.astype(BF16)
    a3 = (r1 - a2.astype(F32)).astype(BF16)
    return a1, a2, a3


def _dot_sel_lhs(sel, a):
    a1, a2, a3 = _split3(a)
    d = functools.partial(jnp.dot, preferred_element_type=F32)
    return d(sel, a1) + d(sel, a2) + d(sel, a3)


def _mod_kernel(c_ref, w_ref, b_ref, o_ref):
    ca = _silu(c_ref[...])
    c1, c2, c3 = _split3(ca)
    w1, w2, w3 = _split3(w_ref[...])
    d = functools.partial(jnp.dot, preferred_element_type=F32)
    acc = d(c1, w1) + d(c1, w2) + d(c2, w1) + d(c1, w3) + d(c2, w2) + d(c3, w1)
    o_ref[...] = acc + b_ref[...]


def _modulation(cvec, w, b, layer):
    n = w.shape[2]
    tn = 1536
    return pl.pallas_call(
        _mod_kernel,
        grid=(n // tn,),
        in_specs=[pl.BlockSpec((16, D_MODEL), lambda j: (0, 0)),
                  pl.BlockSpec((None, D_MODEL, tn), lambda j: (layer, 0, j)),
                  pl.BlockSpec((1, tn), lambda j: (0, j))],
        out_specs=pl.BlockSpec((16, tn), lambda j: (0, j)),
        out_shape=jax.ShapeDtypeStruct((16, n), F32),
        compiler_params=_cparams(("arbitrary",), 40),
        name="modulation",
    )(cvec, w, b)


def _rows_of(nt_lat, xl_ref, xc_ref):
    return jnp.where(pl.program_id(1) < nt_lat, xl_ref[0], xc_ref[0])


def _row_specs(nt_lat, ctx_blk, width):
    return [pl.BlockSpec((1, ROW_TILE, width), lambda b, i: (b, jnp.minimum(i, nt_lat - 1), 0)),
            pl.BlockSpec((1, ROW_TILE, width), lambda b, i: (b, ctx_blk, 0))]


def _dwconv_rows(ext_ref, w_ref, b_ref, wcol, taps, act, o_ref, ocol):
    pad = taps // 2
    rb = 64
    for c in range(0, CONV_GROUP, 128):
        wk = [w_ref[k:k + 1, wcol + c:wcol + c + 128] for k in range(taps)]
        bias = b_ref[0:1, wcol + c:wcol + c + 128]
        for r0 in range(0, ROW_TILE, rb):
            blk = ext_ref[HALO - 8 + r0:HALO + 8 + r0 + rb, c:c + 128]
            acc = bias + wk[pad] * blk[8:8 + rb]
            for k in range(taps):
                if k != pad:
                    acc = acc + wk[k] * pltpu.roll(blk, (pad - k) % (rb + 16), axis=0)[8:8 + rb]
            if act:
                acc = _silu(acc)
            o_ref[0, r0:r0 + rb, ocol + c:ocol + c + 128] = acc.astype(o_ref.dtype)


def _inproj_kernel(nt_lat, xl_ref, xc_ref, xp_ref, xn_ref, m_ref, w_ref, w5_ref, b5_ref, w3_ref, b3_ref,
                   s5_ref, z_ref, xbc_ref, v_ref, x1_ref, x2_ref, dt_ref, *ext_refs):
    i = pl.program_id(1)
    m = m_ref[0, 0]
    mod = lambda x: (_standardise(x) * (1.0 + m[1:2]) + m[0:1]).astype(BF16)
    d = functools.partial(jnp.dot, preferred_element_type=F32)
    h = mod(_rows_of(nt_lat, xl_ref, xc_ref))
    hp, hn = mod(xp_ref[0]), mod(xn_ref[0])
    has_prev = jnp.logical_and(i > 0, i < nt_lat)
    has_next = i < nt_lat - 1
    n5 = SSD_XBC // CONV_GROUP
    hy_outs = (v_ref, x1_ref, x2_ref)

    def project(k):
        wc = w_ref[:, COL_XBC + k * CONV_GROUP:COL_XBC + (k + 1) * CONV_GROUP]
        e = ext_refs[k]
        e[0:HALO, :] = jnp.where(has_prev, d(hp, wc), 0.0)
        e[HALO:HALO + ROW_TILE, :] = d(h, wc)
        e[HALO + ROW_TILE:2 * HALO + ROW_TILE, :] = jnp.where(has_next, d(hn, wc), 0.0)

    def conv(k):
        if k < n5:
            _dwconv_rows(ext_refs[k], w5_ref, b5_ref, k * CONV_GROUP, SSD_CONV, True, xbc_ref, k * CONV_GROUP)
        else:
            _dwconv_rows(ext_refs[k], w3_ref, b3_ref, (k - n5) * CONV_GROUP, HY_SHORT, False, hy_outs[k - n5], 0)

    ngroups = len(ext_refs)
    project(0)
    for k in range(ngroups):
        if k + 1 < ngroups:
            project(k + 1)
        conv(k)
    s5_ref[0] = d(h, w_ref[:, 0:COL_Z])
    z_ref[0] = d(h, w_ref[:, COL_Z:COL_XBC]).astype(z_ref.dtype)
    dt_ref[0] = d(h, w_ref[:, COL_DT:MIX_IN_PAD])


def _inproj(x_lat, x_ctx, ctx_blk, mods, w_p, layer, conv5_w, conv5_b, conv3_w, conv3_b, nt_lat):
    bsz = x_lat.shape[0]
    nt = nt_lat + 1
    t_all = nt * ROW_TILE
    rh = ROW_TILE // HALO
    last = nt_lat * rh - 1
    full = lambda shp: pl.BlockSpec(shp, lambda b, i: (0,) * len(shp))
    outs = ((256, F32), (512, BF16), (SSD_XBC, BF16), (HY_WIDTH, BF16), (HY_WIDTH, BF16), (HY_WIDTH, BF16), (128, F32))
    return pl.pallas_call(
        functools.partial(_inproj_kernel, nt_lat),
        grid=(bsz, nt),
        in_specs=_row_specs(nt_lat, ctx_blk, D_MODEL) + [
                  pl.BlockSpec((1, HALO, D_MODEL), lambda b, i: (b, jnp.clip(i * rh - 1, 0, last), 0)),
                  pl.BlockSpec((1, HALO, D_MODEL), lambda b, i: (b, jnp.clip((i + 1) * rh, 0, last), 0)),
                  pl.BlockSpec((1, 1, 6, D_MODEL), lambda b, i: (b, jnp.where(i < nt_lat, 0, 1), 0, 0)),
                  pl.BlockSpec((None, D_MODEL, MIX_IN_PAD), lambda b, i: (layer, 0, 0)), full((SSD_CONV, SSD_XBC)), full((1, SSD_XBC)),
                  full((HY_SHORT, HY_IN)), full((1, HY_IN))],
        out_specs=[pl.BlockSpec((1, ROW_TILE, w), lambda b, i: (b, i, 0)) for w, _ in outs],
        out_shape=[jax.ShapeDtypeStruct((bsz, t_all, w), dt) for w, dt in outs],
        scratch_shapes=[pltpu.VMEM((ROW_TILE + 2 * HALO, CONV_GROUP), F32)] * ((SSD_XBC + HY_IN) // CONV_GROUP),
        compiler_params=_cparams(("parallel", "parallel"), 40),
        name="inproj",
    )(x_lat, x_ctx, x_lat, x_lat, mods, w_p, conv5_w, conv5_b.reshape(1, -1), conv3_w, conv3_b.reshape(1, -1))


def _s5_kernel(n_lat, n_ctx, u0_ref, u1_ref, ncat_ref, tz_ref, mcat_ref, coef_ref, dvec_ref, y_ref,
               ut_ref, s_ref, hp_ref):
    q, gch, ng = S5_CHUNK, S5_GROUP_CH, S5_GROUPS
    nc = n_lat + n_ctx
    dot = functools.partial(jnp.dot, preferred_element_type=F32)
    u_refs = (u0_ref, u1_ref)
    gpl = 128 // gch
    for j in range(2):
        for s in range(q):
            rows = u_refs[j][0, pl.ds(s, nc, stride=q), :]
            for gg in range(gpl):
                ut_ref[gpl * j + gg, :, s * gch:(s + 1) * gch] = rows[:, gg * gch:(gg + 1) * gch]
    for g in range(ng):
        sg = dot(ut_ref[g].astype(BF16), ncat_ref[g])
        for k in range(4):
            s_ref[k, pl.ds(g, nc, stride=ng), :] = sg[:, k * 128:(k + 1) * 128]
    c1f, c2f, c1b, c2b = coef_ref[0], coef_ref[1], coef_ref[2], coef_ref[3]

    def step(cf, cb, carry):
        hf, hsf, hb, hsb = carry
        rf = pl.multiple_of(cf * ng, ng)
        rb = pl.multiple_of(cb * ng, ng)
        hp_ref[0, pl.ds(rf, ng), :] = hf
        hp_ref[1, pl.ds(rb, ng), :] = hb
        sf = s_ref[0, pl.ds(rf, ng), :]
        sb = s_ref[1, pl.ds(rb, ng), :]
        ssf = s_ref[2, pl.ds(rf, ng), :]
        ssb = s_ref[3, pl.ds(rb, ng), :]
        return (c1f * hf + c2f * hsf + sf, c1f * hsf - c2f * hf + ssf,
                c1b * hb + c2b * hsb + sb, c1b * hsb - c2b * hb + ssb)

    z = jnp.zeros((ng, 128), F32)
    carry = lax.fori_loop(0, n_ctx, lambda i, c: step(n_lat + i, n_lat + n_ctx - 1 - i, c), (z, z, z, z))
    lax.fori_loop(0, n_lat, lambda i, c: step(i, n_lat - 1 - i, c), carry)
    for g in range(ng):
        ug = ut_ref[g]
        hp = jnp.concatenate([hp_ref[0, pl.ds(g, nc, stride=ng), :], hp_ref[1, pl.ds(g, nc, stride=ng), :]], axis=1)
        ut_ref[g] = dot(ug.astype(BF16), tz_ref[g]) + dot(hp.astype(BF16), mcat_ref[g]) + ug * dvec_ref[g]
    for j in range(2):
        for s in range(q):
            rows = jnp.concatenate([ut_ref[gpl * j + gg, :, s * gch:(s + 1) * gch] for gg in range(gpl)], axis=1)
            y_ref[0, j, pl.ds(s, nc, stride=q), :] = rows


def _s5_scan(s5u, ncat, tz, mcat, coef, dvec, n_lat, n_ctx):
    bsz, t_all, _ = s5u.shape
    nc = n_lat + n_ctx
    ng = S5_GROUPS
    const = lambda shp: pl.BlockSpec(shp, lambda b: (0,) * len(shp), pipeline_mode=pl.Buffered(1))
    return pl.pallas_call(
        functools.partial(_s5_kernel, n_lat, n_ctx),
        grid=(bsz,),
        in_specs=[pl.BlockSpec((1, t_all, 128), lambda b: (b, 0, 0)),
                  pl.BlockSpec((1, t_all, 128), lambda b: (b, 0, 1)),
                  const((ng, 256, 512)), const((ng, 256, 256)), const((ng, 256, 256)),
                  const((4, ng, 128)), const((ng, 1, 256))],
        out_specs=pl.BlockSpec((1, 2, t_all, 128), lambda b: (b, 0, 0, 0)),
        out_shape=jax.ShapeDtypeStruct((bsz, 2, t_all, 128), F32),
        scratch_shapes=[pltpu.VMEM((ng, nc, 256), F32), pltpu.VMEM((4, nc * ng, 128), F32),
                        pltpu.VMEM((2, nc * ng, 128), F32)],
        compiler_params=_cparams(("parallel",), V7X_VMEM_LIMIT_MB),
        name="s5_scan",
    )(s5u, s5u, ncat, tz, mcat, coef, dvec)


def _s5_weights(lam_re, lam_im, log_step, b_re, b_im, c_re, c_im):
    q, ng, gch = S5_CHUNK, S5_GROUPS, S5_GROUP_CH
    a_re = jnp.minimum(lam_re.astype(F32), S5_MAX_RE)
    a_im = lam_im.astype(F32)
    step = jnp.exp(log_step.astype(F32))[..., None]
    taus = jnp.arange(q + 1, dtype=F32)[:, None, None, None]
    mag = jnp.exp(taus * (a_re * step))
    pr, pi = mag * jnp.cos(taus * (a_im * step)), mag * jnp.sin(taus * (a_im * step))
    nr, ni, den = pr[1] - 1.0, pi[1], a_re * a_re + a_im * a_im
    fr, fi = ((nr * a_re + ni * a_im) / den)[..., None], ((ni * a_re - nr * a_im) / den)[..., None]
    b_r, b_i = b_re.astype(F32), b_im.astype(F32)
    bb_r, bb_i = fr * b_r - fi * b_i, fr * b_i + fi * b_r
    c_r = jnp.swapaxes(c_re.astype(F32), -1, -2)
    c_i = jnp.swapaxes(c_im.astype(F32), -1, -2)
    pw_r, pw_i = jnp.transpose(pr, (1, 2, 3, 0)), jnp.transpose(pi, (1, 2, 3, 0))

    def lagged(v_r, v_i, d, lags):
        e_r, e_i = jnp.repeat(pw_r[d][..., lags], gch, axis=-1), jnp.repeat(pw_i[d][..., lags], gch, axis=-1)
        n = e_r.shape[-1] // gch
        t_r, t_i = jnp.tile(v_r[d], (1, 1, n)), jnp.tile(v_i[d], (1, 1, n))
        return t_r * e_r - t_i * e_i, t_r * e_i + t_i * e_r

    def lag_kernels(d, lags):
        cp_r, cp_i = lagged(c_r, c_i, d, lags)
        return (jnp.einsum('gpk,gpn->gkn', bb_r[d], cp_r, precision=HI)
                - jnp.einsum('gpk,gpn->gkn', bb_i[d], cp_i, precision=HI))

    k_fwd = lag_kernels(0, slice(0, q))
    k_bwd = lag_kernels(1, slice(q - 1, None, -1))
    edge = (q - 1) * gch
    kfull = jnp.concatenate([k_bwd[..., :edge], k_bwd[..., edge:] + k_fwd[..., :gch], k_fwd[..., gch:]], axis=-1)
    tz = jnp.stack([kfull[:, :, (q - 1 - s) * gch:(2 * q - 1 - s) * gch] for s in range(q)],
                   axis=1).reshape(ng, q * gch, q * gch)
    nf_r, nf_i = (jnp.swapaxes(t, -1, -2) for t in lagged(bb_r, bb_i, 0, slice(q - 1, None, -1)))
    nb_r, nb_i = (jnp.swapaxes(t, -1, -2) for t in lagged(bb_r, bb_i, 1, slice(0, q)))
    ncat = jnp.concatenate([nf_r, nf_i, nb_r, nb_i, nf_i, nf_r, nb_i, nb_r], axis=-1)
    mf_r, mf_i = lagged(c_r, c_i, 0, slice(1, None))
    mb_r, mb_i = lagged(c_r, c_i, 1, slice(q, 0, -1))
    mcat = jnp.concatenate([mf_r, -mf_i, mb_r, -mb_i], axis=1)
    lr, li = pr[q], pi[q]
    coef = jnp.stack([jnp.concatenate([lr[0], lr[0]], axis=-1), jnp.concatenate([-li[0], li[0]], axis=-1),
                      jnp.concatenate([lr[1], lr[1]], axis=-1), jnp.concatenate([-li[1], li[1]], axis=-1)])
    return ncat.astype(BF16), tz.astype(BF16), mcat.astype(BF16), coef.astype(F32)


def _ssd_kernel(cps, xbc_ref, dt_ref, tri_ref, par_ref, exp_ref, y_ref, st_ref):
    direction = pl.program_id(1)

    @pl.when(pl.program_id(2) == 0)
    def _():
        st_ref[...] = jnp.zeros_like(st_ref)

    tri = tri_ref[0]
    mask = tri > 0
    lane = lax.broadcasted_iota(jnp.int32, (SSD_CHUNK, 128), 1)
    lo = lane < SSD_HEADDIM
    zero_b = jnp.zeros((SSD_CHUNK, 128), BF16)
    head_of_lane = lax.broadcasted_iota(jnp.int32, (1, SSD_INNER), 1) // SSD_HEADDIM
    expand = lambda t: jnp.dot(t.astype(BF16), exp_ref[...], preferred_element_type=F32)
    state = [st_ref[:, g * 256:(g + 1) * 256] for g in range(SSD_GROUPS)]
    for c in range(cps):
        sub = jnp.where(direction == 0, c, cps - 1 - c)
        rows = pl.ds(pl.multiple_of(sub * SSD_CHUNK, SSD_CHUNK), SSD_CHUNK)
        xs = xbc_ref[0, rows, 0:512].astype(F32)
        dt_c = _softplus(dt_ref[0, rows, :] + par_ref[0, 0:1, :])
        a_c = par_ref[0, 1:2, :] * dt_c
        cs = _dot_sel_lhs(tri, a_c)
        tot = jnp.sum(a_c, axis=0, keepdims=True)
        cs_t = cs.T
        xdt = xs * expand(dt_c)
        xdt_b = xdt.astype(BF16)
        xd_end = (xdt * expand(jnp.exp(tot - cs))).astype(BF16)
        e_cs = expand(jnp.exp(cs))
        tot_full = jnp.zeros((1, SSD_INNER), F32)
        for head in range(SSD_HEADS):
            tot_full = jnp.where(head_of_lane == head, tot[:, head:head + 1], tot_full)
        e_tot = jnp.exp(tot_full)
        for g in range(SSD_GROUPS):
            bm_b = xbc_ref[0, rows, 512 + g * 128:512 + (g + 1) * 128]
            cm_b = xbc_ref[0, rows, 768 + g * 128:768 + (g + 1) * 128]
            cb = lax.dot_general(cm_b, bm_b, (((1,), (1,)), ((), ())), preferred_element_type=F32)
            y_off = jnp.dot(cm_b, state[g].astype(BF16), preferred_element_type=F32) * e_cs[:, g * 256:(g + 1) * 256]
            for j in range(2):
                c0 = g * 256 + j * 128
                x_pair = xdt_b[:, c0:c0 + 128]
                acc = y_off[:, j * 128:(j + 1) * 128]
                for hh in range(2):
                    head = c0 // SSD_HEADDIM + hh
                    decay = jnp.exp(jnp.where(mask, cs[:, head:head + 1] - cs_t[head:head + 1, :], -1e30))
                    gm = (cb * decay).astype(BF16)
                    xh = jnp.where(lo if hh == 0 else jnp.logical_not(lo), x_pair, zero_b)
                    acc = acc + jnp.dot(gm, xh, preferred_element_type=F32)
                y_ref[0, 0, rows, c0:c0 + 128] = acc.astype(y_ref.dtype)
            state[g] = (state[g] * e_tot[:, g * 256:(g + 1) * 256]
                        + jnp.dot(bm_b.astype(F32).T.astype(BF16), xd_end[:, g * 256:(g + 1) * 256],
                                  preferred_element_type=F32))
    for g in range(SSD_GROUPS):
        st_ref[:, g * 256:(g + 1) * 256] = state[g]


def _ssd_scan(xbc_c, dt_raw, tri, par, expand, n_lat, n_ctx):
    bsz, t_all, _ = xbc_c.shape
    cps = SSD_CHUNKS_PER_STEP
    assert n_lat % cps == 0 and n_ctx % cps == 0
    n_lat, n_ctx = n_lat // cps, n_ctx // cps
    nc = n_lat + n_ctx
    rows = cps * SSD_CHUNK

    def blk(d, i):
        fwd = jnp.where(i < n_ctx, n_lat + i, i - n_ctx)
        return jnp.where(d == 0, fwd, nc - 1 - i)

    return pl.pallas_call(
        functools.partial(_ssd_kernel, cps),
        grid=(bsz, 2, nc),
        in_specs=[pl.BlockSpec((1, rows, SSD_XBC), lambda b, d, i: (b, blk(d, i), 0)),
                  pl.BlockSpec((1, rows, 128), lambda b, d, i: (b, blk(d, i), 0)),
                  pl.BlockSpec((1, SSD_CHUNK, SSD_CHUNK), lambda b, d, i: (d, 0, 0)),
                  pl.BlockSpec((1, 8, 128), lambda b, d, i: (d, 0, 0)),
                  pl.BlockSpec((128, SSD_INNER), lambda b, d, i: (0, 0))],
        out_specs=pl.BlockSpec((1, 1, rows, SSD_INNER), lambda b, d, i: (b, d, blk(d, i), 0)),
        out_shape=jax.ShapeDtypeStruct((bsz, 2, t_all, SSD_INNER), BF16),
        scratch_shapes=[pltpu.VMEM((SSD_STATE, SSD_INNER), F32)],
        compiler_params=_cparams(("parallel", "parallel", "arbitrary")),
        name="ssd_scan",
    )(xbc_c, dt_raw, tri, par, expand)


def _hy_outer_forward(tile, fa_ref, a2_ref, h1, kp):
    dot = functools.partial(jnp.dot, preferred_element_type=F32)
    fa = fa_ref[...]

    def body(i8, carry):
        r0 = pl.multiple_of(i8 * 8, 8)
        x = jnp.concatenate([tile(n1, r0) for n1 in range(h1)], axis=0)
        out = dot(fa, x.astype(BF16))
        for j in range(8):
            row = pl.multiple_of((i8 * 8 + j) * HY_A_PITCH, 8)
            for s in range(2):
                a2_ref[s, pl.ds(row, kp), :] = out[j * kp:(j + 1) * kp, s * 128:(s + 1) * 128]
        return carry

    lax.fori_loop(0, FFT_N2 // 8, body, 0, unroll=2)


def _hy_lat_kernel(k1n, kgrp, v_ref, x1_ref, x2_ref, fa_ref, ga_ref, mf_ref, mi_ref, kr_ref, ki_ref, bias_ref,
                   o_ref, z_ref, a2_ref, b2_ref):
    order = pl.program_id(1)
    ph = pl.program_id(2)
    n2 = FFT_N2
    kp, h1 = fa_ref.shape[0] // 8, fa_ref.shape[1] // 8
    ks = ga_ref.shape[1] // 8
    seq = h1 * n2
    ngroups = k1n // kgrp
    dot = functools.partial(jnp.dot, preferred_element_type=F32)

    @pl.when(jnp.logical_and(order == 0, ph == 0))
    def _():
        for j in range(2):
            z_ref[j] = v_ref[0, :, j * 128:(j + 1) * 128].astype(F32)
            b2_ref[j, 2 * k1n * HY_B_PITCH:ks * HY_B_PITCH, :] = jnp.zeros(((ks - 2 * k1n) * HY_B_PITCH, 128), F32)

    @pl.when(ph == 0)
    def _():
        tile = lambda n1, r0: jnp.concatenate([z_ref[j, pl.ds(n1 * n2 + r0, 8), :] for j in range(2)], axis=1)
        _hy_outer_forward(tile, fa_ref, a2_ref, h1, kp)

    @pl.when(jnp.logical_and(ph >= 1, ph <= ngroups))
    def _():
        for t in range(kgrp):
            k1 = (ph - 1) * kgrp + t
            are = jnp.concatenate([a2_ref[j, pl.ds(2 * k1, n2, stride=HY_A_PITCH), :] for j in range(2)], axis=1)
            aim = jnp.concatenate([a2_ref[j, pl.ds(2 * k1 + 1, n2, stride=HY_A_PITCH), :] for j in range(2)], axis=1)
            x = dot(mf_ref[k1], jnp.concatenate([are, aim], axis=0).astype(BF16))
            xr, xi = x[:n2], x[n2:]
            kr, ki = kr_ref[t].astype(F32), ki_ref[t].astype(F32)
            y = jnp.concatenate([xr * kr - xi * ki, xr * ki + xi * kr], axis=0).astype(BF16)
            bq = dot(mi_ref[k1], y)
            row = pl.multiple_of(2 * k1 * HY_B_PITCH, 8)
            for j in range(2):
                b2_ref[j, pl.ds(row, n2), :] = bq[:n2, j * 128:(j + 1) * 128]
                b2_ref[j, pl.ds(row + HY_B_PITCH, n2), :] = bq[n2:, j * 128:(j + 1) * 128]

    @pl.when(ph == ngroups + 1)
    def _():
        ga = ga_ref[...]

        def body(i8, carry):
            r0 = pl.multiple_of(i8 * 8, 8)
            rows = jnp.concatenate(
                [jnp.concatenate([b2_ref[j, pl.ds(k * HY_B_PITCH + r0, 8), :] for j in range(2)], axis=1)
                 for k in range(ks)], axis=0)
            y = dot(ga, rows.astype(BF16))
            for n1 in range(h1):
                for j in range(2):
                    o_ref[0, j, pl.ds(n1 * n2 + r0, 8), :] = y[n1 * 8:(n1 + 1) * 8, j * 128:(j + 1) * 128]
            return carry

        lax.fori_loop(0, n2 // 8, body, 0, unroll=2)
        rb = min(256, seq)

        def gate(i, carry):
            r0 = pl.multiple_of(i * rb, 8)
            for j in range(2):
                conv = o_ref[0, j, pl.ds(r0, rb), :] + z_ref[j, pl.ds(r0, rb), :] * bias_ref[0, :, j * 128:(j + 1) * 128]
                x1 = x1_ref[0, pl.ds(r0, rb), j * 128:(j + 1) * 128].astype(F32)
                x2 = x2_ref[0, pl.ds(r0, rb), j * 128:(j + 1) * 128].astype(F32)
                z_ref[j, pl.ds(r0, rb), :] = x1 * conv
                o_ref[0, j, pl.ds(r0, rb), :] = x2 * conv
            return carry

        lax.fori_loop(0, seq // rb, gate, 0)


def _hy_spec_kernel(k1n, ts_ref, td_ref, fa_ref, mf_ref, kr_ref, ki_ref, a2_ref):
    n2 = FFT_N2
    kp, h1 = fa_ref.shape[0] // 8, fa_ref.shape[1] // 8
    dot = functools.partial(jnp.dot, preferred_element_type=F32)
    tile = lambda n1, r0: jnp.concatenate([ts_ref[pl.ds(n1 * n2 + r0, 8), :], td_ref[pl.ds(n1 * n2 + r0, 8), :]], axis=1)
    _hy_outer_forward(tile, fa_ref, a2_ref, h1, kp)

    def per_k1(k1, carry):
        are = jnp.concatenate([a2_ref[j, pl.ds(2 * k1, n2, stride=HY_A_PITCH), :] for j in range(2)], axis=1)
        aim = jnp.concatenate([a2_ref[j, pl.ds(2 * k1 + 1, n2, stride=HY_A_PITCH), :] for j in range(2)], axis=1)
        x = dot(mf_ref[k1], jnp.concatenate([are, aim], axis=0).astype(BF16))
        kr_ref[k1] = x[:n2, 0:128].astype(kr_ref.dtype)
        ki_ref[k1] = x[n2:, 128:256].astype(ki_ref.dtype)
        return carry

    lax.fori_loop(0, k1n, per_k1, 0)


def _hyena_spectrum(tsum, tdiff, fa, mfwd):
    seq, lanes = tsum.shape
    n2 = FFT_N2
    k1n = mfwd.shape[0]
    taps = pl.BlockSpec((seq, 128), lambda j: (0, j))
    out = pl.BlockSpec((k1n, n2, 128), lambda j: (0, 0, j))
    return pl.pallas_call(
        functools.partial(_hy_spec_kernel, k1n),
        grid=(lanes // 128,),
        in_specs=[taps, taps, pl.BlockSpec(fa.shape, lambda j: (0, 0)),
                  pl.BlockSpec((k1n, 2 * n2, 2 * n2), lambda j: (0, 0, 0))],
        out_specs=[out, out],
        out_shape=[jax.ShapeDtypeStruct((k1n, n2, lanes), BF16)] * 2,
        scratch_shapes=[pltpu.VMEM((2, n2 * HY_A_PITCH, 128), F32)],
        compiler_params=_cparams(("parallel",), 40),
        name="hyena_spectrum",
    )(tsum, tdiff, fa, mfwd)


def _hyena_lat(v, x1, x2, fa, ga, mfwd, minv, kr, ki, bias, seq):
    bsz, t_all, w = v.shape
    n2 = FFT_N2
    ks = ga.shape[1] // 8
    k1n = mfwd.shape[0]
    kgrp = next(g for g in (HY_K1_PER_STEP, 3, 1) if k1n % g == 0)
    ngroups = k1n // kgrp
    tok = pl.BlockSpec((1, seq, w), lambda b, o, p: (b, 0, 0), pipeline_mode=pl.Buffered(1))
    full = lambda shp: pl.BlockSpec(shp, lambda b, o, p: (0,) * len(shp))
    grp = lambda p: jnp.clip(p - 1, 0, ngroups - 1)
    mat = pl.BlockSpec((k1n, 2 * n2, 2 * n2), lambda b, o, p: (0, 0, 0), pipeline_mode=pl.Buffered(1))
    spec = pl.BlockSpec((kgrp, n2, w), lambda b, o, p: (grp(p), 0, o))
    return pl.pallas_call(
        functools.partial(_hy_lat_kernel, k1n, kgrp),
        grid=(bsz, HY_ORDER, ngroups + 2),
        in_specs=[tok, tok, tok, full(fa.shape), full(ga.shape), mat, mat, spec, spec,
                  pl.BlockSpec((1, 1, w), lambda b, o, p: (o, 0, 0))],
        out_specs=pl.BlockSpec((1, 2, seq, 128), lambda b, o, p: (b, 0, 0, 0)),
        out_shape=jax.ShapeDtypeStruct((bsz, 2, seq, 128), F32),
        scratch_shapes=[pltpu.VMEM((2, seq, 128), F32),
                        pltpu.VMEM((2, n2 * HY_A_PITCH, 128), F32),
                        pltpu.VMEM((2, ks * HY_B_PITCH, 128), F32)],
        compiler_params=_cparams(("parallel", "arbitrary", "arbitrary"), V7X_VMEM_LIMIT_MB),
        name="hyena_lat",
    )(v, x1, x2, fa, ga, mfwd, minv, kr, ki, bias)


def _hyena_lat_consts(seq):
    n = 2 * seq
    n2 = FFT_N2
    n1 = n // n2
    h1 = n1 // 2
    k1n = n1 // 2 + 1
    kp = -(-2 * k1n // 8) * 8
    ks = -(-2 * k1n // 16) * 16
    assert kp <= HY_A_PITCH and n2 <= HY_B_PITCH
    k1 = np.arange(k1n)
    m1 = np.arange(h1)
    ang = 2.0 * np.pi * np.outer(k1, m1) / n1
    fa = np.zeros((kp, h1))
    fa[0:2 * k1n:2] = np.cos(ang)
    fa[1:2 * k1n:2] = -np.sin(ang)
    wgt = np.where((k1 == 0) | (k1 == n1 // 2), 1.0, 2.0) / n
    ga = np.zeros((h1, ks))
    ga[:, 0:2 * k1n:2] = (np.cos(ang) * wgt[:, None]).T
    ga[:, 1:2 * k1n:2] = (-np.sin(ang) * wgt[:, None]).T
    k2 = np.arange(n2)
    m2 = np.arange(n2)
    kk = k1[:, None, None] + n1 * k2[None, :, None]
    th = 2.0 * np.pi * ((kk * m2[None, None, :]) % n) / n
    mc, ms = np.cos(th), np.sin(th)
    mfwd = np.concatenate([np.concatenate([mc, ms], axis=2), np.concatenate([-ms, mc], axis=2)], axis=1)
    mct, mst = np.transpose(mc, (0, 2, 1)), np.transpose(ms, (0, 2, 1))
    minv = np.concatenate([np.concatenate([mct, -mst], axis=2), np.concatenate([mst, mct], axis=2)], axis=1)
    eye = np.eye(8)
    fa8 = np.einsum('kn,jJ->jknJ', fa, eye).reshape(8 * kp, 8 * h1)
    ga8 = np.einsum('nk,jJ->njkJ', ga, eye).reshape(8 * h1, 8 * ks)
    return tuple(jnp.asarray(t, BF16) for t in (fa8, ga8, mfwd, minv))


def _hy_ctx_kernel(v_ref, x1_ref, x2_ref, fc_ref, fs_ref, gc_ref, gs_ref, ts_ref, td_ref, bias_ref, z_ref):
    d = functools.partial(jnp.dot, preferred_element_type=F32)
    w = v_ref.shape[2]
    kr_all = d(fc_ref[...], ts_ref[...].astype(BF16))
    ki_all = d(fs_ref[...], td_ref[...].astype(BF16))

    def conv(u, o):
        ub = u.astype(BF16)
        cr, ci = d(fc_ref[...], ub), d(fs_ref[...], ub)
        kr, ki = kr_all[:, o * w:(o + 1) * w], ki_all[:, o * w:(o + 1) * w]
        pr = (cr * kr - ci * ki).astype(BF16)
        pi = (cr * ki + ci * kr).astype(BF16)
        return d(gc_ref[...], pr) + d(gs_ref[...], pi) + u * bias_ref[o]

    z = x1_ref[0].astype(F32) * conv(v_ref[0].astype(F32), 0)
    z = x2_ref[0].astype(F32) * conv(z, 1)
    for j in range(2):
        z_ref[0, j] = z[:, j * 128:(j + 1) * 128]


def _hyena_ctx(v, x1, x2, fc, fs, gc, gs, tsum, tdiff, bias, seq, ctx_len):
    bsz, _, w = v.shape
    blk = seq // ctx_len
    kpad = fc.shape[0]
    tok = pl.BlockSpec((1, ctx_len, w), lambda b: (b, blk, 0))
    full = lambda shp: pl.BlockSpec(shp, lambda b: (0,) * len(shp))
    return pl.pallas_call(
        _hy_ctx_kernel, grid=(bsz,),
        in_specs=[tok, tok, tok, full((kpad, ctx_len)), full((kpad, ctx_len)), full((ctx_len, kpad)),
                  full((ctx_len, kpad)), full((ctx_len, HY_ORDER * w)), full((ctx_len, HY_ORDER * w)), full((2, 1, w))],
        out_specs=pl.BlockSpec((1, 2, ctx_len, 128), lambda b: (b, 0, 0, 0)),
        out_shape=jax.ShapeDtypeStruct((bsz, 2, ctx_len, 128), F32),
        compiler_params=_cparams(("parallel",)),
        name="hyena_ctx",
    )(v, x1, x2, fc, fs, gc, gs, tsum, tdiff, bias)


def _hyena_ctx_consts(ctx_len):
    n = 2 * ctx_len
    nk = ctx_len + 1
    kpad = -(-nk // 128) * 128
    k = np.arange(nk)
    m = np.arange(ctx_len)
    ang = 2.0 * np.pi * np.outer(k, m) / n
    fc = np.zeros((kpad, ctx_len))
    fs = np.zeros((kpad, ctx_len))
    fc[:nk] = np.cos(ang)
    fs[:nk] = -np.sin(ang)
    wgt = np.where((k == 0) | (k == ctx_len), 1.0, 2.0) / n
    gc = np.zeros((ctx_len, kpad))
    gs = np.zeros((ctx_len, kpad))
    gc[:, :nk] = (np.cos(ang) * wgt[:, None]).T
    gs[:, :nk] = (-np.sin(ang) * wgt[:, None]).T
    return tuple(jnp.asarray(t, BF16) for t in (fc, fs, gc, gs))


def _hyena_filter_taps(length, w1, b1, w2, b2, w3, freq, decay):
    pos = jnp.arange(length, dtype=F32)
    t = pos / max(length - 1, 1)
    bands = jnp.linspace(1e-4, HY_BANDS - 1, HY_BANDS, dtype=F32)
    ang = (2.0 * math.pi / length) * pos[:, None] * bands
    feats = jnp.concatenate([t[:, None], jnp.cos(ang), -jnp.sin(ang)], axis=-1)
    freq = freq.astype(F32)
    mm = functools.partial(jnp.matmul, precision=HI)
    hid = jnp.sin(freq * (mm(feats, w1.astype(F32)) + b1.astype(F32)))
    hid = jnp.sin(freq * (mm(hid, w2.astype(F32)) + b2.astype(F32)))
    h = mm(hid, w3.astype(F32)) * jnp.exp(-t[:, None] * jnp.abs(decay.astype(F32)))
    half = HY_ORDER * HY_WIDTH
    h_fwd, h_bwd = h[:, :half], h[:, half:]
    l1 = (jnp.abs(h_fwd[0] + h_bwd[0]) + jnp.sum(jnp.abs(h_fwd[1:]), axis=0) + jnp.sum(jnp.abs(h_bwd[1:]), axis=0))
    return (h_fwd + h_bwd) / l1, (h_fwd - h_bwd) / l1


def _mixout_kernel(nt_lat, n_sub, lat_only, *refs):
    if lat_only:
        (xl_ref, m_ref, s5y_ref, yf_ref, yb_ref, xs_ref, z_ref, hyl_ref,
         wglu_ref, bglu_ref, vec512_ref, wout_ref, win_ref, wout2_ref, ln_ref, o_ref) = refs
    else:
        (xl_ref, xc_ref, m_ref, s5y_ref, yf_ref, yb_ref, xs_ref, z_ref, hyl_ref, hyc_ref,
         wglu_ref, bglu_ref, vec512_ref, wout_ref, win_ref, wout2_ref, ln_ref, o_ref) = refs
        is_lat = pl.program_id(1) < nt_lat
    m = m_ref[0, 0]
    d = functools.partial(jnp.dot, preferred_element_type=F32)

    def first_sublayer(sub):
        r = slice(sub * ROW_TILE, (sub + 1) * ROW_TILE)
        if lat_only:
            x_in = xl_ref[0, r, :]
            hy = [hyl_ref[0, j, r, :].astype(BF16) for j in range(2)]
        else:
            x_in = jnp.where(is_lat, xl_ref[0, r, :], xc_ref[0, r, :])
            hy = [jnp.where(is_lat, hyl_ref[0, j, r, :], hyc_ref[0, j, r, :]).astype(BF16) for j in range(2)]
        y5 = _gelu_tanh(jnp.concatenate([s5y_ref[0, 0, r, :], s5y_ref[0, 1, r, :]], axis=1))
        y5 = y5 * _sigmoid(d(y5.astype(BF16), wglu_ref[...]) + bglu_ref[...])
        ys = (yf_ref[0, 0, r, :].astype(F32) + yb_ref[0, 0, r, :].astype(F32)
              + vec512_ref[0:1, :] * xs_ref[0, r, :].astype(F32))
        gsd = ys * _silu(z_ref[0, r, :].astype(F32))
        gsd = gsd * lax.rsqrt(jnp.mean(gsd * gsd, axis=-1, keepdims=True) + LN_EPS) * vec512_ref[1:2, :]
        mix = (d(y5.astype(BF16), wout_ref[0:256, :]) + d(gsd.astype(BF16), wout_ref[256:768, :])
               + d(hy[0], wout_ref[768:896, :]) + d(hy[1], wout_ref[896:1024, :]))
        x = _standardise(ALPHA * x_in + m[2:3] * mix) * ln_ref[0:1, :] + ln_ref[1:2, :]
        return x, (_standardise(x) * (1.0 + m[4:5]) + m[3:4]).astype(BF16)

    def ffn_in(h):
        return d(h, win_ref[:, 0:FFN_HIDDEN]), d(h, win_ref[:, FFN_HIDDEN:2 * FFN_HIDDEN])

    def ffn_out(sub, x, gate, up):
        act = (_silu(gate) * up).astype(BF16)
        x = _standardise(ALPHA * x + m[5:6] * d(act, wout2_ref[...]))
        o_ref[0, sub * ROW_TILE:(sub + 1) * ROW_TILE, :] = x * ln_ref[2:3, :] + ln_ref[3:4, :]

    staged = [first_sublayer(0)]
    for sub in range(n_sub):
        x, h = staged[sub]
        gate, up = ffn_in(h)
        if sub + 1 < n_sub:
            staged.append(first_sublayer(sub + 1))
        ffn_out(sub, x, gate, up)


def _mixout_ffn(x_lat, x_ctx, ctx_blk, mods, s5y, yssd, xbc_c, z, hy_lat, hy_ctx, wglu, bglu, vec512, wout, win,
                wout2, layer, ln, nt_lat, rows):
    bsz = x_lat.shape[0]
    lat_only = rows == nt_lat * ROW_TILE
    n_sub = MIX_SUB_TILES if lat_only and rows % (MIX_SUB_TILES * ROW_TILE) == 0 else 1
    tile = n_sub * ROW_TILE
    tok = lambda w: pl.BlockSpec((1, tile, w), lambda b, i: (b, i, 0))
    halves = pl.BlockSpec((1, 2, tile, 128), lambda b, i: (b, 0, i, 0))
    full = lambda shp: pl.BlockSpec(shp, lambda b, i: (0,) * len(shp))
    const = lambda shp: pl.BlockSpec((None,) + shp, lambda b, i: (layer, 0, 0), pipeline_mode=pl.Buffered(1))
    ssd_dir = lambda dirn: pl.BlockSpec((1, 1, tile, SSD_INNER), lambda b, i: (b, dirn, i, 0))
    if lat_only:
        x_specs, x_args = [tok(D_MODEL)], (x_lat,)
        hy_specs, hy_args = [halves], (hy_lat,)
        mod_spec = pl.BlockSpec((1, 1, 6, D_MODEL), lambda b, i: (b, 0, 0, 0))
    else:
        x_specs, x_args = _row_specs(nt_lat, ctx_blk, D_MODEL), (x_lat, x_ctx)
        hy_specs = [pl.BlockSpec((1, 2, tile, 128), lambda b, i: (b, 0, jnp.minimum(i, nt_lat - 1), 0)),
                    pl.BlockSpec((1, 2, tile, 128), lambda b, i: (b, 0, 0, 0))]
        hy_args = (hy_lat, hy_ctx)
        mod_spec = pl.BlockSpec((1, 1, 6, D_MODEL), lambda b, i: (b, jnp.where(i < nt_lat, 0, 1), 0, 0))
    return pl.pallas_call(
        functools.partial(_mixout_kernel, nt_lat, n_sub, lat_only), grid=(bsz, rows // tile),
        in_specs=x_specs + [mod_spec, halves, ssd_dir(0), ssd_dir(1), tok(SSD_INNER), tok(SSD_INNER)] + hy_specs + [
                  full((256, 256)), full((1, 256)), full((2, 512)),
                  const((D_MODEL, D_MODEL)), const((D_MODEL, 2 * FFN_HIDDEN)), const((FFN_HIDDEN, D_MODEL)),
                  full((4, D_MODEL))],
        out_specs=tok(D_MODEL),
        out_shape=jax.ShapeDtypeStruct((bsz, rows, D_MODEL), F32),
        compiler_params=_cparams(("parallel", "parallel"), V7X_VMEM_LIMIT_MB),
        name="mixout_ffn",
    )(*x_args, mods, s5y, yssd, yssd, xbc_c, z, *hy_args, wglu, bglu, vec512, wout, win, wout2, ln)


def _layer(x_lat, x_ctx, ctx_blk, mods, p, seq, ctx_len, want_ctx):
    nt_lat = seq // ROW_TILE
    t_all = seq + ctx_len
    layer = p["layer"]
    s5u, z, xbc_c, v, x1, x2, dt_raw = _inproj(x_lat, x_ctx, ctx_blk, mods, p["w_in"], layer, p["ssd_conv_w"],
                                               p["ssd_conv_b"], p["hy_conv_w"], p["hy_conv_b"], nt_lat)
    s5y = _s5_scan(s5u, p["s5_ncat"], p["s5_tz"], p["s5_mcat"], p["s5_coef"], p["s5_dvec"],
                   seq // S5_CHUNK, ctx_len // S5_CHUNK)
    yssd = _ssd_scan(xbc_c, dt_raw, p["ssd_tri"], p["ssd_par"], p["ssd_expand"],
                     seq // SSD_CHUNK, ctx_len // SSD_CHUNK)
    hy_lat = _hyena_lat(v, x1, x2, p["hy_fa"], p["hy_ga"], p["hy_mfwd"], p["hy_minv"],
                        p["hy_kr"], p["hy_ki"], p["hy_bias"], seq)
    if want_ctx:
        hy_ctx = _hyena_ctx(v, x1, x2, *p["hy_ctx_mats"], *p["hy_ctx_taps"], p["hy_bias"], seq, ctx_len)
        rows = t_all
    else:
        hy_ctx, rows = hy_lat, seq
    return _mixout_ffn(x_lat, x_ctx, ctx_blk, mods, s5y, yssd, xbc_c, z, hy_lat, hy_ctx, p["s5_wglu"], p["s5_bglu"],
                       p["ssd_vec"], p["w_out"], p["ffn_w_in"], p["ffn_w_out"], layer, p["ln"], nt_lat, rows)


def kernel(x, c, ctx, c_ctx, w_mod, b_mod, w_in, s5_lam_re, s5_lam_im, s5_log_step, s5_b_re, s5_b_im, s5_c_re, s5_c_im, s5_d, s5_w_glu, s5_b_glu, ssd_conv_w, ssd_conv_b, ssd_dt_bias, ssd_a_log, ssd_d, ssd_norm_w, hy_conv_w, hy_conv_b, hy_w1, hy_b1, hy_w2, hy_b2, hy_w3, hy_freq, hy_decay, hy_bias, w_out, ln1_g, ln1_b, ffn_w_in, ffn_w_out, ln2_g, ln2_b):
    bsz, seq, _ = x.shape
    ctx_len = ctx.shape[1]
    assert bsz == 8 and seq % ROW_TILE == 0 and ctx_len == ROW_TILE

    fa, ga, mfwd, minv = _hyena_lat_consts(seq)
    ctx_mats = _hyena_ctx_consts(ctx_len)
    tt = np.arange(SSD_CHUNK)
    tri = jnp.asarray(np.stack([tt[None, :] <= tt[:, None], tt[None, :] >= tt[:, None]]), BF16)
    expand = jnp.asarray(np.repeat(np.eye(128, SSD_HEADS), SSD_HEADDIM, axis=1)[:, :SSD_INNER], BF16)
    cvec = jnp.zeros((16, D_MODEL), F32).at[:bsz].set(c.astype(F32)).at[bsz].set(c_ctx.astype(F32))

    w_p = jnp.concatenate([w_in[..., 0:REF_COL_DT], w_in[..., REF_COL_HY:MIX_IN], w_in[..., REF_COL_DT:REF_COL_HY],
                           jnp.zeros((DEPTH, D_MODEL, MIX_IN_PAD - MIX_IN), w_in.dtype)], axis=-1).astype(BF16)
    w_out_b, ffn_in_b, ffn_out_b = w_out.astype(BF16), ffn_w_in.astype(BF16), ffn_w_out.astype(BF16)

    x_lat, x_ctx, ctx_blk = x.astype(F32), ctx.astype(F32), 0
    for l in range(DEPTH):
        want_ctx = l < DEPTH - 1
        mod16 = _modulation(cvec, w_mod.astype(F32), b_mod[l].astype(F32).reshape(1, -1), l)
        mods = jnp.stack([mod16[:bsz].reshape(bsz, 6, D_MODEL),
                          jnp.broadcast_to(mod16[bsz].reshape(1, 6, D_MODEL), (bsz, 6, D_MODEL))], axis=1)
        ncat, tz, mcat, coef = _s5_weights(s5_lam_re[l], s5_lam_im[l], s5_log_step[l], s5_b_re[l], s5_b_im[l],
                                           s5_c_re[l], s5_c_im[l])
        rep = lambda t: jnp.repeat(t.astype(F32), SSD_HEADDIM, axis=-1)
        par = jnp.zeros((2, 8, 128), F32)
        par = par.at[:, 0, :SSD_HEADS].set(ssd_dt_bias[l].astype(F32))
        par = par.at[:, 1, :SSD_HEADS].set(-jnp.exp(ssd_a_log[l].astype(F32)))
        hy_args = (hy_w1[l], hy_b1[l], hy_w2[l], hy_b2[l], hy_w3[l], hy_freq[l], hy_decay[l])
        kr, ki = _hyena_spectrum(*_hyena_filter_taps(seq, *hy_args), fa, mfwd)
        p = dict(
            layer=l, w_in=w_p, s5_ncat=ncat, s5_tz=tz, s5_mcat=mcat, s5_coef=coef,
            ssd_conv_w=ssd_conv_w[l].astype(F32), ssd_conv_b=ssd_conv_b[l].astype(F32),
            ssd_tri=tri, ssd_par=par, ssd_expand=expand,
            hy_conv_w=hy_conv_w[l].astype(F32), hy_conv_b=hy_conv_b[l].astype(F32),
            hy_fa=fa, hy_ga=ga, hy_mfwd=mfwd, hy_minv=minv, hy_kr=kr, hy_ki=ki,
            hy_bias=hy_bias[l].astype(F32)[:, None, :],
            s5_wglu=s5_w_glu[l].astype(BF16),
            s5_bglu=s5_b_glu[l].astype(F32).reshape(1, S5_WIDTH),
            s5_dvec=jnp.tile(s5_d[l].astype(F32).reshape(S5_GROUPS, 1, S5_GROUP_CH), (1, 1, S5_CHUNK)),
            ssd_vec=jnp.stack([rep(ssd_d[l]), ssd_norm_w[l].astype(F32)]),
            w_out=w_out_b, ln=jnp.stack([ln1_g[l], ln1_b[l], ln2_g[l], ln2_b[l]]).astype(F32),
            ffn_w_in=ffn_in_b, ffn_w_out=ffn_out_b,
        )
        if want_ctx:
            p.update(hy_ctx_mats=ctx_mats, hy_ctx_taps=_hyena_filter_taps(ctx_len, *hy_args))
        x_lat = _layer(x_lat, x_ctx, ctx_blk, mods, p, seq, ctx_len, want_ctx)
        x_ctx, ctx_blk = x_lat, seq // ROW_TILE
    return x_lat.astype(x.dtype)
```
